```python
import jax, jax.numpy as jnp
from jax import lax
import numpy as np

D_MODEL = 1024
BATCH = 4
SEQ = 4096
DEPTH = 4

GRID_W = 64
CTX_LEN = 256
N_MIXERS = 3
N_MOD = 9
D_FF = 2816
EPS = 1e-6
N_FOURIER_GROUPS = 4
D_CONV = D_MODEL
CONV_K = 31
N_HEADS = 16
HEAD_DIM = D_MODEL // N_HEADS
WIN_R = 8
WIN_C = 16
Q_BLOCK_C = 16
K_BAND_C = 32
N_COL_BLOCKS = GRID_W // Q_BLOCK_C
NEG_INF = -1e30
N_LAYERS_A = len(range(0, DEPTH, N_MIXERS))
N_LAYERS_B = len(range(1, DEPTH, N_MIXERS))
N_LAYERS_C = len(range(2, DEPTH, N_MIXERS))

kernel_name = "hybrid_fnet_conformer_natten_prefix_dit"


def rms_norm(x):
    xf = x.astype(jnp.float32)
    return (xf * lax.rsqrt(jnp.mean(xf * xf, axis=-1, keepdims=True) + EPS)).astype(x.dtype)


def head_rms_norm(t, gain):
    tf = t.astype(jnp.float32)
    return (tf * lax.rsqrt(jnp.mean(tf * tf, axis=-1, keepdims=True) + EPS)).astype(t.dtype) * gain


def modulate(h, shift, scale):
    return h * (1 + scale) + shift


def swiglu(h, w_in, w_out):
    g, u = jnp.split(h @ w_in, 2, axis=-1)
    return (jax.nn.silu(g) * u) @ w_out


def fourier_mix(h, w_out, b_out):
    b, l, d = h.shape
    hg = h.astype(jnp.float32).reshape(b, l, N_FOURIER_GROUPS, d // N_FOURIER_GROUPS)
    f = jnp.fft.fft2(hg, axes=(1, 3), norm="ortho").real
    return f.reshape(b, l, d).astype(h.dtype) @ w_out + b_out


def conformer_conv(h, w_in, b_in, w_dw, b_dw, ln_g, ln_b, w_out, b_out):
    a, g = jnp.split(h @ w_in + b_in, 2, axis=-1)
    u = a * jax.nn.sigmoid(g)
    u = lax.conv_general_dilated(
        u, w_dw[:, None, :], window_strides=(1,),
        padding=[(CONV_K // 2, CONV_K // 2)],
        dimension_numbers=("NWC", "WIO", "NWC"),
        feature_group_count=D_CONV) + b_dw
    uf = u.astype(jnp.float32)
    mu = jnp.mean(uf, axis=-1, keepdims=True)
    var = jnp.mean(jnp.square(uf - mu), axis=-1, keepdims=True)
    u = ((uf - mu) * lax.rsqrt(var + EPS)).astype(u.dtype) * ln_g + ln_b
    return jax.nn.silu(u) @ w_out + b_out


def neighbourhood_attention(a, ac, w_qkv, q_gain, k_gain, rpb, w_o, ctx_out):
    b, l, d = a.shape
    rows = l // GRID_W
    kr = min(WIN_R, rows)
    n_lat = kr * K_BAND_C
    scale = HEAD_DIM ** -0.5
    H, hd, ncb, qbc, kbc = N_HEADS, HEAD_DIM, N_COL_BLOCKS, Q_BLOCK_C, K_BAND_C

    q, k, v = jnp.split(a @ w_qkv, 3, axis=-1)
    q = head_rms_norm(q.reshape(b, l, H, hd), q_gain)
    k = head_rms_norm(k.reshape(b, l, H, hd), k_gain)
    v = v.reshape(b, l, H, hd)
    qc_, kc, vc = jnp.split(ac @ w_qkv, 3, axis=-1)
    n_ctx = ac.shape[1]
    kc = head_rms_norm(kc.reshape(b, n_ctx, H, hd), k_gain)
    vc = vc.reshape(b, n_ctx, H, hd)

    jb = np.arange(ncb)
    band0 = np.clip(jb * qbc - WIN_C // 2, 0, GRID_W - kbc)
    band_cols = band0[:, None] + np.arange(kbc)
    q_cols = jb[:, None] * qbc + np.arange(qbc)
    win0 = np.clip(q_cols - WIN_C // 2, 0, GRID_W - WIN_C)
    kcol = band_cols[:, None, None, :]
    w0 = win0[:, :, None, None]
    col_ok = np.broadcast_to((kcol >= w0) & (kcol < w0 + WIN_C), (ncb, qbc, kr, kbc)).reshape(ncb, qbc, n_lat)
    dc_idx = np.broadcast_to(np.clip(kcol - q_cols[:, :, None, None] + WIN_C - 1, 0, 2 * WIN_C - 2),
                             (ncb, qbc, kr, kbc))

    q_grid = q.reshape(b, rows, ncb, qbc, H, hd)
    k_grid = k.reshape(b, rows, GRID_W, H, hd)
    v_grid = v.reshape(b, rows, GRID_W, H, hd)

    def row_block(r):
        rs = jnp.clip(r - kr // 2, 0, rows - kr)
        q_r = lax.dynamic_index_in_dim(q_grid, r, axis=1, keepdims=False)

        def band(t):
            t = lax.dynamic_slice_in_dim(t, rs, kr, axis=1)[:, :, band_cols]
            return jnp.moveaxis(t, 2, 1).reshape(b, ncb, n_lat, H, hd)

        k_r, v_r = band(k_grid), band(v_grid)
        dr_idx = (rs - r + WIN_R - 1) + jnp.arange(kr)
        bias = rpb[:, dr_idx[None, None, :, None], dc_idx].reshape(H, ncb, qbc, n_lat).astype(jnp.float32)
        s_lat = jnp.einsum('bjqhd,bjnhd->bhjqn', q_r, k_r).astype(jnp.float32) * scale + bias
        s_lat = jnp.where(col_ok, s_lat, NEG_INF)
        s_ctx = jnp.einsum('bjqhd,bchd->bhjqc', q_r, kc).astype(jnp.float32) * scale
        p = jax.nn.softmax(jnp.concatenate([s_lat, s_ctx], axis=-1), axis=-1).astype(v.dtype)
        return (jnp.einsum('bhjqn,bjnhd->bjqhd', p[..., :n_lat], v_r)
                + jnp.einsum('bhjqc,bchd->bjqhd', p[..., n_lat:], vc))

    o = lax.map(row_block, jnp.arange(rows))
    y = jnp.moveaxis(o, 0, 1).reshape(b, l, d) @ w_o

    y_ctx = None
    if ctx_out:
        qc = head_rms_norm(qc_.reshape(b, n_ctx, H, hd), q_gain)
        s = jnp.einsum('bqhd,bkhd->bhqk', qc, kc).astype(jnp.float32) * scale
        pc = jax.nn.softmax(s, axis=-1).astype(vc.dtype)
        y_ctx = jnp.einsum('bhqk,bkhd->bqhd', pc, vc).reshape(b, n_ctx, d) @ w_o
    return y, y_ctx


def setup_inputs(seed: int = 0) -> dict:
    key = jax.random.key(seed)
    ks = jax.random.split(key, 24)
    D = D_MODEL

    def nrm(k, shape, s):
        return jax.random.normal(k, shape, jnp.float32) * s

    return {
        "x": nrm(ks[0], (BATCH, SEQ, D), 1.0),
        "c": nrm(ks[1], (BATCH, D), 1.0),
        "ctx": nrm(ks[2], (BATCH, CTX_LEN, D), 1.0),
        "c_ctx": nrm(ks[3], (D,), 1.0),
        "w_mod": nrm(ks[4], (DEPTH, D, N_MOD * D), 0.5 * D ** -0.5),
        "b_mod": nrm(ks[5], (DEPTH, N_MOD * D), 0.02),
        "w_ff_in": nrm(ks[6], (DEPTH, 2, D, 2 * D_FF), D ** -0.5),
        "w_ff_out": nrm(ks[7], (DEPTH, 2, D_FF, D), D_FF ** -0.5),
        "fnet_w_out": nrm(ks[8], (N_LAYERS_A, D, D), D ** -0.5),
        "fnet_b_out": nrm(ks[9], (N_LAYERS_A, D), 0.02),
        "conv_w_in": nrm(ks[10], (N_LAYERS_B, D, 2 * D_CONV), D ** -0.5),
        "conv_b_in": nrm(ks[11], (N_LAYERS_B, 2 * D_CONV), 0.02),
        "conv_w_dw": nrm(ks[12], (N_LAYERS_B, CONV_K, D_CONV), CONV_K ** -0.5),
        "conv_b_dw": nrm(ks[13], (N_LAYERS_B, D_CONV), 0.02),
        "conv_ln_g": 1.0 + nrm(ks[14], (N_LAYERS_B, D_CONV), 0.02),
        "conv_ln_b": nrm(ks[15], (N_LAYERS_B, D_CONV), 0.02),
        "conv_w_out": nrm(ks[16], (N_LAYERS_B, D_CONV, D), D_CONV ** -0.5),
        "conv_b_out": nrm(ks[17], (N_LAYERS_B, D), 0.02),
        "na_w_qkv": nrm(ks[18], (N_LAYERS_C, D, 3 * D), D ** -0.5),
        "na_q_gain": 1.0 + nrm(ks[19], (N_LAYERS_C, HEAD_DIM), 0.02),
        "na_k_gain": 1.0 + nrm(ks[20], (N_LAYERS_C, HEAD_DIM), 0.02),
        "na_rpb": nrm(ks[21], (N_LAYERS_C, N_HEADS, 2 * WIN_R - 1, 2 * WIN_C - 1), 0.1),
        "na_w_o": nrm(ks[22], (N_LAYERS_C, D, D), D ** -0.5),
    }


def reference(x, c, ctx, c_ctx, w_mod, b_mod, w_ff_in, w_ff_out, fnet_w_out, fnet_b_out,
              conv_w_in, conv_b_in, conv_w_dw, conv_b_dw, conv_ln_g, conv_ln_b, conv_w_out, conv_b_out,
              na_w_qkv, na_q_gain, na_k_gain, na_rpb, na_w_o):
    silu_c = jax.nn.silu(c)
    silu_cc = jax.nn.silu(c_ctx)
    h, hc = x, ctx
    for i in range(DEPTH):
        kind = i % N_MIXERS
        slot = i // N_MIXERS
        ctx_post = i < DEPTH - 1
        ctx_pre = ctx_post or kind == 2
        m = jnp.split((silu_c @ w_mod[i] + b_mod[i])[:, None, :], N_MOD, axis=-1)
        mc = jnp.split((silu_cc @ w_mod[i] + b_mod[i])[None, None, :], N_MOD, axis=-1)

        h = h + 0.5 * m[2] * swiglu(modulate(rms_norm(h), m[0], m[1]), w_ff_in[i, 0], w_ff_out[i, 0])
        if ctx_pre:
            hc = hc + 0.5 * mc[2] * swiglu(modulate(rms_norm(hc), mc[0], mc[1]), w_ff_in[i, 0], w_ff_out[i, 0])

        a = modulate(rms_norm(h), m[3], m[4])
        yc = None
        if kind == 0:
            y = fourier_mix(a, fnet_w_out[slot], fnet_b_out[slot])
            if ctx_post:
                yc = fourier_mix(modulate(rms_norm(hc), mc[3], mc[4]), fnet_w_out[slot], fnet_b_out[slot])
        elif kind == 1:
            cp = (conv_w_in[slot], conv_b_in[slot], conv_w_dw[slot], conv_b_dw[slot],
                  conv_ln_g[slot], conv_ln_b[slot], conv_w_out[slot], conv_b_out[slot])
            y = conformer_conv(a, *cp)
            if ctx_post:
                yc = conformer_conv(modulate(rms_norm(hc), mc[3], mc[4]), *cp)
        else:
            ac = modulate(rms_norm(hc), mc[3], mc[4])
            y, yc = neighbourhood_attention(a, ac, na_w_qkv[slot], na_q_gain[slot], na_k_gain[slot],
                                            na_rpb[slot], na_w_o[slot], ctx_post)
        h = h + m[5] * y
        if ctx_post:
            hc = hc + mc[5] * yc

        h = h + 0.5 * m[8] * swiglu(modulate(rms_norm(h), m[6], m[7]), w_ff_in[i, 1], w_ff_out[i, 1])
        if ctx_post:
            hc = hc + 0.5 * mc[8] * swiglu(modulate(rms_norm(hc), mc[6], mc[7]), w_ff_in[i, 1], w_ff_out[i, 1])
    return h
```

```python
import functools

import numpy as np
import jax
import jax.numpy as jnp
from jax import lax
from jax.experimental import pallas as pl
from jax.experimental.pallas import tpu as pltpu

F32 = jnp.float32
BF16 = jnp.bfloat16

D = 1024
D_FF = 2816
N_MOD = 9
EPS = 1e-6
GRID_W = 64
LANE = 128
N_GROUPS = 4
GROUP_C = D // N_GROUPS
CONV_K = 31
CONV_HALO = 16
N_HEADS = 16
HEAD_DIM = D // N_HEADS
WIN_R = 8
WIN_C = 16
NEG_INF = -1e30
MOD_ROWS = 8

FF_TILE = 256
TM_FFN = 1024
TM_LIN = 512
TM_CONV = 256
COL_TILE = 16
VMEM_LIMIT = 52 * 1024 * 1024


def _cparams(sem):
    return pltpu.CompilerParams(dimension_semantics=sem, vmem_limit_bytes=VMEM_LIMIT)


def _silu(x):
    return x / (1.0 + jnp.exp(-x))


def _norm_mod(x, shift, scale):
    ms = jnp.mean(x * x, axis=-1, keepdims=True)
    return x * lax.rsqrt(ms + EPS) * (1.0 + scale) + shift


def _dot(a, b):
    return jnp.dot(a, b, preferred_element_type=F32)


def _mod_kernel(c_ref, w_ref, b_ref, o_ref):
    s = _silu(c_ref[...]).astype(BF16)
    o_ref[...] = _dot(s, w_ref[...].astype(BF16)) + b_ref[...]


def _mod_call(cond, w_mod, b_mod):
    depth = w_mod.shape[0]
    return pl.pallas_call(
        _mod_kernel,
        grid=(depth, N_MOD),
        in_specs=[
            pl.BlockSpec((MOD_ROWS, D), lambda l, k: (0, 0)),
            pl.BlockSpec((None, D, D), lambda l, k: (l, 0, k)),
            pl.BlockSpec((None, 1, D), lambda l, k: (l, 0, k)),
        ],
        out_specs=pl.BlockSpec((None, MOD_ROWS, D), lambda l, k: (l, 0, k)),
        out_shape=jax.ShapeDtypeStruct((depth, MOD_ROWS, N_MOD * D), F32),
        compiler_params=_cparams(("arbitrary", "arbitrary")),
        name="mod",
    )(cond, w_mod, b_mod.reshape(depth, 1, N_MOD * D))


def _mod_row(tile, tiles_per_batch, n_batch):
    return jnp.minimum(tile // tiles_per_batch, n_batch)


def _ffn_kernel(h_ref, mod_ref, wg_ref, wu_ref, wo_ref, o_ref, xn_ref, acc_ref, *, k0, n_ff):
    j = pl.program_id(1)

    @pl.when(j == 0)
    def _():
        xn_ref[...] = _norm_mod(h_ref[...], mod_ref[k0:k0 + 1, :], mod_ref[k0 + 1:k0 + 2, :]).astype(BF16)

    xn = xn_ref[...]
    g = _dot(xn, wg_ref[...].astype(BF16))
    u = _dot(xn, wu_ref[...].astype(BF16))
    y = _dot((_silu(g) * u).astype(BF16), wo_ref[...].astype(BF16))

    @pl.when(j == 0)
    def _():
        acc_ref[...] = y

    @pl.when(j > 0)
    def _():
        acc_ref[...] += y

    @pl.when(j == n_ff - 1)
    def _():
        o_ref[...] = h_ref[...] + (0.5 * mod_ref[k0 + 2:k0 + 3, :]) * acc_ref[...]


def _ffn_call(hs, mod_l, w_ff_in, w_ff_out, layer, which, k0, n_tiles, n_batch, seq, alias):
    n_ff = D_FF // FF_TILE
    tpb = seq // TM_FFN
    out_rows = hs.shape[0] if alias else n_tiles * TM_FFN
    return pl.pallas_call(
        functools.partial(_ffn_kernel, k0=k0, n_ff=n_ff),
        grid=(n_tiles, n_ff),
        in_specs=[
            pl.BlockSpec((TM_FFN, D), lambda i, j: (i, 0)),
            pl.BlockSpec((None, N_MOD, D), lambda i, j: (_mod_row(i, tpb, n_batch), 0, 0)),
            pl.BlockSpec((None, None, D, FF_TILE), lambda i, j: (layer, which, 0, j)),
            pl.BlockSpec((None, None, D, FF_TILE), lambda i, j: (layer, which, 0, n_ff + j)),
            pl.BlockSpec((None, None, FF_TILE, D), lambda i, j: (layer, which, j, 0)),
        ],
        out_specs=pl.BlockSpec((TM_FFN, D), lambda i, j: (i, 0)),
        out_shape=jax.ShapeDtypeStruct((out_rows, D), F32),
        scratch_shapes=[pltpu.VMEM((TM_FFN, D), BF16), pltpu.VMEM((TM_FFN, D), F32)],
        input_output_aliases={0: 0} if alias else {},
        compiler_params=_cparams(("arbitrary", "arbitrary")),
        name="ffn",
    )(hs, mod_l, w_ff_in, w_ff_in, w_ff_out)


def _lin_res_kernel(*refs, fourier):
    if fourier:
        xr_ref, xi_ref, cc_ref, sc_ref, w_ref, b_ref, h_ref, mod_ref, o_ref = refs
        parts = []
        for g in range(N_GROUPS):
            sl = slice(g * GROUP_C, (g + 1) * GROUP_C)
            parts.append(_dot(xr_ref[:, sl], cc_ref[...]) + _dot(xi_ref[:, sl], sc_ref[...]))
        x = jnp.concatenate(parts, axis=-1).astype(BF16)
    else:
        x_ref, w_ref, b_ref, h_ref, mod_ref, o_ref = refs
        x = x_ref[...]
    y = _dot(x, w_ref[...]) + b_ref[...]
    o_ref[...] = h_ref[...] + mod_ref[5:6, :] * y


def _lin_res_call(hs, xs, tables, w, b, mod_l, tile0, n_tiles, mod_row_fn):
    fourier = len(xs) == 2
    x_spec = pl.BlockSpec((TM_LIN, D), lambda i: (i, 0))
    t_spec = pl.BlockSpec((GROUP_C, GROUP_C), lambda i: (0, 0))
    in_specs = [x_spec] * len(xs) + [t_spec] * len(tables) + [
        pl.BlockSpec((D, D), lambda i: (0, 0)),
        pl.BlockSpec((1, D), lambda i: (0, 0)),
        pl.BlockSpec((TM_LIN, D), lambda i: (i + tile0, 0)),
        pl.BlockSpec((None, N_MOD, D), lambda i: (mod_row_fn(i + tile0), 0, 0)),
    ]
    h_index = len(xs) + len(tables) + 2
    return pl.pallas_call(
        functools.partial(_lin_res_kernel, fourier=fourier),
        grid=(n_tiles,),
        in_specs=in_specs,
        out_specs=pl.BlockSpec((TM_LIN, D), lambda i: (i + tile0, 0)),
        out_shape=jax.ShapeDtypeStruct(hs.shape, F32),
        input_output_aliases={h_index: 0},
        compiler_params=_cparams(("arbitrary",)),
        name="lin_res_fourier" if fourier else "lin_res",
    )(*xs, *tables, w, b.reshape(1, D), hs, mod_l)


def _put_cols(ref, val):
    for lt in range(D // LANE):
        ref[lt] = val[:, lt * LANE:(lt + 1) * LANE]


def _get_cols(ref):
    return jnp.concatenate([ref[lt] for lt in range(D // LANE)], axis=-1)


def _get_rows_strided(ref, start, n, stride):
    rows = pl.ds(start, n, stride=stride)
    return jnp.concatenate([ref[lt, rows, :] for lt in range(D // LANE)], axis=-1)


def _put_rows_strided(ref, start, n, stride, val):
    rows = pl.ds(start, n, stride=stride)
    for lt in range(D // LANE):
        ref[lt, rows, :] = val[:, lt * LANE:(lt + 1) * LANE]


def _fnet_s1_kernel(h_ref, mod_ref, f1_ref, zr_ref, zi_ref, xs_ref, zrs_ref, zis_ref):
    rows = h_ref.shape[0]
    xn = _norm_mod(h_ref[...], mod_ref[3:4, :], mod_ref[4:5, :])
    _put_cols(xs_ref, xn.reshape(rows * COL_TILE, D))
    for ci in range(COL_TILE):
        z = _dot(f1_ref[...], _get_rows_strided(xs_ref, ci, rows, COL_TILE).astype(BF16))
        _put_rows_strided(zrs_ref, ci, rows, COL_TILE, z[:rows])
        _put_rows_strided(zis_ref, ci, rows, COL_TILE, z[rows:])
    zr_ref[...] = _get_cols(zrs_ref).reshape(rows, COL_TILE, D).astype(BF16)
    zi_ref[...] = _get_cols(zis_ref).reshape(rows, COL_TILE, D).astype(BF16)


def _fnet_s2_kernel(zr_ref, zi_ref, m2_ref, xr_ref, xi_ref, xrs_ref, xis_ref):
    rows = zr_ref.shape[1]
    for ki in range(COL_TILE):
        z = jnp.concatenate([zr_ref[ki], zi_ref[ki]], axis=0)
        x = _dot(m2_ref[ki], z)
        _put_rows_strided(xrs_ref, ki, rows, COL_TILE, x[:rows])
        _put_rows_strided(xis_ref, ki, rows, COL_TILE, x[rows:])
    xr_ref[...] = _get_cols(xrs_ref).reshape(rows, COL_TILE, D).astype(BF16)
    xi_ref[...] = _get_cols(xis_ref).reshape(rows, COL_TILE, D).astype(BF16)


def _fnet_ctx_kernel(h_ref, mod_ref, f_ref, xr_ref, xi_ref):
    n = h_ref.shape[0]
    xn = _norm_mod(h_ref[...], mod_ref[3:4, :], mod_ref[4:5, :]).astype(BF16)
    x = _dot(f_ref[...], xn)
    xr_ref[...] = x[:n].astype(BF16)
    xi_ref[...] = x[n:].astype(BF16)


def _dft_tables(rows):
    n_seq = rows * GRID_W
    a = np.arange(rows)
    th1 = 2.0 * np.pi * np.outer(a, a) / rows
    f1 = np.concatenate([np.cos(th1), -np.sin(th1)], axis=0)
    k2 = np.arange(GRID_W)[:, None, None]
    k1 = np.arange(rows)[None, :, None]
    n1 = np.arange(GRID_W)[None, None, :]
    th2 = 2.0 * np.pi * (n1 * (GRID_W * k1 + k2) % n_seq) / n_seq
    c2, s2 = np.cos(th2), np.sin(th2)
    m2 = np.concatenate([np.concatenate([c2, s2], axis=2), np.concatenate([-s2, c2], axis=2)], axis=1)
    return f1.astype(np.float32), m2.astype(np.float32)


def _channel_tables(n_seq):
    c = np.arange(GROUP_C)
    th = 2.0 * np.pi * np.outer(c, c) / GROUP_C
    s = 1.0 / np.sqrt(float(n_seq) * GROUP_C)
    return (np.cos(th) * s).astype(np.float32), (np.sin(th) * s).astype(np.float32)


def _fourier_mixer(hs, mod_l, w_out, b_out, n_batch, seq, n_ctx, with_ctx):
    rows = seq // GRID_W
    assert rows == GRID_W, "the two-stage DFT is written for a square token grid"
    n_lat = n_batch * seq
    f1, m2 = _dft_tables(rows)
    cc, sc = _channel_tables(seq)
    f1, m2, cc, sc = (jnp.asarray(t).astype(BF16) for t in (f1, m2, cc, sc))
    w = w_out.astype(BF16)
    n_ct = GRID_W // COL_TILE
    grid3 = hs.reshape(hs.shape[0] // GRID_W, GRID_W, D)
    z_shape = jax.ShapeDtypeStruct((n_batch * rows, GRID_W, D), BF16)
    blk_cols = pl.BlockSpec((rows, COL_TILE, D), lambda b, c: (b, c, 0))
    blk_rows = pl.BlockSpec((COL_TILE, GRID_W, D), lambda b, c: (b * n_ct + c, 0, 0))
    scr = pltpu.VMEM((D // LANE, rows * COL_TILE, LANE), F32)
    zr, zi = pl.pallas_call(
        _fnet_s1_kernel,
        grid=(n_batch, n_ct),
        in_specs=[blk_cols,
                  pl.BlockSpec((None, N_MOD, D), lambda b, c: (b, 0, 0)),
                  pl.BlockSpec((2 * rows, rows), lambda b, c: (0, 0))],
        out_specs=[blk_cols, blk_cols],
        out_shape=[z_shape, z_shape],
        scratch_shapes=[scr, scr, scr],
        compiler_params=_cparams(("arbitrary", "arbitrary")),
        name="fnet_s1",
    )(grid3, mod_l, f1)
    xr, xi = pl.pallas_call(
        _fnet_s2_kernel,
        grid=(n_batch, n_ct),
        in_specs=[blk_rows, blk_rows,
                  pl.BlockSpec((COL_TILE, 2 * rows, 2 * GRID_W), lambda b, c: (c, 0, 0))],
        out_specs=[blk_cols, blk_cols],
        out_shape=[z_shape, z_shape],
        scratch_shapes=[scr, scr],
        compiler_params=_cparams(("arbitrary", "arbitrary")),
        name="fnet_s2",
    )(zr, zi, m2)
    tpb = seq // TM_LIN
    row_fn = lambda t: _mod_row(t, tpb, n_batch)
    hs = _lin_res_call(hs, (xr.reshape(n_lat, D), xi.reshape(n_lat, D)), (cc, sc), w, b_out, mod_l,
                       0, n_lat // TM_LIN, row_fn)
    if with_ctx:
        a = np.arange(n_ctx)
        th = 2.0 * np.pi * np.outer(a, a) / n_ctx
        s = np.sqrt(float(seq) / n_ctx)
        fc = jnp.asarray(np.concatenate([np.cos(th) * s, -np.sin(th) * s], axis=0).astype(np.float32)).astype(BF16)
        c_shape = jax.ShapeDtypeStruct((n_batch * n_ctx, D), BF16)
        tile0 = n_lat // n_ctx
        xr, xi = pl.pallas_call(
            _fnet_ctx_kernel,
            grid=(n_batch,),
            in_specs=[pl.BlockSpec((n_ctx, D), lambda b: (tile0 + b, 0)),
                      pl.BlockSpec((None, N_MOD, D), lambda b: (n_batch, 0, 0)),
                      pl.BlockSpec((2 * n_ctx, n_ctx), lambda b: (0, 0))],
            out_specs=[pl.BlockSpec((n_ctx, D), lambda b: (b, 0))] * 2,
            out_shape=[c_shape, c_shape],
            compiler_params=_cparams(("arbitrary",)),
            name="fnet_ctx",
        )(hs, mod_l, fc)
        hs = _lin_res_call(hs, (xr, xi), (cc, sc), w, b_out, mod_l,
                           n_lat // TM_LIN, n_batch * n_ctx // TM_LIN, row_fn)
    return hs


def _glu_kernel(h_ref, mod_ref, wa_ref, wg_ref, ba_ref, bg_ref, o_ref, xn_ref):
    @pl.when(pl.program_id(1) == 0)
    def _():
        xn_ref[...] = _norm_mod(h_ref[...], mod_ref[3:4, :], mod_ref[4:5, :]).astype(BF16)

    xn = xn_ref[...]
    a = _dot(xn, wa_ref[...]) + ba_ref[...]
    g = _dot(xn, wg_ref[...]) + bg_ref[...]
    o_ref[...] = a / (1.0 + jnp.exp(-g))


def _conv_kernel(u_ref, up_ref, un_ref, wdw_ref, bdw_ref, lng_ref, lnb_ref, w_ref, b_ref, h_ref, mod_ref,
                 o_ref, buf_ref, v_ref, *, lat_tiles, tiles_per_seq):
    i = pl.program_id(0)
    is_ctx = i >= lat_tiles
    first = jnp.logical_or(is_ctx, i % tiles_per_seq == 0)
    last = jnp.logical_or(is_ctx, i % tiles_per_seq == tiles_per_seq - 1)
    tm = u_ref.shape[0]
    buf_ref[0:CONV_HALO, :] = jnp.where(first, 0.0, up_ref[...])
    buf_ref[CONV_HALO:CONV_HALO + tm, :] = u_ref[...]
    buf_ref[CONV_HALO + tm:, :] = jnp.where(last, 0.0, un_ref[...])
    chunk = 32
    off = CONV_HALO - CONV_K // 2
    for c0 in range(0, tm, chunk):
        acc = jnp.zeros((chunk, D), F32) + bdw_ref[...]
        for k in range(CONV_K):
            acc = acc + wdw_ref[k:k + 1, :] * buf_ref[c0 + off + k:c0 + off + k + chunk, :]
        mu = jnp.mean(acc, axis=-1, keepdims=True)
        d = acc - mu
        var = jnp.mean(d * d, axis=-1, keepdims=True)
        v = d * lax.rsqrt(var + EPS) * lng_ref[...] + lnb_ref[...]
        v_ref[c0:c0 + chunk, :] = _silu(v).astype(BF16)
    y = _dot(v_ref[...], w_ref[...]) + b_ref[...]
    o_ref[...] = h_ref[...] + mod_ref[5:6, :] * y


def _conv_mixer(hs, mod_l, w_in, b_in, w_dw, b_dw, ln_g, ln_b, w_out, b_out, n_batch, seq, n_ctx):
    n_all = hs.shape[0]
    n_lat = n_batch * seq
    assert n_ctx == TM_CONV, "context sequences are one conv tile long"
    n_tiles = n_all // TM_FFN
    n_col = D // FF_TILE
    tpb = seq // TM_FFN
    col = pl.BlockSpec((D, FF_TILE), lambda i, j: (0, j))
    gcol = pl.BlockSpec((D, FF_TILE), lambda i, j: (0, n_col + j))
    u = pl.pallas_call(
        _glu_kernel,
        grid=(n_tiles, n_col),
        in_specs=[pl.BlockSpec((TM_FFN, D), lambda i, j: (i, 0)),
                  pl.BlockSpec((None, N_MOD, D), lambda i, j: (_mod_row(i, tpb, n_batch), 0, 0)),
                  col, gcol,
                  pl.BlockSpec((1, FF_TILE), lambda i, j: (0, j)),
                  pl.BlockSpec((1, FF_TILE), lambda i, j: (0, n_col + j))],
        out_specs=pl.BlockSpec((TM_FFN, FF_TILE), lambda i, j: (i, j)),
        out_shape=jax.ShapeDtypeStruct((n_all, D), F32),
        scratch_shapes=[pltpu.VMEM((TM_FFN, D), BF16)],
        compiler_params=_cparams(("arbitrary", "arbitrary")),
        name="conv_glu",
    )(hs, mod_l, w_in.astype(BF16), w_in.astype(BF16), b_in.reshape(1, 2 * D), b_in.reshape(1, 2 * D))

    n_ct = n_all // TM_CONV
    hpt = TM_CONV // CONV_HALO
    n_halo = n_all // CONV_HALO
    tps = seq // TM_CONV
    vec = pl.BlockSpec((1, D), lambda i: (0, 0))
    tile = pl.BlockSpec((TM_CONV, D), lambda i: (i, 0))
    return pl.pallas_call(
        functools.partial(_conv_kernel, lat_tiles=n_lat // TM_CONV, tiles_per_seq=tps),
        grid=(n_ct,),
        in_specs=[tile,
                  pl.BlockSpec((CONV_HALO, D), lambda i: (jnp.maximum(i * hpt - 1, 0), 0)),
                  pl.BlockSpec((CONV_HALO, D), lambda i: (jnp.minimum((i + 1) * hpt, n_halo - 1), 0)),
                  pl.BlockSpec((CONV_K, D), lambda i: (0, 0)),
                  vec, vec, vec,
                  pl.BlockSpec((D, D), lambda i: (0, 0)),
                  vec, tile,
                  pl.BlockSpec((None, N_MOD, D), lambda i: (_mod_row(i, tps, n_batch), 0, 0))],
        out_specs=tile,
        out_shape=jax.ShapeDtypeStruct(hs.shape, F32),
        scratch_shapes=[pltpu.VMEM((TM_CONV + 2 * CONV_HALO, D), F32), pltpu.VMEM((TM_CONV, D), BF16)],
        input_output_aliases={9: 0},
        compiler_params=_cparams(("arbitrary",)),
        name="conv_dw",
    )(u, u, u, w_dw, b_dw.reshape(1, D), ln_g.reshape(1, D), ln_b.reshape(1, D), w_out.astype(BF16),
      b_out.reshape(1, D), hs, mod_l)


def _qkv_kernel(h_ref, mod_ref, w_ref, gain_ref, e_ref, o_ref, xn_ref, *, n_norm):
    j = pl.program_id(1)

    @pl.when(j == 0)
    def _():
        xn_ref[...] = _norm_mod(h_ref[...], mod_ref[3:4, :], mod_ref[4:5, :]).astype(BF16)

    y = _dot(xn_ref[...], w_ref[...])

    @pl.when(j < n_norm)
    def _():
        ss = _dot((y * y).astype(BF16), e_ref[...])
        o_ref[...] = (y * lax.rsqrt(ss * (1.0 / HEAD_DIM) + EPS) * gain_ref[...]).astype(BF16)

    @pl.when(j >= n_norm)
    def _():
        o_ref[...] = y.astype(BF16)


def _softmax_pv(s_list, v_list):
    m = s_list[0].max(axis=-1, keepdims=True)
    for s in s_list[1:]:
        m = jnp.maximum(m, s.max(axis=-1, keepdims=True))
    den = 0.0
    acc = 0.0
    for s, v in zip(s_list, v_list):
        p = jnp.exp(s - m)
        den = den + p.sum(axis=-1, keepdims=True)
        acc = acc + _dot(p.astype(BF16), v)
    return acc / den


def _qk(q, k):
    return lax.dot_general(q, k, (((1,), (1,)), ((), ())), preferred_element_type=F32)


def _natten_kernel(q_ref, k_ref, v_ref, kc_ref, vc_ref, bias_ref, o_ref, *, rows, kr):
    r = pl.program_id(1)
    rs = jnp.clip(r - kr // 2, 0, rows - kr)
    win = pl.ds(pl.multiple_of(rs * GRID_W, GRID_W), kr * GRID_W)
    outs = []
    for h in range(N_HEADS):
        hd = slice(h * HEAD_DIM, (h + 1) * HEAD_DIM)
        q = q_ref[:, hd]
        s_lat = _qk(q, k_ref[win, hd]) + bias_ref[h]
        s_ctx = _qk(q, kc_ref[:, hd])
        outs.append(_softmax_pv([s_lat, s_ctx], [v_ref[win, hd], vc_ref[:, hd]]))
    o_ref[...] = jnp.concatenate(outs, axis=-1).astype(BF16)


def _ctx_attn_kernel(q_ref, k_ref, v_ref, o_ref):
    outs = []
    for h in range(N_HEADS):
        hd = slice(h * HEAD_DIM, (h + 1) * HEAD_DIM)
        outs.append(_softmax_pv([_qk(q_ref[:, hd], k_ref[:, hd])], [v_ref[:, hd]]))
    o_ref[...] = jnp.concatenate(outs, axis=-1).astype(BF16)


def _natten_bias(rpb, rows, kr):
    n_off = kr
    qc = np.arange(GRID_W)[:, None]
    kc = np.arange(GRID_W)[None, :]
    w0 = np.clip(qc - WIN_C // 2, 0, GRID_W - WIN_C)
    ok = (kc >= w0) & (kc < w0 + WIN_C)
    dc = np.clip(kc - qc + WIN_C - 1, 0, 2 * WIN_C - 2)
    dr = np.arange(n_off)[:, None] + np.arange(kr)[None, :]
    t = rpb[:, dr[:, :, None, None], dc[None, None, :, :]]
    t = jnp.where(ok[None, None, None], t.astype(F32), NEG_INF)
    return jnp.transpose(t, (1, 0, 3, 2, 4)).reshape(n_off, N_HEADS, GRID_W, kr * GRID_W)


def _natten_mixer(hs, mod_l, w_qkv, q_gain, k_gain, rpb, w_o, n_batch, seq, n_ctx):
    n_all = hs.shape[0]
    n_lat = n_batch * seq
    rows = seq // GRID_W
    kr = min(WIN_R, rows)
    n_tiles = n_all // TM_FFN
    tpb = seq // TM_FFN
    n_col = 3 * D // FF_TILE
    scale = HEAD_DIM ** -0.5
    gain = jnp.concatenate([jnp.tile(q_gain, N_HEADS) * scale, jnp.tile(k_gain, N_HEADS),
                            jnp.ones((D,), F32)]).reshape(1, 3 * D)
    lane = np.arange(FF_TILE)
    e = jnp.asarray((lane[:, None] // HEAD_DIM == lane[None, :] // HEAD_DIM).astype(np.float32)).astype(BF16)
    qkv = pl.pallas_call(
        functools.partial(_qkv_kernel, n_norm=2 * D // FF_TILE),
        grid=(n_tiles, n_col),
        in_specs=[pl.BlockSpec((TM_FFN, D), lambda i, j: (i, 0)),
                  pl.BlockSpec((None, N_MOD, D), lambda i, j: (_mod_row(i, tpb, n_batch), 0, 0)),
                  pl.BlockSpec((D, FF_TILE), lambda i, j: (0, j)),
                  pl.BlockSpec((1, FF_TILE), lambda i, j: (0, j)),
                  pl.BlockSpec((FF_TILE, FF_TILE), lambda i, j: (0, 0))],
        out_specs=pl.BlockSpec((TM_FFN, FF_TILE), lambda i, j: (i, j)),
        out_shape=jax.ShapeDtypeStruct((n_all, 3 * D), BF16),
        scratch_shapes=[pltpu.VMEM((TM_FFN, D), BF16)],
        compiler_params=_cparams(("arbitrary", "arbitrary")),
        name="qkv",
    )(hs, mod_l, w_qkv.astype(BF16), gain, e)

    bias = _natten_bias(rpb, rows, kr)
    ctx0 = n_lat // n_ctx

    def bias_index(b, r):
        rs = jnp.clip(r - kr // 2, 0, rows - kr)
        return (rs - r + WIN_R - 1, 0, 0, 0)

    o_lat = pl.pallas_call(
        functools.partial(_natten_kernel, rows=rows, kr=kr),
        grid=(n_batch, rows),
        in_specs=[pl.BlockSpec((GRID_W, D), lambda b, r: (b * rows + r, 0)),
                  pl.BlockSpec((seq, D), lambda b, r: (b, 1)),
                  pl.BlockSpec((seq, D), lambda b, r: (b, 2)),
                  pl.BlockSpec((n_ctx, D), lambda b, r: (ctx0 + b, 1)),
                  pl.BlockSpec((n_ctx, D), lambda b, r: (ctx0 + b, 2)),
                  pl.BlockSpec((None, N_HEADS, GRID_W, kr * GRID_W), bias_index)],
        out_specs=pl.BlockSpec((GRID_W, D), lambda b, r: (b * rows + r, 0)),
        out_shape=jax.ShapeDtypeStruct((n_lat, D), BF16),
        compiler_params=_cparams(("arbitrary", "arbitrary")),
        name="natten",
    )(qkv, qkv, qkv, qkv, qkv, bias)
    o_ctx = pl.pallas_call(
        _ctx_attn_kernel,
        grid=(n_batch,),
        in_specs=[pl.BlockSpec((n_ctx, D), lambda b: (ctx0 + b, 0)),
                  pl.BlockSpec((n_ctx, D), lambda b: (ctx0 + b, 1)),
                  pl.BlockSpec((n_ctx, D), lambda b: (ctx0 + b, 2))],
        out_specs=pl.BlockSpec((n_ctx, D), lambda b: (b, 0)),
        out_shape=jax.ShapeDtypeStruct((n_batch * n_ctx, D), BF16),
        compiler_params=_cparams(("arbitrary",)),
        name="ctx_attn",
    )(qkv, qkv, qkv)
    w = w_o.astype(BF16)
    zero_b = jnp.zeros((D,), F32)
    row_fn = lambda t: _mod_row(t, seq // TM_LIN, n_batch)
    hs = _lin_res_call(hs, (o_lat,), (), w, zero_b, mod_l, 0, n_lat // TM_LIN, row_fn)
    return _lin_res_call(hs, (o_ctx,), (), w, zero_b, mod_l, n_lat // TM_LIN, n_batch * n_ctx // TM_LIN, row_fn)


def kernel(x, c, ctx, c_ctx, w_mod, b_mod, w_ff_in, w_ff_out, fnet_w_out, fnet_b_out, conv_w_in, conv_b_in, conv_w_dw, conv_b_dw, conv_ln_g, conv_ln_b, conv_w_out, conv_b_out, na_w_qkv, na_q_gain, na_k_gain, na_rpb, na_w_o):
    n_batch, seq, d = x.shape
    n_ctx = ctx.shape[1]
    depth = w_mod.shape[0]
    assert d == D and n_batch < MOD_ROWS
    assert seq % TM_FFN == 0 and (n_batch * n_ctx) % TM_FFN == 0
    n_lat = n_batch * seq

    cond = jnp.concatenate([c, c_ctx[None, :], jnp.zeros((MOD_ROWS - n_batch - 1, D), F32)], axis=0)
    mod = _mod_call(cond, w_mod, b_mod).reshape(depth, MOD_ROWS, N_MOD, D)
    hs = jnp.concatenate([x.reshape(n_lat, D), ctx.reshape(n_batch * n_ctx, D)], axis=0)
    lat_tiles = n_lat // TM_FFN
    all_tiles = hs.shape[0] // TM_FFN

    for i in range(depth):
        kind, slot = i % 3, i // 3
        ctx_post = i < depth - 1
        ctx_pre = ctx_post or kind == 2
        mod_l = mod[i]
        hs = _ffn_call(hs, mod_l, w_ff_in, w_ff_out, i, 0, 0, all_tiles if ctx_pre else lat_tiles,
                       n_batch, seq, True)
        if kind == 0:
            hs = _fourier_mixer(hs, mod_l, fnet_w_out[slot], fnet_b_out[slot], n_batch, seq, n_ctx, ctx_post)
        elif kind == 1:
            assert ctx_post
            hs = _conv_mixer(hs, mod_l, conv_w_in[slot], conv_b_in[slot], conv_w_dw[slot], conv_b_dw[slot],
                             conv_ln_g[slot], conv_ln_b[slot], conv_w_out[slot], conv_b_out[slot],
                             n_batch, seq, n_ctx)
        else:
            assert ctx_post
            hs = _natten_mixer(hs, mod_l, na_w_qkv[slot], na_q_gain[slot], na_k_gain[slot], na_rpb[slot],
                               na_w_o[slot], n_batch, seq, n_ctx)
        final = i == depth - 1
        hs = _ffn_call(hs, mod_l, w_ff_in, w_ff_out, i, 1, 6, all_tiles if ctx_post else lat_tiles,
                       n_batch, seq, not final)
    return hs[:n_lat].reshape(n_batch, seq, D)
```

```python
import functools

import numpy as np
import jax
import jax.numpy as jnp
from jax import lax
from jax.experimental import pallas as pl
from jax.experimental.pallas import tpu as pltpu

F32 = jnp.float32
BF16 = jnp.bfloat16

D = 1024
D_FF = 2816
N_MOD = 9
EPS = 1e-6
GRID_W = 64
LANE = 128
SUBLANE = 8
N_GROUPS = 4
GROUP_C = D // N_GROUPS
CONV_K = 31
CONV_HALO = 16
N_HEADS = 16
HEAD_DIM = D // N_HEADS
WIN_R = 8
WIN_C = 16
NEG_INF = -1e30
MOD_ROWS = 8

FF_TILE = 256
TM_FFN = 1024
TM_LIN = 512
TM_CONV = 256
COL_TILE = 16
VMEM_LIMIT = 52 * 1024 * 1024


def _cparams(sem):
    return pltpu.CompilerParams(dimension_semantics=sem, vmem_limit_bytes=VMEM_LIMIT)


def _silu(x):
    return x / (1.0 + jnp.exp(-x))


def _norm_mod(x, shift, scale):
    ms = jnp.mean(x * x, axis=-1, keepdims=True)
    return x * lax.rsqrt(ms + EPS) * (1.0 + scale) + shift


def _dot(a, b):
    return jnp.dot(a, b, preferred_element_type=F32)


def _mod_kernel(c_ref, w_ref, b_ref, o_ref):
    s = _silu(c_ref[...]).astype(BF16)
    o_ref[...] = _dot(s, w_ref[...].astype(BF16)) + b_ref[...]


def _mod_call(cond, w_mod, b_mod):
    depth = w_mod.shape[0]
    return pl.pallas_call(
        _mod_kernel,
        grid=(depth, N_MOD),
        in_specs=[
            pl.BlockSpec((MOD_ROWS, D), lambda l, k: (0, 0)),
            pl.BlockSpec((None, D, D), lambda l, k: (l, 0, k)),
            pl.BlockSpec((None, 1, D), lambda l, k: (l, 0, k)),
        ],
        out_specs=pl.BlockSpec((None, MOD_ROWS, D), lambda l, k: (l, 0, k)),
        out_shape=jax.ShapeDtypeStruct((depth, MOD_ROWS, N_MOD * D), F32),
        compiler_params=_cparams(("arbitrary", "arbitrary")),
        name="mod",
    )(cond, w_mod, b_mod.reshape(depth, 1, N_MOD * D))


def _mod_row(tile, tiles_per_batch, n_batch):
    return jnp.minimum(tile // tiles_per_batch, n_batch)


def _ffn_kernel(h_ref, mod_ref, wg_ref, wu_ref, wo_ref, o_ref, xn_ref, acc_ref, *, k0, n_ff):
    j = pl.program_id(1)

    @pl.when(j == 0)
    def _():
        xn_ref[...] = _norm_mod(h_ref[...], mod_ref[k0:k0 + 1, :], mod_ref[k0 + 1:k0 + 2, :]).astype(BF16)
        acc_ref[...] = jnp.zeros_like(acc_ref)

    xn = xn_ref[...]
    g = _dot(xn, wg_ref[...].astype(BF16))
    u = _dot(xn, wu_ref[...].astype(BF16))
    acc_ref[...] += _dot((_silu(g) * u).astype(BF16), wo_ref[...].astype(BF16))

    @pl.when(j == n_ff - 1)
    def _():
        o_ref[...] = h_ref[...] + (0.5 * mod_ref[k0 + 2:k0 + 3, :]) * acc_ref[...]


def _ffn_call(hs, mod_l, w_ff_in, w_ff_out, layer, which, k0, n_tiles, n_batch, seq, alias):
    n_ff = D_FF // FF_TILE
    tpb = seq // TM_FFN
    out_rows = hs.shape[0] if alias else n_tiles * TM_FFN
    return pl.pallas_call(
        functools.partial(_ffn_kernel, k0=k0, n_ff=n_ff),
        grid=(n_tiles, n_ff),
        in_specs=[
            pl.BlockSpec((TM_FFN, D), lambda i, j: (i, 0)),
            pl.BlockSpec((None, N_MOD, D), lambda i, j: (_mod_row(i, tpb, n_batch), 0, 0)),
            pl.BlockSpec((None, None, D, FF_TILE), lambda i, j: (layer, which, 0, j)),
            pl.BlockSpec((None, None, D, FF_TILE), lambda i, j: (layer, which, 0, n_ff + j)),
            pl.BlockSpec((None, None, FF_TILE, D), lambda i, j: (layer, which, j, 0)),
        ],
        out_specs=pl.BlockSpec((TM_FFN, D), lambda i, j: (i, 0)),
        out_shape=jax.ShapeDtypeStruct((out_rows, D), F32),
        scratch_shapes=[pltpu.VMEM((TM_FFN, D), BF16), pltpu.VMEM((TM_FFN, D), F32)],
        input_output_aliases={0: 0} if alias else {},
        compiler_params=_cparams(("arbitrary", "arbitrary")),
        name="ffn",
    )(hs, mod_l, w_ff_in, w_ff_in, w_ff_out)


def _lin_res_kernel(*refs, fourier):
    if fourier:
        xr_ref, xi_ref, cc_ref, sc_ref, w_ref, b_ref, h_ref, mod_ref, o_ref = refs
        parts = []
        for g in range(N_GROUPS):
            sl = slice(g * GROUP_C, (g + 1) * GROUP_C)
            parts.append(_dot(xr_ref[:, sl], cc_ref[...]) + _dot(xi_ref[:, sl], sc_ref[...]))
        x = jnp.concatenate(parts, axis=-1).astype(BF16)
    else:
        x_ref, w_ref, b_ref, h_ref, mod_ref, o_ref = refs
        x = x_ref[...]
    y = _dot(x, w_ref[...]) + b_ref[...]
    o_ref[...] = h_ref[...] + mod_ref[5:6, :] * y


def _lin_res_call(hs, xs, tables, w, b, mod_l, tile0, n_tiles, mod_row_fn):
    fourier = len(xs) == 2
    x_spec = pl.BlockSpec((TM_LIN, D), lambda i: (i, 0))
    t_spec = pl.BlockSpec((GROUP_C, GROUP_C), lambda i: (0, 0))
    in_specs = [x_spec] * len(xs) + [t_spec] * len(tables) + [
        pl.BlockSpec((D, D), lambda i: (0, 0)),
        pl.BlockSpec((1, D), lambda i: (0, 0)),
        pl.BlockSpec((TM_LIN, D), lambda i: (i + tile0, 0)),
        pl.BlockSpec((None, N_MOD, D), lambda i: (mod_row_fn(i + tile0), 0, 0)),
    ]
    h_index = len(xs) + len(tables) + 2
    return pl.pallas_call(
        functools.partial(_lin_res_kernel, fourier=fourier),
        grid=(n_tiles,),
        in_specs=in_specs,
        out_specs=pl.BlockSpec((TM_LIN, D), lambda i: (i + tile0, 0)),
        out_shape=jax.ShapeDtypeStruct(hs.shape, F32),
        input_output_aliases={h_index: 0},
        compiler_params=_cparams(("arbitrary",)),
        name="lin_res_fourier" if fourier else "lin_res",
    )(*xs, *tables, w, b.reshape(1, D), hs, mod_l)


def _put_cols(ref, val):
    for lt in range(D // LANE):
        ref[lt] = val[:, lt * LANE:(lt + 1) * LANE]


def _get_cols(ref):
    return jnp.concatenate([ref[lt] for lt in range(D // LANE)], axis=-1)


def _get_rows_strided(ref, start, n, stride):
    rows = pl.ds(start, n, stride=stride)
    return jnp.concatenate([ref[lt, rows, :] for lt in range(D // LANE)], axis=-1)


def _put_rows_strided(ref, start, n, stride, val):
    rows = pl.ds(start, n, stride=stride)
    for lt in range(D // LANE):
        ref[lt, rows, :] = val[:, lt * LANE:(lt + 1) * LANE]


def _fnet_s1_kernel(h_ref, mod_ref, f1_ref, zr_ref, zi_ref, xs_ref, zrs_ref, zis_ref):
    rows = h_ref.shape[0]
    xn = _norm_mod(h_ref[...], mod_ref[3:4, :], mod_ref[4:5, :])
    _put_cols(xs_ref, xn.reshape(rows * COL_TILE, D))
    for ci in range(COL_TILE):
        z = _dot(f1_ref[...], _get_rows_strided(xs_ref, ci, rows, COL_TILE).astype(BF16))
        _put_rows_strided(zrs_ref, ci, rows, COL_TILE, z[:rows])
        _put_rows_strided(zis_ref, ci, rows, COL_TILE, z[rows:])
    zr_ref[...] = _get_cols(zrs_ref).reshape(rows, COL_TILE, D).astype(BF16)
    zi_ref[...] = _get_cols(zis_ref).reshape(rows, COL_TILE, D).astype(BF16)


def _fnet_s2_kernel(zr_ref, zi_ref, m2_ref, xr_ref, xi_ref, xrs_ref, xis_ref):
    rows = zr_ref.shape[1]
    for ki in range(COL_TILE):
        z = jnp.concatenate([zr_ref[ki], zi_ref[ki]], axis=0)
        x = _dot(m2_ref[ki], z)
        _put_rows_strided(xrs_ref, ki, rows, COL_TILE, x[:rows])
        _put_rows_strided(xis_ref, ki, rows, COL_TILE, x[rows:])
    xr_ref[...] = _get_cols(xrs_ref).reshape(rows, COL_TILE, D).astype(BF16)
    xi_ref[...] = _get_cols(xis_ref).reshape(rows, COL_TILE, D).astype(BF16)


def _fnet_ctx_kernel(h_ref, mod_ref, f_ref, xr_ref, xi_ref):
    n = h_ref.shape[0]
    xn = _norm_mod(h_ref[...], mod_ref[3:4, :], mod_ref[4:5, :]).astype(BF16)
    x = _dot(f_ref[...], xn)
    xr_ref[...] = x[:n].astype(BF16)
    xi_ref[...] = x[n:].astype(BF16)


def _dft_tables(rows):
    n_seq = rows * GRID_W
    a = np.arange(rows)
    th1 = 2.0 * np.pi * np.outer(a, a) / rows
    f1 = np.concatenate([np.cos(th1), -np.sin(th1)], axis=0)
    k2 = np.arange(GRID_W)[:, None, None]
    k1 = np.arange(rows)[None, :, None]
    n1 = np.arange(GRID_W)[None, None, :]
    th2 = 2.0 * np.pi * (n1 * (GRID_W * k1 + k2) % n_seq) / n_seq
    c2, s2 = np.cos(th2), np.sin(th2)
    m2 = np.concatenate([np.concatenate([c2, s2], axis=2), np.concatenate([-s2, c2], axis=2)], axis=1)
    return f1.astype(np.float32), m2.astype(np.float32)


def _channel_tables(n_seq):
    c = np.arange(GROUP_C)
    th = 2.0 * np.pi * np.outer(c, c) / GROUP_C
    s = 1.0 / np.sqrt(float(n_seq) * GROUP_C)
    return (np.cos(th) * s).astype(np.float32), (np.sin(th) * s).astype(np.float32)


def _fourier_mixer(hs, mod_l, w_out, b_out, n_batch, seq, n_ctx, with_ctx):
    rows = seq // GRID_W
    assert rows == GRID_W, "the two-stage DFT is written for a square token grid"
    n_lat = n_batch * seq
    f1, m2 = _dft_tables(rows)
    cc, sc = _channel_tables(seq)
    f1, m2, cc, sc = (jnp.asarray(t).astype(BF16) for t in (f1, m2, cc, sc))
    w = w_out.astype(BF16)
    n_ct = GRID_W // COL_TILE
    grid3 = hs.reshape(hs.shape[0] // GRID_W, GRID_W, D)
    z_shape = jax.ShapeDtypeStruct((n_batch * rows, GRID_W, D), BF16)
    blk_cols = pl.BlockSpec((rows, COL_TILE, D), lambda b, c: (b, c, 0))
    blk_rows = pl.BlockSpec((COL_TILE, GRID_W, D), lambda b, c: (b * n_ct + c, 0, 0))
    scr = pltpu.VMEM((D // LANE, rows * COL_TILE, LANE), F32)
    zr, zi = pl.pallas_call(
        _fnet_s1_kernel,
        grid=(n_batch, n_ct),
        in_specs=[blk_cols,
                  pl.BlockSpec((None, N_MOD, D), lambda b, c: (b, 0, 0)),
                  pl.BlockSpec((2 * rows, rows), lambda b, c: (0, 0))],
        out_specs=[blk_cols, blk_cols],
        out_shape=[z_shape, z_shape],
        scratch_shapes=[scr, scr, scr],
        compiler_params=_cparams(("arbitrary", "arbitrary")),
        name="fnet_s1",
    )(grid3, mod_l, f1)
    xr, xi = pl.pallas_call(
        _fnet_s2_kernel,
        grid=(n_batch, n_ct),
        in_specs=[blk_rows, blk_rows,
                  pl.BlockSpec((COL_TILE, 2 * rows, 2 * GRID_W), lambda b, c: (c, 0, 0))],
        out_specs=[blk_cols, blk_cols],
        out_shape=[z_shape, z_shape],
        scratch_shapes=[scr, scr],
        compiler_params=_cparams(("arbitrary", "arbitrary")),
        name="fnet_s2",
    )(zr, zi, m2)
    tpb = seq // TM_LIN
    row_fn = lambda t: _mod_row(t, tpb, n_batch)
    hs = _lin_res_call(hs, (xr.reshape(n_lat, D), xi.reshape(n_lat, D)), (cc, sc), w, b_out, mod_l,
                       0, n_lat // TM_LIN, row_fn)
    if with_ctx:
        a = np.arange(n_ctx)
        th = 2.0 * np.pi * np.outer(a, a) / n_ctx
        s = np.sqrt(float(seq) / n_ctx)
        fc = jnp.asarray(np.concatenate([np.cos(th) * s, -np.sin(th) * s], axis=0).astype(np.float32)).astype(BF16)
        c_shape = jax.ShapeDtypeStruct((n_batch * n_ctx, D), BF16)
        tile0 = n_lat // n_ctx
        xr, xi = pl.pallas_call(
            _fnet_ctx_kernel,
            grid=(n_batch,),
            in_specs=[pl.BlockSpec((n_ctx, D), lambda b: (tile0 + b, 0)),
                      pl.BlockSpec((None, N_MOD, D), lambda b: (n_batch, 0, 0)),
                      pl.BlockSpec((2 * n_ctx, n_ctx), lambda b: (0, 0))],
            out_specs=[pl.BlockSpec((n_ctx, D), lambda b: (b, 0))] * 2,
            out_shape=[c_shape, c_shape],
            compiler_params=_cparams(("arbitrary",)),
            name="fnet_ctx",
        )(hs, mod_l, fc)
        hs = _lin_res_call(hs, (xr, xi), (cc, sc), w, b_out, mod_l,
                           n_lat // TM_LIN, n_batch * n_ctx // TM_LIN, row_fn)
    return hs


def _glu_kernel(h_ref, mod_ref, wa_ref, wg_ref, ba_ref, bg_ref, o_ref, xn_ref):
    @pl.when(pl.program_id(1) == 0)
    def _():
        xn_ref[...] = _norm_mod(h_ref[...], mod_ref[3:4, :], mod_ref[4:5, :]).astype(BF16)

    xn = xn_ref[...]
    a = _dot(xn, wa_ref[...]) + ba_ref[...]
    g = _dot(xn, wg_ref[...]) + bg_ref[...]
    o_ref[...] = a / (1.0 + jnp.exp(-g))


def _conv_kernel(u_ref, up_ref, un_ref, wdw_ref, bdw_ref, lng_ref, lnb_ref, w_ref, b_ref, h_ref, mod_ref,
                 o_ref, buf_ref, v_ref, *, lat_tiles, tiles_per_seq):
    i = pl.program_id(0)
    is_ctx = i >= lat_tiles
    first = jnp.logical_or(is_ctx, i % tiles_per_seq == 0)
    last = jnp.logical_or(is_ctx, i % tiles_per_seq == tiles_per_seq - 1)
    tm = u_ref.shape[0]
    buf_ref[0, 0:CONV_HALO, :] = jnp.where(first, 0.0, up_ref[...])
    buf_ref[0, CONV_HALO:CONV_HALO + tm, :] = u_ref[...]
    buf_ref[0, CONV_HALO + tm:, :] = jnp.where(last, 0.0, un_ref[...])
    n_shifted = tm + 2 * CONV_HALO - SUBLANE
    for s in range(1, SUBLANE):
        buf_ref[s, 0:n_shifted, :] = buf_ref[0, s:s + n_shifted, :]
    chunk = 32
    off = CONV_HALO - CONV_K // 2
    for c0 in range(0, tm, chunk):
        acc = jnp.zeros((chunk, D), F32) + bdw_ref[...]
        for k in range(CONV_K):
            row = c0 + (off + k) // SUBLANE * SUBLANE
            acc = acc + wdw_ref[k:k + 1, :] * buf_ref[(off + k) % SUBLANE, row:row + chunk, :]
        mu = jnp.mean(acc, axis=-1, keepdims=True)
        d = acc - mu
        var = jnp.mean(d * d, axis=-1, keepdims=True)
        v = d * lax.rsqrt(var + EPS) * lng_ref[...] + lnb_ref[...]
        v_ref[c0:c0 + chunk, :] = _silu(v).astype(BF16)
    y = _dot(v_ref[...], w_ref[...]) + b_ref[...]
    o_ref[...] = h_ref[...] + mod_ref[5:6, :] * y


def _conv_mixer(hs, mod_l, w_in, b_in, w_dw, b_dw, ln_g, ln_b, w_out, b_out, n_batch, seq, n_ctx):
    n_all = hs.shape[0]
    n_lat = n_batch * seq
    assert n_ctx == TM_CONV, "context sequences are one conv tile long"
    n_tiles = n_all // TM_FFN
    n_col = D // FF_TILE
    tpb = seq // TM_FFN
    col = pl.BlockSpec((D, FF_TILE), lambda i, j: (0, j))
    gcol = pl.BlockSpec((D, FF_TILE), lambda i, j: (0, n_col + j))
    u = pl.pallas_call(
        _glu_kernel,
        grid=(n_tiles, n_col),
        in_specs=[pl.BlockSpec((TM_FFN, D), lambda i, j: (i, 0)),
                  pl.BlockSpec((None, N_MOD, D), lambda i, j: (_mod_row(i, tpb, n_batch), 0, 0)),
                  col, gcol,
                  pl.BlockSpec((1, FF_TILE), lambda i, j: (0, j)),
                  pl.BlockSpec((1, FF_TILE), lambda i, j: (0, n_col + j))],
        out_specs=pl.BlockSpec((TM_FFN, FF_TILE), lambda i, j: (i, j)),
        out_shape=jax.ShapeDtypeStruct((n_all, D), F32),
        scratch_shapes=[pltpu.VMEM((TM_FFN, D), BF16)],
        compiler_params=_cparams(("arbitrary", "arbitrary")),
        name="conv_glu",
    )(hs, mod_l, w_in.astype(BF16), w_in.astype(BF16), b_in.reshape(1, 2 * D), b_in.reshape(1, 2 * D))

    n_ct = n_all // TM_CONV
    hpt = TM_CONV // CONV_HALO
    n_halo = n_all // CONV_HALO
    tps = seq // TM_CONV
    vec = pl.BlockSpec((1, D), lambda i: (0, 0))
    tile = pl.BlockSpec((TM_CONV, D), lambda i: (i, 0))
    return pl.pallas_call(
        functools.partial(_conv_kernel, lat_tiles=n_lat // TM_CONV, tiles_per_seq=tps),
        grid=(n_ct,),
        in_specs=[tile,
                  pl.BlockSpec((CONV_HALO, D), lambda i: (jnp.maximum(i * hpt - 1, 0), 0)),
                  pl.BlockSpec((CONV_HALO, D), lambda i: (jnp.minimum((i + 1) * hpt, n_halo - 1), 0)),
                  pl.BlockSpec((CONV_K, D), lambda i: (0, 0)),
                  vec, vec, vec,
                  pl.BlockSpec((D, D), lambda i: (0, 0)),
                  vec, tile,
                  pl.BlockSpec((None, N_MOD, D), lambda i: (_mod_row(i, tps, n_batch), 0, 0))],
        out_specs=tile,
        out_shape=jax.ShapeDtypeStruct(hs.shape, F32),
        scratch_shapes=[pltpu.VMEM((SUBLANE, TM_CONV + 2 * CONV_HALO, D), F32), pltpu.VMEM((TM_CONV, D), BF16)],
        input_output_aliases={9: 0},
        compiler_params=_cparams(("arbitrary",)),
        name="conv_dw",
    )(u, u, u, w_dw, b_dw.reshape(1, D), ln_g.reshape(1, D), ln_b.reshape(1, D), w_out.astype(BF16),
      b_out.reshape(1, D), hs, mod_l)


def _qkv_kernel(h_ref, mod_ref, w_ref, gain_ref, e_ref, o_ref, xn_ref, *, n_norm):
    j = pl.program_id(1)

    @pl.when(j == 0)
    def _():
        xn_ref[...] = _norm_mod(h_ref[...], mod_ref[3:4, :], mod_ref[4:5, :]).astype(BF16)

    y = _dot(xn_ref[...], w_ref[...])
    ss = _dot((y * y).astype(BF16), e_ref[...])
    rs = lax.rsqrt(ss * (1.0 / HEAD_DIM) + EPS) * gain_ref[...]
    o_ref[...] = (y * jnp.where(j < n_norm, rs, 1.0)).astype(BF16)


def _softmax_pv(s_list, v_list):
    m = s_list[0].max(axis=-1, keepdims=True)
    for s in s_list[1:]:
        m = jnp.maximum(m, s.max(axis=-1, keepdims=True))
    den = 0.0
    acc = 0.0
    for s, v in zip(s_list, v_list):
        p = jnp.exp(s - m)
        den = den + p.sum(axis=-1, keepdims=True)
        acc = acc + _dot(p.astype(BF16), v)
    return acc / den


def _qk(q, k):
    return lax.dot_general(q, k, (((1,), (1,)), ((), ())), preferred_element_type=F32)


def _natten_kernel(q_ref, k_ref, v_ref, kc_ref, vc_ref, bias_ref, o_ref, *, rows, kr):
    r = pl.program_id(1)
    rs = jnp.clip(r - kr // 2, 0, rows - kr)
    dr0 = rs - r + WIN_R - 1
    win = pl.ds(pl.multiple_of(rs * GRID_W, GRID_W), kr * GRID_W)
    left = lax.broadcasted_iota(jnp.int32, (GRID_W, LANE), 1) < HEAD_DIM
    for p in range(N_HEADS // 2):
        cols = slice(p * LANE, (p + 1) * LANE)
        q = q_ref[:, cols]
        zero = jnp.zeros_like(q)
        q2 = jnp.concatenate([jnp.where(left, q, zero), jnp.where(left, zero, q)], axis=0)
        bias = jnp.concatenate([bias_ref[p, dr0 + 2 * t] for t in range(kr // 2)], axis=-1)
        s_lat = _qk(q2, k_ref[win, cols]) + bias
        s_ctx = _qk(q2, kc_ref[:, cols])
        o = _softmax_pv([s_lat, s_ctx], [v_ref[win, cols], vc_ref[:, cols]])
        o_ref[:, cols] = jnp.where(left, o[:GRID_W], o[GRID_W:]).astype(BF16)


def _bias_kernel(rpb_ref, o_ref):
    qc = lax.broadcasted_iota(jnp.int32, (GRID_W, LANE), 0)
    lane = lax.broadcasted_iota(jnp.int32, (GRID_W, LANE), 1)
    kc = lane % GRID_W
    w0 = jnp.clip(qc - WIN_C // 2, 0, GRID_W - WIN_C)
    ok = jnp.logical_and(kc >= w0, kc < w0 + WIN_C)
    tiles = []
    for dr in range(2 * WIN_R - 1):
        t = jnp.broadcast_to(rpb_ref[dr:dr + 1, :], (GRID_W, LANE))
        t = pltpu.roll(t, LANE - (WIN_C - 1), axis=1)
        for bit in range(GRID_W.bit_length() - 1):
            t = jnp.where((qc >> bit) & 1 == 1, pltpu.roll(t, 1 << bit, axis=1), t)
        tiles.append(t)
    for p in range(2 * WIN_R - 2):
        pair = jnp.where(lane < GRID_W, tiles[p], pltpu.roll(tiles[p + 1], GRID_W, axis=1))
        o_ref[p] = jnp.where(ok, pair, NEG_INF)


def _ctx_attn_kernel(q_ref, k_ref, v_ref, o_ref):
    outs = []
    for h in range(N_HEADS):
        hd = slice(h * HEAD_DIM, (h + 1) * HEAD_DIM)
        outs.append(_softmax_pv([_qk(q_ref[:, hd], k_ref[:, hd])], [v_ref[:, hd]]))
    o_ref[...] = jnp.concatenate(outs, axis=-1).astype(BF16)


def _natten_bias(rpb):
    n_dr, n_dc = 2 * WIN_R - 1, 2 * WIN_C - 1
    assert rpb.shape == (N_HEADS, n_dr, n_dc) and GRID_W * 2 == LANE
    padded = jnp.pad(rpb.astype(F32), ((0, 0), (0, 0), (0, LANE - n_dc)))
    return pl.pallas_call(
        _bias_kernel,
        grid=(N_HEADS,),
        in_specs=[pl.BlockSpec((None, n_dr, LANE), lambda h: (h, 0, 0))],
        out_specs=pl.BlockSpec((None, n_dr - 1, GRID_W, LANE), lambda h: (h // 2, 0, h % 2, 0)),
        out_shape=jax.ShapeDtypeStruct((N_HEADS // 2, n_dr - 1, 2 * GRID_W, LANE), F32),
        compiler_params=_cparams(("arbitrary",)),
        name="natten_bias",
    )(padded)


def _natten_mixer(hs, mod_l, w_qkv, q_gain, k_gain, rpb, w_o, n_batch, seq, n_ctx):
    n_all = hs.shape[0]
    n_lat = n_batch * seq
    rows = seq // GRID_W
    kr = min(WIN_R, rows)
    n_tiles = n_all // TM_FFN
    tpb = seq // TM_FFN
    n_col = 3 * D // FF_TILE
    scale = HEAD_DIM ** -0.5
    gain = jnp.concatenate([jnp.tile(q_gain, N_HEADS) * scale, jnp.tile(k_gain, N_HEADS),
                            jnp.ones((D,), F32)]).reshape(1, 3 * D)
    lane = np.arange(FF_TILE)
    e = jnp.asarray((lane[:, None] // HEAD_DIM == lane[None, :] // HEAD_DIM).astype(np.float32)).astype(BF16)
    qkv = pl.pallas_call(
        functools.partial(_qkv_kernel, n_norm=2 * D // FF_TILE),
        grid=(n_tiles, n_col),
        in_specs=[pl.BlockSpec((TM_FFN, D), lambda i, j: (i, 0)),
                  pl.BlockSpec((None, N_MOD, D), lambda i, j: (_mod_row(i, tpb, n_batch), 0, 0)),
                  pl.BlockSpec((D, FF_TILE), lambda i, j: (0, j)),
                  pl.BlockSpec((1, FF_TILE), lambda i, j: (0, j)),
                  pl.BlockSpec((FF_TILE, FF_TILE), lambda i, j: (0, 0))],
        out_specs=pl.BlockSpec((TM_FFN, FF_TILE), lambda i, j: (i, j)),
        out_shape=jax.ShapeDtypeStruct((n_all, 3 * D), BF16),
        scratch_shapes=[pltpu.VMEM((TM_FFN, D), BF16)],
        compiler_params=_cparams(("arbitrary", "arbitrary")),
        name="qkv",
    )(hs, mod_l, w_qkv.astype(BF16), gain, e)

    assert kr == WIN_R, "bias tiles pair up the key rows of a full window"
    bias = _natten_bias(rpb)
    ctx0 = n_lat // n_ctx
    once = pl.Buffered(1)
    o_lat = pl.pallas_call(
        functools.partial(_natten_kernel, rows=rows, kr=kr),
        grid=(n_batch, rows),
        in_specs=[pl.BlockSpec((GRID_W, D), lambda b, r: (b * rows + r, 0)),
                  pl.BlockSpec((seq, D), lambda b, r: (b, 1)),
                  pl.BlockSpec((seq, D), lambda b, r: (b, 2)),
                  pl.BlockSpec((n_ctx, D), lambda b, r: (ctx0 + b, 1)),
                  pl.BlockSpec((n_ctx, D), lambda b, r: (ctx0 + b, 2)),
                  pl.BlockSpec(bias.shape, lambda b, r: (0, 0, 0, 0), pipeline_mode=once)],
        out_specs=pl.BlockSpec((GRID_W, D), lambda b, r: (b * rows + r, 0)),
        out_shape=jax.ShapeDtypeStruct((n_lat, D), BF16),
        compiler_params=_cparams(("arbitrary", "arbitrary")),
        name="natten",
    )(qkv, qkv, qkv, qkv, qkv, bias)
    o_ctx = pl.pallas_call(
        _ctx_attn_kernel,
        grid=(n_batch,),
        in_specs=[pl.BlockSpec((n_ctx, D), lambda b: (ctx0 + b, 0)),
                  pl.BlockSpec((n_ctx, D), lambda b: (ctx0 + b, 1)),
                  pl.BlockSpec((n_ctx, D), lambda b: (ctx0 + b, 2))],
        out_specs=pl.BlockSpec((n_ctx, D), lambda b: (b, 0)),
        out_shape=jax.ShapeDtypeStruct((n_batch * n_ctx, D), BF16),
        compiler_params=_cparams(("arbitrary",)),
        name="ctx_attn",
    )(qkv, qkv, qkv)
    w = w_o.astype(BF16)
    zero_b = jnp.zeros((D,), F32)
    row_fn = lambda t: _mod_row(t, seq // TM_LIN, n_batch)
    hs = _lin_res_call(hs, (o_lat,), (), w, zero_b, mod_l, 0, n_lat // TM_LIN, row_fn)
    return _lin_res_call(hs, (o_ctx,), (), w, zero_b, mod_l, n_lat // TM_LIN, n_batch * n_ctx // TM_LIN, row_fn)


def kernel(x, c, ctx, c_ctx, w_mod, b_mod, w_ff_in, w_ff_out, fnet_w_out, fnet_b_out, conv_w_in, conv_b_in, conv_w_dw, conv_b_dw, conv_ln_g, conv_ln_b, conv_w_out, conv_b_out, na_w_qkv, na_q_gain, na_k_gain, na_rpb, na_w_o):
    n_batch, seq, d = x.shape
    n_ctx = ctx.shape[1]
    depth = w_mod.shape[0]
    assert d == D and n_batch < MOD_ROWS
    assert seq % TM_FFN == 0 and (n_batch * n_ctx) % TM_FFN == 0
    n_lat = n_batch * seq

    cond = jnp.concatenate([c, c_ctx[None, :], jnp.zeros((MOD_ROWS - n_batch - 1, D), F32)], axis=0)
    mod = _mod_call(cond, w_mod, b_mod).reshape(depth, MOD_ROWS, N_MOD, D)
    hs = jnp.concatenate([x.reshape(n_lat, D), ctx.reshape(n_batch * n_ctx, D)], axis=0)
    lat_tiles = n_lat // TM_FFN
    all_tiles = hs.shape[0] // TM_FFN

    for i in range(depth):
        kind, slot = i % 3, i // 3
        ctx_post = i < depth - 1
        ctx_pre = ctx_post or kind == 2
        mod_l = mod[i]
        hs = _ffn_call(hs, mod_l, w_ff_in, w_ff_out, i, 0, 0, all_tiles if ctx_pre else lat_tiles,
                       n_batch, seq, True)
        if kind == 0:
            hs = _fourier_mixer(hs, mod_l, fnet_w_out[slot], fnet_b_out[slot], n_batch, seq, n_ctx, ctx_post)
        elif kind == 1:
            assert ctx_post
            hs = _conv_mixer(hs, mod_l, conv_w_in[slot], conv_b_in[slot], conv_w_dw[slot], conv_b_dw[slot],
                             conv_ln_g[slot], conv_ln_b[slot], conv_w_out[slot], conv_b_out[slot],
                             n_batch, seq, n_ctx)
        else:
            assert ctx_post
            hs = _natten_mixer(hs, mod_l, na_w_qkv[slot], na_q_gain[slot], na_k_gain[slot], na_rpb[slot],
                               na_w_o[slot], n_batch, seq, n_ctx)
        final = i == depth - 1
        hs = _ffn_call(hs, mod_l, w_ff_in, w_ff_out, i, 1, 6, all_tiles if ctx_post else lat_tiles,
                       n_batch, seq, not final)
    return hs[:n_lat].reshape(n_batch, seq, D)
```

```python
import functools

import numpy as np
import jax
import jax.numpy as jnp
from jax import lax
from jax.experimental import pallas as pl
from jax.experimental.pallas import tpu as pltpu

F32 = jnp.float32
BF16 = jnp.bfloat16

D = 1024
D_FF = 2816
N_MOD = 9
EPS = 1e-6
GRID_W = 64
LANE = 128
SUBLANE = 8
N_GROUPS = 4
GROUP_C = D // N_GROUPS
CONV_K = 31
CONV_HALO = 16
N_HEADS = 16
HEAD_DIM = D // N_HEADS
WIN_R = 8
WIN_C = 16
NEG_INF = -1e30
MOD_ROWS = 8

FF_TILE = 256
TM_FFN = 1024
TM_LIN = 512
TM_CONV = 256
COL_TILE = 16
VMEM_LIMIT = 52 * 1024 * 1024


def _cparams(sem):
    return pltpu.CompilerParams(dimension_semantics=sem, vmem_limit_bytes=VMEM_LIMIT)


def _silu(x):
    return x / (1.0 + jnp.exp(-x))


def _norm_mod(x, shift, scale):
    ms = jnp.mean(x * x, axis=-1, keepdims=True)
    return x * lax.rsqrt(ms + EPS) * (1.0 + scale) + shift


def _dot(a, b):
    return jnp.dot(a, b, preferred_element_type=F32)


def _mod_kernel(c_ref, w_ref, b_ref, o_ref):
    s = _silu(c_ref[...]).astype(BF16)
    o_ref[...] = _dot(s, w_ref[...].astype(BF16)) + b_ref[...]


def _mod_call(cond, w_mod, b_mod):
    depth = w_mod.shape[0]
    return pl.pallas_call(
        _mod_kernel,
        grid=(depth, N_MOD),
        in_specs=[
            pl.BlockSpec((MOD_ROWS, D), lambda l, k: (0, 0)),
            pl.BlockSpec((None, D, D), lambda l, k: (l, 0, k)),
            pl.BlockSpec((None, 1, D), lambda l, k: (l, 0, k)),
        ],
        out_specs=pl.BlockSpec((None, MOD_ROWS, D), lambda l, k: (l, 0, k)),
        out_shape=jax.ShapeDtypeStruct((depth, MOD_ROWS, N_MOD * D), F32),
        compiler_params=_cparams(("arbitrary", "arbitrary")),
        name="mod",
    )(cond, w_mod, b_mod.reshape(depth, 1, N_MOD * D))


def _mod_row(tile, tiles_per_batch, n_batch):
    return jnp.minimum(tile // tiles_per_batch, n_batch)


def _ffn_cast_kernel(g_ref, u_ref, o_ref, gu_ref, ob_ref):
    gu_ref[0] = g_ref[...].astype(BF16)
    gu_ref[1] = u_ref[...].astype(BF16)
    ob_ref[...] = o_ref[...].astype(BF16)


def _ffn_weights_bf16(w_ff_in, w_ff_out):
    depth, n_sub = w_ff_in.shape[:2]
    n_ff = D_FF // FF_TILE
    return pl.pallas_call(
        _ffn_cast_kernel,
        grid=(depth, n_sub, n_ff),
        in_specs=[pl.BlockSpec((None, None, D, FF_TILE), lambda l, s, t: (l, s, 0, t)),
                  pl.BlockSpec((None, None, D, FF_TILE), lambda l, s, t: (l, s, 0, n_ff + t)),
                  pl.BlockSpec((None, None, FF_TILE, D), lambda l, s, t: (l, s, t, 0))],
        out_specs=[pl.BlockSpec((None, None, 2, None, D, FF_TILE), lambda l, s, t: (l, s, 0, t, 0, 0)),
                   pl.BlockSpec((None, None, FF_TILE, D), lambda l, s, t: (l, s, t, 0))],
        out_shape=[jax.ShapeDtypeStruct((depth, n_sub, 2, n_ff, D, FF_TILE), BF16),
                   jax.ShapeDtypeStruct(w_ff_out.shape, BF16)],
        compiler_params=_cparams(("arbitrary", "arbitrary", "arbitrary")),
        name="ffn_cast",
    )(w_ff_in, w_ff_in, w_ff_out)


def _ffn_kernel(h_ref, mod_ref, wg_ref, wu_ref, wo_ref, o_ref, xn_ref, acc_ref, *, k0, n_ff):
    j = pl.program_id(1)

    @pl.when(j == 0)
    def _():
        xn_ref[...] = _norm_mod(h_ref[...], mod_ref[k0:k0 + 1, :], mod_ref[k0 + 1:k0 + 2, :]).astype(BF16)
        acc_ref[...] = jnp.zeros_like(acc_ref)

    xn = xn_ref[...]
    g = _dot(xn, wg_ref[...])
    u = _dot(xn, wu_ref[...])
    acc_ref[...] += _dot((_silu(g) * u).astype(BF16), wo_ref[...])

    @pl.when(j == n_ff - 1)
    def _():
        o_ref[...] = h_ref[...] + (0.5 * mod_ref[k0 + 2:k0 + 3, :]) * acc_ref[...]


def _ffn_call(hs, mod_l, w_gu, w_o, layer, which, k0, n_tiles, n_batch, seq, alias):
    n_ff = D_FF // FF_TILE
    tpb = seq // TM_FFN
    out_rows = hs.shape[0] if alias else n_tiles * TM_FFN
    return pl.pallas_call(
        functools.partial(_ffn_kernel, k0=k0, n_ff=n_ff),
        grid=(n_tiles, n_ff),
        in_specs=[
            pl.BlockSpec((TM_FFN, D), lambda i, j: (i, 0)),
            pl.BlockSpec((None, N_MOD, D), lambda i, j: (_mod_row(i, tpb, n_batch), 0, 0)),
            pl.BlockSpec((None, None, None, None, D, FF_TILE), lambda i, j: (layer, which, 0, j, 0, 0)),
            pl.BlockSpec((None, None, None, None, D, FF_TILE), lambda i, j: (layer, which, 1, j, 0, 0)),
            pl.BlockSpec((None, None, FF_TILE, D), lambda i, j: (layer, which, j, 0)),
        ],
        out_specs=pl.BlockSpec((TM_FFN, D), lambda i, j: (i, 0)),
        out_shape=jax.ShapeDtypeStruct((out_rows, D), F32),
        scratch_shapes=[pltpu.VMEM((TM_FFN, D), BF16), pltpu.VMEM((TM_FFN, D), F32)],
        input_output_aliases={0: 0} if alias else {},
        compiler_params=_cparams(("arbitrary", "arbitrary")),
        name="ffn",
    )(hs, mod_l, w_gu, w_gu, w_o)


def _lin_res_kernel(*refs, fourier):
    if fourier:
        xr_ref, xi_ref, cc_ref, sc_ref, w_ref, b_ref, h_ref, mod_ref, o_ref = refs
        parts = []
        for g in range(N_GROUPS):
            sl = slice(g * GROUP_C, (g + 1) * GROUP_C)
            parts.append(_dot(xr_ref[:, sl], cc_ref[...]) + _dot(xi_ref[:, sl], sc_ref[...]))
        x = jnp.concatenate(parts, axis=-1).astype(BF16)
    else:
        x_ref, w_ref, b_ref, h_ref, mod_ref, o_ref = refs
        x = x_ref[...]
    y = _dot(x, w_ref[...]) + b_ref[...]
    o_ref[...] = h_ref[...] + mod_ref[5:6, :] * y


def _lin_res_call(hs, xs, tables, w, b, mod_l, tile0, n_tiles, mod_row_fn):
    fourier = len(xs) == 2
    x_spec = pl.BlockSpec((TM_LIN, D), lambda i: (i, 0))
    t_spec = pl.BlockSpec((GROUP_C, GROUP_C), lambda i: (0, 0))
    in_specs = [x_spec] * len(xs) + [t_spec] * len(tables) + [
        pl.BlockSpec((D, D), lambda i: (0, 0)),
        pl.BlockSpec((1, D), lambda i: (0, 0)),
        pl.BlockSpec((TM_LIN, D), lambda i: (i + tile0, 0)),
        pl.BlockSpec((None, N_MOD, D), lambda i: (mod_row_fn(i + tile0), 0, 0)),
    ]
    h_index = len(xs) + len(tables) + 2
    return pl.pallas_call(
        functools.partial(_lin_res_kernel, fourier=fourier),
        grid=(n_tiles,),
        in_specs=in_specs,
        out_specs=pl.BlockSpec((TM_LIN, D), lambda i: (i + tile0, 0)),
        out_shape=jax.ShapeDtypeStruct(hs.shape, F32),
        input_output_aliases={h_index: 0},
        compiler_params=_cparams(("arbitrary",)),
        name="lin_res_fourier" if fourier else "lin_res",
    )(*xs, *tables, w, b.reshape(1, D), hs, mod_l)


def _fnet_rows_kernel(h_ref, mod_ref, kf_ref, zr_ref, zi_ref):
    rows = h_ref.shape[0]
    xn = _norm_mod(h_ref[...], mod_ref[3:4, :], mod_ref[4:5, :])
    n = rows * SUBLANE
    zr, zi = [], []
    for c0 in range(0, COL_TILE, SUBLANE):
        a = xn[:, c0:c0 + SUBLANE, :].reshape(n, D).astype(BF16)
        z = _dot(kf_ref[...], a)
        zr.append(z[:n].reshape(rows, SUBLANE, D))
        zi.append(z[n:].reshape(rows, SUBLANE, D))
    zr_ref[...] = jnp.concatenate(zr, axis=1).astype(BF16)
    zi_ref[...] = jnp.concatenate(zi, axis=1).astype(BF16)


def _fnet_cols_kernel(zr_ref, zi_ref, m2_ref, cc_ref, sc_ref, w_ref, b_ref, h_ref, mod_ref, o_ref,
                      xr_ref, xi_ref, y_ref):
    n = zr_ref.shape[1]
    for j in range(SUBLANE):
        z = jnp.concatenate([zr_ref[j], zi_ref[j]], axis=0)
        x = _dot(m2_ref[j], z)
        xr_ref[j * n:(j + 1) * n, :] = x[:n].astype(BF16)
        xi_ref[j * n:(j + 1) * n, :] = x[n:].astype(BF16)
    parts = []
    for g in range(N_GROUPS):
        sl = slice(g * GROUP_C, (g + 1) * GROUP_C)
        parts.append(_dot(xr_ref[:, sl], cc_ref[...]) + _dot(xi_ref[:, sl], sc_ref[...]))
    y = _dot(jnp.concatenate(parts, axis=-1).astype(BF16), w_ref[...]) + b_ref[...]
    for j in range(SUBLANE):
        rows = pl.ds(j, n, stride=SUBLANE)
        for lt in range(D // LANE):
            y_ref[lt, rows, :] = y[j * n:(j + 1) * n, lt * LANE:(lt + 1) * LANE]
    y_t = jnp.concatenate([y_ref[lt] for lt in range(D // LANE)], axis=-1).reshape(n, SUBLANE, D)
    o_ref[...] = h_ref[...] + mod_ref[5:6, :] * y_t


def _fnet_ctx_kernel(h_ref, mod_ref, f_ref, xr_ref, xi_ref):
    n = h_ref.shape[0]
    xn = _norm_mod(h_ref[...], mod_ref[3:4, :], mod_ref[4:5, :]).astype(BF16)
    x = _dot(f_ref[...], xn)
    xr_ref[...] = x[:n].astype(BF16)
    xi_ref[...] = x[n:].astype(BF16)


def _dft_tables(rows):
    n_seq = rows * GRID_W
    a = np.arange(rows)
    th1 = 2.0 * np.pi * np.outer(a, a) / rows
    f1 = np.concatenate([np.cos(th1), -np.sin(th1)], axis=0)
    f1 = np.kron(f1, np.eye(SUBLANE))
    k2 = np.arange(GRID_W)[:, None, None]
    k1 = np.arange(rows)[None, :, None]
    n1 = np.arange(GRID_W)[None, None, :]
    th2 = 2.0 * np.pi * (n1 * (GRID_W * k1 + k2) % n_seq) / n_seq
    c2, s2 = np.cos(th2), np.sin(th2)
    m2 = np.concatenate([np.concatenate([c2, s2], axis=2), np.concatenate([-s2, c2], axis=2)], axis=1)
    return f1.astype(np.float32), m2.astype(np.float32)


def _channel_tables(n_seq):
    c = np.arange(GROUP_C)
    th = 2.0 * np.pi * np.outer(c, c) / GROUP_C
    s = 1.0 / np.sqrt(float(n_seq) * GROUP_C)
    return (np.cos(th) * s).astype(np.float32), (np.sin(th) * s).astype(np.float32)


def _fourier_mixer(hs, mod_l, w_out, b_out, n_batch, seq, n_ctx, with_ctx):
    rows = seq // GRID_W
    assert rows == GRID_W, "the two-stage DFT is written for a square token grid"
    n_lat = n_batch * seq
    f1, m2 = _dft_tables(rows)
    cc, sc = _channel_tables(seq)
    f1, m2, cc, sc = (jnp.asarray(t).astype(BF16) for t in (f1, m2, cc, sc))
    w = w_out.astype(BF16)
    n_ct = GRID_W // COL_TILE
    grid3 = hs.reshape(hs.shape[0] // GRID_W, GRID_W, D)
    z_shape = jax.ShapeDtypeStruct((n_batch * rows, GRID_W, D), BF16)
    blk_cols = pl.BlockSpec((rows, COL_TILE, D), lambda b, c: (b, c, 0))
    mod_b = pl.BlockSpec((None, N_MOD, D), lambda b, c: (b, 0, 0))
    zr, zi = pl.pallas_call(
        _fnet_rows_kernel,
        grid=(n_batch, n_ct),
        in_specs=[blk_cols, mod_b, pl.BlockSpec(f1.shape, lambda b, c: (0, 0))],
        out_specs=[blk_cols, blk_cols],
        out_shape=[z_shape, z_shape],
        compiler_params=_cparams(("arbitrary", "arbitrary")),
        name="fnet_rows",
    )(grid3, mod_l, f1)
    n_kt = GRID_W // SUBLANE
    blk_k2 = pl.BlockSpec((SUBLANE, GRID_W, D), lambda b, k: (b * n_kt + k, 0, 0))
    blk_tok = pl.BlockSpec((rows, SUBLANE, D), lambda b, k: (b, k, 0))
    table = pl.BlockSpec((GROUP_C, GROUP_C), lambda b, k: (0, 0))
    n_tok = rows * SUBLANE
    grid3 = pl.pallas_call(
        _fnet_cols_kernel,
        grid=(n_batch, n_kt),
        in_specs=[blk_k2, blk_k2,
                  pl.BlockSpec((SUBLANE, 2 * rows, 2 * GRID_W), lambda b, k: (k, 0, 0)),
                  table, table,
                  pl.BlockSpec((D, D), lambda b, k: (0, 0)),
                  pl.BlockSpec((1, D), lambda b, k: (0, 0)),
                  blk_tok, mod_b],
        out_specs=blk_tok,
        out_shape=jax.ShapeDtypeStruct(grid3.shape, F32),
        scratch_shapes=[pltpu.VMEM((n_tok, D), BF16), pltpu.VMEM((n_tok, D), BF16),
                        pltpu.VMEM((D // LANE, n_tok, LANE), F32)],
        input_output_aliases={7: 0},
        compiler_params=_cparams(("arbitrary", "arbitrary")),
        name="fnet_cols",
    )(zr, zi, m2, cc, sc, w, b_out.reshape(1, D), grid3, mod_l)
    hs = grid3.reshape(hs.shape)
    tpb = seq // TM_LIN
    row_fn = lambda t: _mod_row(t, tpb, n_batch)
    if with_ctx:
        a = np.arange(n_ctx)
        th = 2.0 * np.pi * np.outer(a, a) / n_ctx
        s = np.sqrt(float(seq) / n_ctx)
        fc = jnp.asarray(np.concatenate([np.cos(th) * s, -np.sin(th) * s], axis=0).astype(np.float32)).astype(BF16)
        c_shape = jax.ShapeDtypeStruct((n_batch * n_ctx, D), BF16)
        tile0 = n_lat // n_ctx
        xr, xi = pl.pallas_call(
            _fnet_ctx_kernel,
            grid=(n_batch,),
            in_specs=[pl.BlockSpec((n_ctx, D), lambda b: (tile0 + b, 0)),
                      pl.BlockSpec((None, N_MOD, D), lambda b: (n_batch, 0, 0)),
                      pl.BlockSpec((2 * n_ctx, n_ctx), lambda b: (0, 0))],
            out_specs=[pl.BlockSpec((n_ctx, D), lambda b: (b, 0))] * 2,
            out_shape=[c_shape, c_shape],
            compiler_params=_cparams(("arbitrary",)),
            name="fnet_ctx",
        )(hs, mod_l, fc)
        hs = _lin_res_call(hs, (xr, xi), (cc, sc), w, b_out, mod_l,
                           n_lat // TM_LIN, n_batch * n_ctx // TM_LIN, row_fn)
    return hs


def _glu_kernel(h_ref, mod_ref, wa_ref, wg_ref, ba_ref, bg_ref, o_ref, xn_ref):
    @pl.when(pl.program_id(1) == 0)
    def _():
        xn_ref[...] = _norm_mod(h_ref[...], mod_ref[3:4, :], mod_ref[4:5, :]).astype(BF16)

    xn = xn_ref[...]
    a = _dot(xn, wa_ref[...]) + ba_ref[...]
    g = _dot(xn, wg_ref[...]) + bg_ref[...]
    o_ref[...] = a / (1.0 + jnp.exp(-g))


def _conv_kernel(u_ref, up_ref, un_ref, wdw_ref, bdw_ref, lng_ref, lnb_ref, w_ref, b_ref, h_ref, mod_ref,
                 o_ref, buf_ref, v_ref, *, lat_tiles, tiles_per_seq):
    i = pl.program_id(0)
    is_ctx = i >= lat_tiles
    first = jnp.logical_or(is_ctx, i % tiles_per_seq == 0)
    last = jnp.logical_or(is_ctx, i % tiles_per_seq == tiles_per_seq - 1)
    tm = u_ref.shape[0]
    buf_ref[0, 0:CONV_HALO, :] = jnp.where(first, 0.0, up_ref[...])
    buf_ref[0, CONV_HALO:CONV_HALO + tm, :] = u_ref[...]
    buf_ref[0, CONV_HALO + tm:, :] = jnp.where(last, 0.0, un_ref[...])
    n_shifted = tm + 2 * CONV_HALO - SUBLANE
    for s in range(1, SUBLANE):
        buf_ref[s, 0:n_shifted, :] = buf_ref[0, s:s + n_shifted, :]
    chunk = 32
    off = CONV_HALO - CONV_K // 2
    for c0 in range(0, tm, chunk):
        acc = jnp.zeros((chunk, D), F32) + bdw_ref[...]
        for k in range(CONV_K):
            row = c0 + (off + k) // SUBLANE * SUBLANE
            acc = acc + wdw_ref[k:k + 1, :] * buf_ref[(off + k) % SUBLANE, row:row + chunk, :]
        mu = jnp.mean(acc, axis=-1, keepdims=True)
        d = acc - mu
        var = jnp.mean(d * d, axis=-1, keepdims=True)
        v = d * lax.rsqrt(var + EPS) * lng_ref[...] + lnb_ref[...]
        v_ref[c0:c0 + chunk, :] = _silu(v).astype(BF16)
    y = _dot(v_ref[...], w_ref[...]) + b_ref[...]
    o_ref[...] = h_ref[...] + mod_ref[5:6, :] * y


def _conv_mixer(hs, mod_l, w_in, b_in, w_dw, b_dw, ln_g, ln_b, w_out, b_out, n_batch, seq, n_ctx):
    n_lat = n_batch * seq
    n_all = n_lat + n_batch * n_ctx
    assert n_ctx == TM_CONV, "context sequences are one conv tile long"
    n_tiles = n_all // TM_FFN
    n_col = D // FF_TILE
    tpb = seq // TM_FFN
    col = pl.BlockSpec((D, FF_TILE), lambda i, j: (0, j))
    gcol = pl.BlockSpec((D, FF_TILE), lambda i, j: (0, n_col + j))
    u = pl.pallas_call(
        _glu_kernel,
        grid=(n_tiles, n_col),
        in_specs=[pl.BlockSpec((TM_FFN, D), lambda i, j: (i, 0)),
                  pl.BlockSpec((None, N_MOD, D), lambda i, j: (_mod_row(i, tpb, n_batch), 0, 0)),
                  col, gcol,
                  pl.BlockSpec((1, FF_TILE), lambda i, j: (0, j)),
                  pl.BlockSpec((1, FF_TILE), lambda i, j: (0, n_col + j))],
        out_specs=pl.BlockSpec((TM_FFN, FF_TILE), lambda i, j: (i, j)),
        out_shape=jax.ShapeDtypeStruct((n_all, D), F32),
        scratch_shapes=[pltpu.VMEM((TM_FFN, D), BF16)],
        compiler_params=_cparams(("arbitrary", "arbitrary")),
        name="conv_glu",
    )(hs, mod_l, w_in.astype(BF16), w_in.astype(BF16), b_in.reshape(1, 2 * D), b_in.reshape(1, 2 * D))

    n_ct = n_all // TM_CONV
    hpt = TM_CONV // CONV_HALO
    n_halo = n_all // CONV_HALO
    tps = seq // TM_CONV
    vec = pl.BlockSpec((1, D), lambda i: (0, 0))
    tile = pl.BlockSpec((TM_CONV, D), lambda i: (i, 0))
    return pl.pallas_call(
        functools.partial(_conv_kernel, lat_tiles=n_lat // TM_CONV, tiles_per_seq=tps),
        grid=(n_ct,),
        in_specs=[tile,
                  pl.BlockSpec((CONV_HALO, D), lambda i: (jnp.maximum(i * hpt - 1, 0), 0)),
                  pl.BlockSpec((CONV_HALO, D), lambda i: (jnp.minimum((i + 1) * hpt, n_halo - 1), 0)),
                  pl.BlockSpec((CONV_K, D), lambda i: (0, 0)),
                  vec, vec, vec,
                  pl.BlockSpec((D, D), lambda i: (0, 0)),
                  vec, tile,
                  pl.BlockSpec((None, N_MOD, D), lambda i: (_mod_row(i, tps, n_batch), 0, 0))],
        out_specs=tile,
        out_shape=jax.ShapeDtypeStruct(hs.shape, F32),
        scratch_shapes=[pltpu.VMEM((SUBLANE, TM_CONV + 2 * CONV_HALO, D), F32), pltpu.VMEM((TM_CONV, D), BF16)],
        input_output_aliases={9: 0},
        compiler_params=_cparams(("arbitrary",)),
        name="conv_dw",
    )(u, u, u, w_dw, b_dw.reshape(1, D), ln_g.reshape(1, D), ln_b.reshape(1, D), w_out.astype(BF16),
      b_out.reshape(1, D), hs, mod_l)


def _qkv_kernel(h_ref, mod_ref, w_ref, gain_ref, e_ref, o_ref, xn_ref, *, n_norm):
    j = pl.program_id(1)

    @pl.when(j == 0)
    def _():
        xn_ref[...] = _norm_mod(h_ref[...], mod_ref[3:4, :], mod_ref[4:5, :]).astype(BF16)

    y = _dot(xn_ref[...], w_ref[...])
    ss = _dot((y * y).astype(BF16), e_ref[...])
    rs = lax.rsqrt(ss * (1.0 / HEAD_DIM) + EPS) * gain_ref[...]
    o_ref[...] = (y * jnp.where(j < n_norm, rs, 1.0)).astype(BF16)


def _softmax_pv(s_list, v_list):
    m = s_list[0].max(axis=-1, keepdims=True)
    for s in s_list[1:]:
        m = jnp.maximum(m, s.max(axis=-1, keepdims=True))
    den = 0.0
    acc = 0.0
    for s, v in zip(s_list, v_list):
        p = jnp.exp(s - m)
        den = den + p.sum(axis=-1, keepdims=True)
        acc = acc + _dot(p.astype(BF16), v)
    return acc / den


def _qk(q, k):
    return lax.dot_general(q, k, (((1,), (1,)), ((), ())), preferred_element_type=F32)


def _natten_kernel(q_ref, k_ref, v_ref, kc_ref, vc_ref, bias_ref, o_ref, *, rows, kr):
    r = pl.program_id(1)
    rs = jnp.clip(r - kr // 2, 0, rows - kr)
    dr0 = rs - r + WIN_R - 1
    win = pl.ds(pl.multiple_of(rs * GRID_W, GRID_W), kr * GRID_W)
    left = lax.broadcasted_iota(jnp.int32, (GRID_W, LANE), 1) < HEAD_DIM
    for p in range(N_HEADS // 2):
        cols = slice(p * LANE, (p + 1) * LANE)
        q = q_ref[:, cols]
        zero = jnp.zeros_like(q)
        q2 = jnp.concatenate([jnp.where(left, q, zero), jnp.where(left, zero, q)], axis=0)
        bias = jnp.concatenate([bias_ref[p, dr0 + 2 * t] for t in range(kr // 2)], axis=-1)
        s_lat = _qk(q2, k_ref[win, cols]) + bias
        s_ctx = _qk(q2, kc_ref[:, cols])
        o = _softmax_pv([s_lat, s_ctx], [v_ref[win, cols], vc_ref[:, cols]])
        o_ref[:, cols] = jnp.where(left, o[:GRID_W], o[GRID_W:]).astype(BF16)


def _bias_kernel(rpb_ref, o_ref):
    qc = lax.broadcasted_iota(jnp.int32, (GRID_W, LANE), 0)
    lane = lax.broadcasted_iota(jnp.int32, (GRID_W, LANE), 1)
    kc = lane % GRID_W
    w0 = jnp.clip(qc - WIN_C // 2, 0, GRID_W - WIN_C)
    ok = jnp.logical_and(kc >= w0, kc < w0 + WIN_C)
    tiles = []
    for dr in range(2 * WIN_R - 1):
        t = jnp.broadcast_to(rpb_ref[dr:dr + 1, :], (GRID_W, LANE))
        t = pltpu.roll(t, LANE - (WIN_C - 1), axis=1)
        for bit in range(GRID_W.bit_length() - 1):
            t = jnp.where((qc >> bit) & 1 == 1, pltpu.roll(t, 1 << bit, axis=1), t)
        tiles.append(t)
    for p in range(2 * WIN_R - 2):
        pair = jnp.where(lane < GRID_W, tiles[p], pltpu.roll(tiles[p + 1], GRID_W, axis=1))
        o_ref[p] = jnp.where(ok, pair, NEG_INF)


def _ctx_attn_kernel(q_ref, k_ref, v_ref, o_ref):
    outs = []
    for h in range(N_HEADS):
        hd = slice(h * HEAD_DIM, (h + 1) * HEAD_DIM)
        outs.append(_softmax_pv([_qk(q_ref[:, hd], k_ref[:, hd])], [v_ref[:, hd]]))
    o_ref[...] = jnp.concatenate(outs, axis=-1).astype(BF16)


def _natten_bias(rpb):
    n_dr, n_dc = 2 * WIN_R - 1, 2 * WIN_C - 1
    assert rpb.shape == (N_HEADS, n_dr, n_dc) and GRID_W * 2 == LANE
    padded = jnp.pad(rpb.astype(F32), ((0, 0), (0, 0), (0, LANE - n_dc)))
    return pl.pallas_call(
        _bias_kernel,
        grid=(N_HEADS,),
        in_specs=[pl.BlockSpec((None, n_dr, LANE), lambda h: (h, 0, 0))],
        out_specs=pl.BlockSpec((None, n_dr - 1, GRID_W, LANE), lambda h: (h // 2, 0, h % 2, 0)),
        out_shape=jax.ShapeDtypeStruct((N_HEADS // 2, n_dr - 1, 2 * GRID_W, LANE), F32),
        compiler_params=_cparams(("arbitrary",)),
        name="natten_bias",
    )(padded)


def _natten_mixer(hs, mod_l, w_qkv, q_gain, k_gain, rpb, w_o, n_batch, seq, n_ctx):
    n_lat = n_batch * seq
    n_all = n_lat + n_batch * n_ctx
    rows = seq // GRID_W
    kr = min(WIN_R, rows)
    n_tiles = n_all // TM_FFN
    tpb = seq // TM_FFN
    n_col = 3 * D // FF_TILE
    scale = HEAD_DIM ** -0.5
    gain = jnp.concatenate([jnp.tile(q_gain, N_HEADS) * scale, jnp.tile(k_gain, N_HEADS),
                            jnp.ones((D,), F32)]).reshape(1, 3 * D)
    lane = np.arange(FF_TILE)
    e = jnp.asarray((lane[:, None] // HEAD_DIM == lane[None, :] // HEAD_DIM).astype(np.float32)).astype(BF16)
    qkv = pl.pallas_call(
        functools.partial(_qkv_kernel, n_norm=2 * D // FF_TILE),
        grid=(n_tiles, n_col),
        in_specs=[pl.BlockSpec((TM_FFN, D), lambda i, j: (i, 0)),
                  pl.BlockSpec((None, N_MOD, D), lambda i, j: (_mod_row(i, tpb, n_batch), 0, 0)),
                  pl.BlockSpec((D, FF_TILE), lambda i, j: (0, j)),
                  pl.BlockSpec((1, FF_TILE), lambda i, j: (0, j)),
                  pl.BlockSpec((FF_TILE, FF_TILE), lambda i, j: (0, 0))],
        out_specs=pl.BlockSpec((TM_FFN, FF_TILE), lambda i, j: (i, j)),
        out_shape=jax.ShapeDtypeStruct((n_all, 3 * D), BF16),
        scratch_shapes=[pltpu.VMEM((TM_FFN, D), BF16)],
        compiler_params=_cparams(("arbitrary", "arbitrary")),
        name="qkv",
    )(hs, mod_l, w_qkv.astype(BF16), gain, e)

    assert kr == WIN_R, "bias tiles pair up the key rows of a full window"
    bias = _natten_bias(rpb)
    ctx0 = n_lat // n_ctx
    once = pl.Buffered(1)
    o_lat = pl.pallas_call(
        functools.partial(_natten_kernel, rows=rows, kr=kr),
        grid=(n_batch, rows),
        in_specs=[pl.BlockSpec((GRID_W, D), lambda b, r: (b * rows + r, 0)),
                  pl.BlockSpec((seq, D), lambda b, r: (b, 1)),
                  pl.BlockSpec((seq, D), lambda b, r: (b, 2)),
                  pl.BlockSpec((n_ctx, D), lambda b, r: (ctx0 + b, 1)),
                  pl.BlockSpec((n_ctx, D), lambda b, r: (ctx0 + b, 2)),
                  pl.BlockSpec(bias.shape, lambda b, r: (0, 0, 0, 0), pipeline_mode=once)],
        out_specs=pl.BlockSpec((GRID_W, D), lambda b, r: (b * rows + r, 0)),
        out_shape=jax.ShapeDtypeStruct((n_lat, D), BF16),
        compiler_params=_cparams(("arbitrary", "arbitrary")),
        name="natten",
    )(qkv, qkv, qkv, qkv, qkv, bias)
    o_ctx = pl.pallas_call(
        _ctx_attn_kernel,
        grid=(n_batch,),
        in_specs=[pl.BlockSpec((n_ctx, D), lambda b: (ctx0 + b, 0)),
                  pl.BlockSpec((n_ctx, D), lambda b: (ctx0 + b, 1)),
                  pl.BlockSpec((n_ctx, D), lambda b: (ctx0 + b, 2))],
        out_specs=pl.BlockSpec((n_ctx, D), lambda b: (b, 0)),
        out_shape=jax.ShapeDtypeStruct((n_batch * n_ctx, D), BF16),
        compiler_params=_cparams(("arbitrary",)),
        name="ctx_attn",
    )(qkv, qkv, qkv)
    w = w_o.astype(BF16)
    zero_b = jnp.zeros((D,), F32)
    row_fn = lambda t: _mod_row(t, seq // TM_LIN, n_batch)
    hs = _lin_res_call(hs, (o_lat,), (), w, zero_b, mod_l, 0, n_lat // TM_LIN, row_fn)
    return _lin_res_call(hs, (o_ctx,), (), w, zero_b, mod_l, n_lat // TM_LIN, n_batch * n_ctx // TM_LIN, row_fn)


def kernel(x, c, ctx, c_ctx, w_mod, b_mod, w_ff_in, w_ff_out, fnet_w_out, fnet_b_out, conv_w_in, conv_b_in, conv_w_dw, conv_b_dw, conv_ln_g, conv_ln_b, conv_w_out, conv_b_out, na_w_qkv, na_q_gain, na_k_gain, na_rpb, na_w_o):
    n_batch, seq, d = x.shape
    n_ctx = ctx.shape[1]
    depth = w_mod.shape[0]
    assert d == D and n_batch < MOD_ROWS
    assert seq % TM_FFN == 0 and (n_batch * n_ctx) % TM_FFN == 0
    n_lat = n_batch * seq

    cond = jnp.concatenate([c, c_ctx[None, :], jnp.zeros((MOD_ROWS - n_batch - 1, D), F32)], axis=0)
    mod = _mod_call(cond, w_mod, b_mod).reshape(depth, MOD_ROWS, N_MOD, D)
    n_tok = n_lat + n_batch * n_ctx
    hs = jnp.concatenate([x.reshape(n_lat, D), ctx.reshape(n_batch * n_ctx, D),
                          jnp.zeros((-n_tok % seq, D), F32)], axis=0)
    lat_tiles = n_lat // TM_FFN
    all_tiles = n_tok // TM_FFN
    w_gu, w_o = _ffn_weights_bf16(w_ff_in, w_ff_out)

    for i in range(depth):
        kind, slot = i % 3, i // 3
        ctx_post = i < depth - 1
        ctx_pre = ctx_post or kind == 2
        mod_l = mod[i]
        hs = _ffn_call(hs, mod_l, w_gu, w_o, i, 0, 0, all_tiles if ctx_pre else lat_tiles,
                       n_batch, seq, True)
        if kind == 0:
            hs = _fourier_mixer(hs, mod_l, fnet_w_out[slot], fnet_b_out[slot], n_batch, seq, n_ctx, ctx_post)
        elif kind == 1:
            assert ctx_post
            hs = _conv_mixer(hs, mod_l, conv_w_in[slot], conv_b_in[slot], conv_w_dw[slot], conv_b_dw[slot],
                             conv_ln_g[slot], conv_ln_b[slot], conv_w_out[slot], conv_b_out[slot],
                             n_batch, seq, n_ctx)
        else:
            assert ctx_post
            hs = _natten_mixer(hs, mod_l, na_w_qkv[slot], na_q_gain[slot], na_k_gain[slot], na_rpb[slot],
                               na_w_o[slot], n_batch, seq, n_ctx)
        final = i == depth - 1
        hs = _ffn_call(hs, mod_l, w_gu, w_o, i, 1, 6, all_tiles if ctx_post else lat_tiles,
                       n_batch, seq, not final)
    return hs[:n_lat].reshape(n_batch, seq, D)
```

```python
import functools

import numpy as np
import jax
import jax.numpy as jnp
from jax import lax
from jax.experimental import pallas as pl
from jax.experimental.pallas import tpu as pltpu

F32 = jnp.float32
BF16 = jnp.bfloat16

D = 1024
D_FF = 2816
N_MOD = 9
EPS = 1e-6
GRID_W = 64
LANE = 128
SUBLANE = 8
N_GROUPS = 4
GROUP_C = D // N_GROUPS
CONV_K = 31
CONV_HALO = 16
N_HEADS = 16
HEAD_DIM = D // N_HEADS
WIN_R = 8
WIN_C = 16
NEG_INF = -1e30
MOD_ROWS = 8

FF_TILE = 256
TM_FFN = 1024
PROJ_ROWS = 512
GLU_COLS = 512
FFN_ROWS = 256
TM_LIN = 512
TM_CONV = 256
COL_TILE = 16
VMEM_LIMIT = 52 * 1024 * 1024


def _cparams(sem):
    return pltpu.CompilerParams(dimension_semantics=sem, vmem_limit_bytes=VMEM_LIMIT)


def _silu(x):
    return x / (1.0 + jnp.exp(-x))


def _norm_mod(x, shift, scale):
    ms = jnp.mean(x * x, axis=-1, keepdims=True)
    return x * lax.rsqrt(ms + EPS) * (1.0 + scale) + shift


def _dot(a, b):
    return jnp.dot(a, b, preferred_element_type=F32)


def _mod_kernel(c_ref, w_ref, b_ref, o_ref):
    s = _silu(c_ref[...]).astype(BF16)
    o_ref[...] = _dot(s, w_ref[...].astype(BF16)) + b_ref[...]


def _mod_call(cond, w_mod, b_mod):
    depth = w_mod.shape[0]
    return pl.pallas_call(
        _mod_kernel,
        grid=(depth, N_MOD),
        in_specs=[
            pl.BlockSpec((MOD_ROWS, D), lambda l, k: (0, 0)),
            pl.BlockSpec((None, D, D), lambda l, k: (l, 0, k)),
            pl.BlockSpec((None, 1, D), lambda l, k: (l, 0, k)),
        ],
        out_specs=pl.BlockSpec((None, MOD_ROWS, D), lambda l, k: (l, 0, k)),
        out_shape=jax.ShapeDtypeStruct((depth, MOD_ROWS, N_MOD * D), F32),
        compiler_params=_cparams(("arbitrary", "arbitrary")),
        name="mod",
    )(cond, w_mod, b_mod.reshape(depth, 1, N_MOD * D))


def _mod_row(tile, tiles_per_batch, n_batch):
    return jnp.minimum(tile // tiles_per_batch, n_batch)


def _ffn_kernel(h_ref, mod_ref, wg_ref, wu_ref, wo_ref, o_ref, xn_ref, acc_ref, *, k0, n_ff):
    j = pl.program_id(1)
    n_chunks = TM_FFN // FFN_ROWS
    assert n_ff >= 2

    def step(first, last):
        wg, wu, wo = wg_ref[...].astype(BF16), wu_ref[...].astype(BF16), wo_ref[...].astype(BF16)
        gu = {}
        for c in range(n_chunks + 1):
            if c < n_chunks:
                rows = slice(c * FFN_ROWS, (c + 1) * FFN_ROWS)
                if first:
                    xn = _norm_mod(h_ref[rows, :], mod_ref[k0:k0 + 1, :], mod_ref[k0 + 1:k0 + 2, :]).astype(BF16)
                    xn_ref[rows, :] = xn
                else:
                    xn = xn_ref[rows, :]
                gu[c] = (_dot(xn, wg), _dot(xn, wu))
            if c >= 1:
                rows = slice((c - 1) * FFN_ROWS, c * FFN_ROWS)
                g, u = gu.pop(c - 1)
                y = _dot((_silu(g) * u).astype(BF16), wo)
                if first:
                    acc_ref[rows, :] = y
                elif last:
                    o_ref[rows, :] = h_ref[rows, :] + (0.5 * mod_ref[k0 + 2:k0 + 3, :]) * (acc_ref[rows, :] + y)
                else:
                    acc_ref[rows, :] += y

    pl.when(j == 0)(functools.partial(step, True, False))
    pl.when(jnp.logical_and(j > 0, j < n_ff - 1))(functools.partial(step, False, False))
    pl.when(j == n_ff - 1)(functools.partial(step, False, True))


def _ffn_call(hs, mod_l, w_gu, w_o, layer, which, k0, n_tiles, n_batch, seq, alias):
    n_ff = D_FF // FF_TILE
    tpb = seq // TM_FFN
    out_rows = hs.shape[0] if alias else n_tiles * TM_FFN
    return pl.pallas_call(
        functools.partial(_ffn_kernel, k0=k0, n_ff=n_ff),
        grid=(n_tiles, n_ff),
        in_specs=[
            pl.BlockSpec((TM_FFN, D), lambda i, j: (i, 0)),
            pl.BlockSpec((None, N_MOD, D), lambda i, j: (_mod_row(i, tpb, n_batch), 0, 0)),
            pl.BlockSpec((None, None, D, FF_TILE), lambda i, j: (layer, which, 0, j)),
            pl.BlockSpec((None, None, D, FF_TILE), lambda i, j: (layer, which, 0, n_ff + j)),
            pl.BlockSpec((None, None, FF_TILE, D), lambda i, j: (layer, which, j, 0)),
        ],
        out_specs=pl.BlockSpec((TM_FFN, D), lambda i, j: (i, 0)),
        out_shape=jax.ShapeDtypeStruct((out_rows, D), F32),
        scratch_shapes=[pltpu.VMEM((TM_FFN, D), BF16), pltpu.VMEM((TM_FFN, D), F32)],
        input_output_aliases={0: 0} if alias else {},
        compiler_params=_cparams(("arbitrary", "arbitrary")),
        name="ffn",
    )(hs, mod_l, w_gu, w_gu, w_o)


def _lin_res_kernel(*refs, fourier):
    if fourier:
        xr_ref, xi_ref, cc_ref, sc_ref, w_ref, b_ref, h_ref, mod_ref, o_ref = refs
        parts = []
        for g in range(N_GROUPS):
            sl = slice(g * GROUP_C, (g + 1) * GROUP_C)
            parts.append(_dot(xr_ref[:, sl], cc_ref[...]) + _dot(xi_ref[:, sl], sc_ref[...]))
        x = jnp.concatenate(parts, axis=-1).astype(BF16)
    else:
        x_ref, w_ref, b_ref, h_ref, mod_ref, o_ref = refs
        x = x_ref[...]
    y = _dot(x, w_ref[...]) + b_ref[...]
    o_ref[...] = h_ref[...] + mod_ref[5:6, :] * y


def _lin_res_call(hs, xs, tables, w, b, mod_l, tile0, n_tiles, mod_row_fn):
    fourier = len(xs) == 2
    x_spec = pl.BlockSpec((TM_LIN, D), lambda i: (i, 0))
    t_spec = pl.BlockSpec((GROUP_C, GROUP_C), lambda i: (0, 0))
    in_specs = [x_spec] * len(xs) + [t_spec] * len(tables) + [
        pl.BlockSpec((D, D), lambda i: (0, 0)),
        pl.BlockSpec((1, D), lambda i: (0, 0)),
        pl.BlockSpec((TM_LIN, D), lambda i: (i + tile0, 0)),
        pl.BlockSpec((None, N_MOD, D), lambda i: (mod_row_fn(i + tile0), 0, 0)),
    ]
    h_index = len(xs) + len(tables) + 2
    return pl.pallas_call(
        functools.partial(_lin_res_kernel, fourier=fourier),
        grid=(n_tiles,),
        in_specs=in_specs,
        out_specs=pl.BlockSpec((TM_LIN, D), lambda i: (i + tile0, 0)),
        out_shape=jax.ShapeDtypeStruct(hs.shape, F32),
        input_output_aliases={h_index: 0},
        compiler_params=_cparams(("arbitrary",)),
        name="lin_res_fourier" if fourier else "lin_res",
    )(*xs, *tables, w, b.reshape(1, D), hs, mod_l)


def _fnet_rows_kernel(h_ref, mod_ref, kf_ref, zr_ref, zi_ref):
    rows = h_ref.shape[0]
    xn = _norm_mod(h_ref[...], mod_ref[3:4, :], mod_ref[4:5, :])
    n = rows * SUBLANE
    zr, zi = [], []
    for c0 in range(0, COL_TILE, SUBLANE):
        a = xn[:, c0:c0 + SUBLANE, :].reshape(n, D).astype(BF16)
        z = _dot(kf_ref[...], a)
        zr.append(z[:n].reshape(rows, SUBLANE, D))
        zi.append(z[n:].reshape(rows, SUBLANE, D))
    zr_ref[...] = jnp.concatenate(zr, axis=1).astype(BF16)
    zi_ref[...] = jnp.concatenate(zi, axis=1).astype(BF16)


def _fnet_cols_kernel(zr_ref, zi_ref, m2_ref, cc_ref, sc_ref, w_ref, b_ref, h_ref, mod_ref, o_ref,
                      xr_ref, xi_ref, y_ref):
    n = zr_ref.shape[1]
    for j in range(SUBLANE):
        z = jnp.concatenate([zr_ref[j], zi_ref[j]], axis=0)
        x = _dot(m2_ref[j], z)
        xr_ref[j * n:(j + 1) * n, :] = x[:n].astype(BF16)
        xi_ref[j * n:(j + 1) * n, :] = x[n:].astype(BF16)
    parts = []
    for g in range(N_GROUPS):
        sl = slice(g * GROUP_C, (g + 1) * GROUP_C)
        parts.append(_dot(xr_ref[:, sl], cc_ref[...]) + _dot(xi_ref[:, sl], sc_ref[...]))
    y = _dot(jnp.concatenate(parts, axis=-1).astype(BF16), w_ref[...]) + b_ref[...]
    for j in range(SUBLANE):
        rows = pl.ds(j, n, stride=SUBLANE)
        for lt in range(D // LANE):
            y_ref[lt, rows, :] = y[j * n:(j + 1) * n, lt * LANE:(lt + 1) * LANE]
    y_t = jnp.concatenate([y_ref[lt] for lt in range(D // LANE)], axis=-1).reshape(n, SUBLANE, D)
    o_ref[...] = h_ref[...] + mod_ref[5:6, :] * y_t


def _fnet_ctx_kernel(h_ref, mod_ref, f_ref, xr_ref, xi_ref):
    n = h_ref.shape[0]
    xn = _norm_mod(h_ref[...], mod_ref[3:4, :], mod_ref[4:5, :]).astype(BF16)
    x = _dot(f_ref[...], xn)
    xr_ref[...] = x[:n].astype(BF16)
    xi_ref[...] = x[n:].astype(BF16)


def _dft_tables(rows):
    n_seq = rows * GRID_W
    a = np.arange(rows)
    th1 = 2.0 * np.pi * np.outer(a, a) / rows
    f1 = np.concatenate([np.cos(th1), -np.sin(th1)], axis=0)
    f1 = np.kron(f1, np.eye(SUBLANE))
    k2 = np.arange(GRID_W)[:, None, None]
    k1 = np.arange(rows)[None, :, None]
    n1 = np.arange(GRID_W)[None, None, :]
    th2 = 2.0 * np.pi * (n1 * (GRID_W * k1 + k2) % n_seq) / n_seq
    c2, s2 = np.cos(th2), np.sin(th2)
    m2 = np.concatenate([np.concatenate([c2, s2], axis=2), np.concatenate([-s2, c2], axis=2)], axis=1)
    return f1.astype(np.float32), m2.astype(np.float32)


def _channel_tables(n_seq):
    c = np.arange(GROUP_C)
    th = 2.0 * np.pi * np.outer(c, c) / GROUP_C
    s = 1.0 / np.sqrt(float(n_seq) * GROUP_C)
    return (np.cos(th) * s).astype(np.float32), (np.sin(th) * s).astype(np.float32)


def _fourier_mixer(hs, mod_l, w_out, b_out, n_batch, seq, n_ctx, with_ctx):
    rows = seq // GRID_W
    assert rows == GRID_W, "the two-stage DFT is written for a square token grid"
    n_lat = n_batch * seq
    f1, m2 = _dft_tables(rows)
    cc, sc = _channel_tables(seq)
    f1, m2, cc, sc = (jnp.asarray(t).astype(BF16) for t in (f1, m2, cc, sc))
    w = w_out.astype(BF16)
    n_ct = GRID_W // COL_TILE
    grid3 = hs.reshape(hs.shape[0] // GRID_W, GRID_W, D)
    z_shape = jax.ShapeDtypeStruct((n_batch * rows, GRID_W, D), BF16)
    blk_cols = pl.BlockSpec((rows, COL_TILE, D), lambda b, c: (b, c, 0))
    mod_b = pl.BlockSpec((None, N_MOD, D), lambda b, c: (b, 0, 0))
    zr, zi = pl.pallas_call(
        _fnet_rows_kernel,
        grid=(n_batch, n_ct),
        in_specs=[blk_cols, mod_b, pl.BlockSpec(f1.shape, lambda b, c: (0, 0))],
        out_specs=[blk_cols, blk_cols],
        out_shape=[z_shape, z_shape],
        compiler_params=_cparams(("arbitrary", "arbitrary")),
        name="fnet_rows",
    )(grid3, mod_l, f1)
    n_kt = GRID_W // SUBLANE
    blk_k2 = pl.BlockSpec((SUBLANE, GRID_W, D), lambda b, k: (b * n_kt + k, 0, 0))
    blk_tok = pl.BlockSpec((rows, SUBLANE, D), lambda b, k: (b, k, 0))
    table = pl.BlockSpec((GROUP_C, GROUP_C), lambda b, k: (0, 0))
    n_tok = rows * SUBLANE
    grid3 = pl.pallas_call(
        _fnet_cols_kernel,
        grid=(n_batch, n_kt),
        in_specs=[blk_k2, blk_k2,
                  pl.BlockSpec((SUBLANE, 2 * rows, 2 * GRID_W), lambda b, k: (k, 0, 0)),
                  table, table,
                  pl.BlockSpec((D, D), lambda b, k: (0, 0)),
                  pl.BlockSpec((1, D), lambda b, k: (0, 0)),
                  blk_tok, mod_b],
        out_specs=blk_tok,
        out_shape=jax.ShapeDtypeStruct(grid3.shape, F32),
        scratch_shapes=[pltpu.VMEM((n_tok, D), BF16), pltpu.VMEM((n_tok, D), BF16),
                        pltpu.VMEM((D // LANE, n_tok, LANE), F32)],
        input_output_aliases={7: 0},
        compiler_params=_cparams(("arbitrary", "arbitrary")),
        name="fnet_cols",
    )(zr, zi, m2, cc, sc, w, b_out.reshape(1, D), grid3, mod_l)
    hs = grid3.reshape(hs.shape)
    tpb = seq // TM_LIN
    row_fn = lambda t: _mod_row(t, tpb, n_batch)
    if with_ctx:
        a = np.arange(n_ctx)
        th = 2.0 * np.pi * np.outer(a, a) / n_ctx
        s = np.sqrt(float(seq) / n_ctx)
        fc = jnp.asarray(np.concatenate([np.cos(th) * s, -np.sin(th) * s], axis=0).astype(np.float32)).astype(BF16)
        c_shape = jax.ShapeDtypeStruct((n_batch * n_ctx, D), BF16)
        tile0 = n_lat // n_ctx
        xr, xi = pl.pallas_call(
            _fnet_ctx_kernel,
            grid=(n_batch,),
            in_specs=[pl.BlockSpec((n_ctx, D), lambda b: (tile0 + b, 0)),
                      pl.BlockSpec((None, N_MOD, D), lambda b: (n_batch, 0, 0)),
                      pl.BlockSpec((2 * n_ctx, n_ctx), lambda b: (0, 0))],
            out_specs=[pl.BlockSpec((n_ctx, D), lambda b: (b, 0))] * 2,
            out_shape=[c_shape, c_shape],
            compiler_params=_cparams(("arbitrary",)),
            name="fnet_ctx",
        )(hs, mod_l, fc)
        hs = _lin_res_call(hs, (xr, xi), (cc, sc), w, b_out, mod_l,
                           n_lat // TM_LIN, n_batch * n_ctx // TM_LIN, row_fn)
    return hs


def _mixer_input_chunks(h_ref, mod_ref, xn_ref, first):
    for r0 in range(0, h_ref.shape[0], PROJ_ROWS):
        rows = slice(r0, r0 + PROJ_ROWS)
        if first:
            xn = _norm_mod(h_ref[rows, :], mod_ref[3:4, :], mod_ref[4:5, :]).astype(BF16)
            xn_ref[rows, :] = xn
        else:
            xn = xn_ref[rows, :]
        yield rows, xn


def _glu_kernel(h_ref, mod_ref, wa_ref, wg_ref, ba_ref, bg_ref, o_ref, xn_ref):
    def step(first):
        for rows, xn in _mixer_input_chunks(h_ref, mod_ref, xn_ref, first):
            a = _dot(xn, wa_ref[...]) + ba_ref[...]
            g = _dot(xn, wg_ref[...]) + bg_ref[...]
            o_ref[rows, :] = a / (1.0 + jnp.exp(-g))

    j = pl.program_id(1)
    pl.when(j == 0)(functools.partial(step, True))
    pl.when(j > 0)(functools.partial(step, False))


def _conv_kernel(u_ref, up_ref, un_ref, wdw_ref, bdw_ref, lng_ref, lnb_ref, w_ref, b_ref, h_ref, mod_ref,
                 o_ref, buf_ref, v_ref, *, lat_tiles, tiles_per_seq):
    i = pl.program_id(0)
    is_ctx = i >= lat_tiles
    first = jnp.logical_or(is_ctx, i % tiles_per_seq == 0)
    last = jnp.logical_or(is_ctx, i % tiles_per_seq == tiles_per_seq - 1)
    tm = u_ref.shape[0]
    buf_ref[0, 0:CONV_HALO, :] = jnp.where(first, 0.0, up_ref[...])
    buf_ref[0, CONV_HALO:CONV_HALO + tm, :] = u_ref[...]
    buf_ref[0, CONV_HALO + tm:, :] = jnp.where(last, 0.0, un_ref[...])
    n_shifted = tm + 2 * CONV_HALO - SUBLANE
    for s in range(1, SUBLANE):
        buf_ref[s, 0:n_shifted, :] = buf_ref[0, s:s + n_shifted, :]
    chunk = 32
    off = CONV_HALO - CONV_K // 2
    for c0 in range(0, tm, chunk):
        acc = jnp.zeros((chunk, D), F32) + bdw_ref[...]
        for k in range(CONV_K):
            row = c0 + (off + k) // SUBLANE * SUBLANE
            acc = acc + wdw_ref[k:k + 1, :] * buf_ref[(off + k) % SUBLANE, row:row + chunk, :]
        mu = jnp.mean(acc, axis=-1, keepdims=True)
        d = acc - mu
        var = jnp.mean(d * d, axis=-1, keepdims=True)
        v = d * lax.rsqrt(var + EPS) * lng_ref[...] + lnb_ref[...]
        v_ref[c0:c0 + chunk, :] = _silu(v).astype(BF16)
    y = _dot(v_ref[...], w_ref[...]) + b_ref[...]
    o_ref[...] = h_ref[...] + mod_ref[5:6, :] * y


def _conv_mixer(hs, mod_l, w_in, b_in, w_dw, b_dw, ln_g, ln_b, w_out, b_out, n_batch, seq, n_ctx):
    n_lat = n_batch * seq
    n_all = n_lat + n_batch * n_ctx
    assert n_ctx == TM_CONV, "context sequences are one conv tile long"
    n_tiles = n_all // TM_FFN
    n_col = D // GLU_COLS
    tpb = seq // TM_FFN
    col = pl.BlockSpec((D, GLU_COLS), lambda i, j: (0, j))
    gcol = pl.BlockSpec((D, GLU_COLS), lambda i, j: (0, n_col + j))
    u = pl.pallas_call(
        _glu_kernel,
        grid=(n_tiles, n_col),
        in_specs=[pl.BlockSpec((TM_FFN, D), lambda i, j: (i, 0)),
                  pl.BlockSpec((None, N_MOD, D), lambda i, j: (_mod_row(i, tpb, n_batch), 0, 0)),
                  col, gcol,
                  pl.BlockSpec((1, GLU_COLS), lambda i, j: (0, j)),
                  pl.BlockSpec((1, GLU_COLS), lambda i, j: (0, n_col + j))],
        out_specs=pl.BlockSpec((TM_FFN, GLU_COLS), lambda i, j: (i, j)),
        out_shape=jax.ShapeDtypeStruct((n_all, D), F32),
        scratch_shapes=[pltpu.VMEM((TM_FFN, D), BF16)],
        compiler_params=_cparams(("arbitrary", "arbitrary")),
        name="conv_glu",
    )(hs, mod_l, w_in.astype(BF16), w_in.astype(BF16), b_in.reshape(1, 2 * D), b_in.reshape(1, 2 * D))

    n_ct = n_all // TM_CONV
    hpt = TM_CONV // CONV_HALO
    n_halo = n_all // CONV_HALO
    tps = seq // TM_CONV
    vec = pl.BlockSpec((1, D), lambda i: (0, 0))
    tile = pl.BlockSpec((TM_CONV, D), lambda i: (i, 0))
    return pl.pallas_call(
        functools.partial(_conv_kernel, lat_tiles=n_lat // TM_CONV, tiles_per_seq=tps),
        grid=(n_ct,),
        in_specs=[tile,
                  pl.BlockSpec((CONV_HALO, D), lambda i: (jnp.maximum(i * hpt - 1, 0), 0)),
                  pl.BlockSpec((CONV_HALO, D), lambda i: (jnp.minimum((i + 1) * hpt, n_halo - 1), 0)),
                  pl.BlockSpec((CONV_K, D), lambda i: (0, 0)),
                  vec, vec, vec,
                  pl.BlockSpec((D, D), lambda i: (0, 0)),
                  vec, tile,
                  pl.BlockSpec((None, N_MOD, D), lambda i: (_mod_row(i, tps, n_batch), 0, 0))],
        out_specs=tile,
        out_shape=jax.ShapeDtypeStruct(hs.shape, F32),
        scratch_shapes=[pltpu.VMEM((SUBLANE, TM_CONV + 2 * CONV_HALO, D), F32), pltpu.VMEM((TM_CONV, D), BF16)],
        input_output_aliases={9: 0},
        compiler_params=_cparams(("arbitrary",)),
        name="conv_dw",
    )(u, u, u, w_dw, b_dw.reshape(1, D), ln_g.reshape(1, D), ln_b.reshape(1, D), w_out.astype(BF16),
      b_out.reshape(1, D), hs, mod_l)


def _qkv_kernel(h_ref, mod_ref, w_ref, gain_ref, e_ref, o_ref, xn_ref):
    n_e = e_ref.shape[0]

    def step(first, head_norm):
        for rows, xn in _mixer_input_chunks(h_ref, mod_ref, xn_ref, first):
            y = _dot(xn, w_ref[...])
            if not head_norm:
                o_ref[rows, :] = y.astype(BF16)
                continue
            for c0 in range(0, D, n_e):
                cols = slice(c0, c0 + n_e)
                yc = y[:, cols]
                ss = _dot((yc * yc).astype(BF16), e_ref[...])
                rs = lax.rsqrt(ss * (1.0 / HEAD_DIM) + EPS) * gain_ref[:, cols]
                o_ref[rows, cols] = (yc * rs).astype(BF16)

    j = pl.program_id(1)
    pl.when(j == 0)(functools.partial(step, True, True))
    pl.when(j == 1)(functools.partial(step, False, True))
    pl.when(j == 2)(functools.partial(step, False, False))


def _softmax_pv(s_list, v_list):
    m = s_list[0].max(axis=-1, keepdims=True)
    for s in s_list[1:]:
        m = jnp.maximum(m, s.max(axis=-1, keepdims=True))
    den = 0.0
    acc = 0.0
    for s, v in zip(s_list, v_list):
        p = jnp.exp(s - m)
        den = den + p.sum(axis=-1, keepdims=True)
        acc = acc + _dot(p.astype(BF16), v)
    return acc / den


def _qk(q, k):
    return lax.dot_general(q, k, (((1,), (1,)), ((), ())), preferred_element_type=F32)


def _natten_kernel(q_ref, k_ref, v_ref, kc_ref, vc_ref, bias_ref, o_ref, *, rows, kr):
    r = pl.program_id(1)
    rs = jnp.clip(r - kr // 2, 0, rows - kr)
    dr0 = rs - r + WIN_R - 1
    win = pl.ds(pl.multiple_of(rs * GRID_W, GRID_W), kr * GRID_W)
    left = lax.broadcasted_iota(jnp.int32, (GRID_W, LANE), 1) < HEAD_DIM
    for p in range(N_HEADS // 2):
        cols = slice(p * LANE, (p + 1) * LANE)
        q = q_ref[:, cols]
        zero = jnp.zeros_like(q)
        q2 = jnp.concatenate([jnp.where(left, q, zero), jnp.where(left, zero, q)], axis=0)
        bias = jnp.concatenate([bias_ref[p, dr0 + 2 * t] for t in range(kr // 2)], axis=-1)
        s_lat = _qk(q2, k_ref[win, cols]) + bias
        s_ctx = _qk(q2, kc_ref[:, cols])
        o = _softmax_pv([s_lat, s_ctx], [v_ref[win, cols], vc_ref[:, cols]])
        o_ref[:, cols] = jnp.where(left, o[:GRID_W], o[GRID_W:]).astype(BF16)


def _bias_kernel(rpb_ref, o_ref):
    qc = lax.broadcasted_iota(jnp.int32, (GRID_W, LANE), 0)
    lane = lax.broadcasted_iota(jnp.int32, (GRID_W, LANE), 1)
    kc = lane % GRID_W
    w0 = jnp.clip(qc - WIN_C // 2, 0, GRID_W - WIN_C)
    ok = jnp.logical_and(kc >= w0, kc < w0 + WIN_C)
    tiles = []
    for dr in range(2 * WIN_R - 1):
        t = jnp.broadcast_to(rpb_ref[dr:dr + 1, :], (GRID_W, LANE))
        t = pltpu.roll(t, LANE - (WIN_C - 1), axis=1)
        for bit in range(GRID_W.bit_length() - 1):
            t = jnp.where((qc >> bit) & 1 == 1, pltpu.roll(t, 1 << bit, axis=1), t)
        tiles.append(t)
    for p in range(2 * WIN_R - 2):
        pair = jnp.where(lane < GRID_W, tiles[p], pltpu.roll(tiles[p + 1], GRID_W, axis=1))
        o_ref[p] = jnp.where(ok, pair, NEG_INF)


def _ctx_attn_kernel(q_ref, k_ref, v_ref, o_ref):
    outs = []
    for h in range(N_HEADS):
        hd = slice(h * HEAD_DIM, (h + 1) * HEAD_DIM)
        outs.append(_softmax_pv([_qk(q_ref[:, hd], k_ref[:, hd])], [v_ref[:, hd]]))
    o_ref[...] = jnp.concatenate(outs, axis=-1).astype(BF16)


def _natten_bias(rpb):
    n_dr, n_dc = 2 * WIN_R - 1, 2 * WIN_C - 1
    assert rpb.shape == (N_HEADS, n_dr, n_dc) and GRID_W * 2 == LANE
    padded = jnp.pad(rpb.astype(F32), ((0, 0), (0, 0), (0, LANE - n_dc)))
    return pl.pallas_call(
        _bias_kernel,
        grid=(N_HEADS,),
        in_specs=[pl.BlockSpec((None, n_dr, LANE), lambda h: (h, 0, 0))],
        out_specs=pl.BlockSpec((None, n_dr - 1, GRID_W, LANE), lambda h: (h // 2, 0, h % 2, 0)),
        out_shape=jax.ShapeDtypeStruct((N_HEADS // 2, n_dr - 1, 2 * GRID_W, LANE), F32),
        compiler_params=_cparams(("arbitrary",)),
        name="natten_bias",
    )(padded)


def _natten_mixer(hs, mod_l, w_qkv, q_gain, k_gain, rpb, w_o, n_batch, seq, n_ctx):
    n_lat = n_batch * seq
    n_all = n_lat + n_batch * n_ctx
    rows = seq // GRID_W
    kr = min(WIN_R, rows)
    n_tiles = n_all // TM_FFN
    tpb = seq // TM_FFN
    scale = HEAD_DIM ** -0.5
    gain = jnp.concatenate([jnp.tile(q_gain, N_HEADS) * scale, jnp.tile(k_gain, N_HEADS),
                            jnp.ones((D,), F32)]).reshape(1, 3 * D)
    lane = np.arange(FF_TILE)
    e = jnp.asarray((lane[:, None] // HEAD_DIM == lane[None, :] // HEAD_DIM).astype(np.float32)).astype(BF16)
    qkv = pl.pallas_call(
        _qkv_kernel,
        grid=(n_tiles, 3),
        in_specs=[pl.BlockSpec((TM_FFN, D), lambda i, j: (i, 0)),
                  pl.BlockSpec((None, N_MOD, D), lambda i, j: (_mod_row(i, tpb, n_batch), 0, 0)),
                  pl.BlockSpec((D, D), lambda i, j: (0, j)),
                  pl.BlockSpec((1, D), lambda i, j: (0, j)),
                  pl.BlockSpec((FF_TILE, FF_TILE), lambda i, j: (0, 0))],
        out_specs=pl.BlockSpec((TM_FFN, D), lambda i, j: (i, j)),
        out_shape=jax.ShapeDtypeStruct((n_all, 3 * D), BF16),
        scratch_shapes=[pltpu.VMEM((TM_FFN, D), BF16)],
        compiler_params=_cparams(("arbitrary", "arbitrary")),
        name="qkv",
    )(hs, mod_l, w_qkv.astype(BF16), gain, e)

    assert kr == WIN_R, "bias tiles pair up the key rows of a full window"
    bias = _natten_bias(rpb)
    ctx0 = n_lat // n_ctx
    once = pl.Buffered(1)
    o_lat = pl.pallas_call(
        functools.partial(_natten_kernel, rows=rows, kr=kr),
        grid=(n_batch, rows),
        in_specs=[pl.BlockSpec((GRID_W, D), lambda b, r: (b * rows + r, 0)),
                  pl.BlockSpec((seq, D), lambda b, r: (b, 1)),
                  pl.BlockSpec((seq, D), lambda b, r: (b, 2)),
                  pl.BlockSpec((n_ctx, D), lambda b, r: (ctx0 + b, 1)),
                  pl.BlockSpec((n_ctx, D), lambda b, r: (ctx0 + b, 2)),
                  pl.BlockSpec(bias.shape, lambda b, r: (0, 0, 0, 0), pipeline_mode=once)],
        out_specs=pl.BlockSpec((GRID_W, D), lambda b, r: (b * rows + r, 0)),
        out_shape=jax.ShapeDtypeStruct((n_lat, D), BF16),
        compiler_params=_cparams(("arbitrary", "arbitrary")),
        name="natten",
    )(qkv, qkv, qkv, qkv, qkv, bias)
    o_ctx = pl.pallas_call(
        _ctx_attn_kernel,
        grid=(n_batch,),
        in_specs=[pl.BlockSpec((n_ctx, D), lambda b: (ctx0 + b, 0)),
                  pl.BlockSpec((n_ctx, D), lambda b: (ctx0 + b, 1)),
                  pl.BlockSpec((n_ctx, D), lambda b: (ctx0 + b, 2))],
        out_specs=pl.BlockSpec((n_ctx, D), lambda b: (b, 0)),
        out_shape=jax.ShapeDtypeStruct((n_batch * n_ctx, D), BF16),
        compiler_params=_cparams(("arbitrary",)),
        name="ctx_attn",
    )(qkv, qkv, qkv)
    w = w_o.astype(BF16)
    zero_b = jnp.zeros((D,), F32)
    row_fn = lambda t: _mod_row(t, seq // TM_LIN, n_batch)
    hs = _lin_res_call(hs, (o_lat,), (), w, zero_b, mod_l, 0, n_lat // TM_LIN, row_fn)
    return _lin_res_call(hs, (o_ctx,), (), w, zero_b, mod_l, n_lat // TM_LIN, n_batch * n_ctx // TM_LIN, row_fn)


def kernel(x, c, ctx, c_ctx, w_mod, b_mod, w_ff_in, w_ff_out, fnet_w_out, fnet_b_out, conv_w_in, conv_b_in, conv_w_dw, conv_b_dw, conv_ln_g, conv_ln_b, conv_w_out, conv_b_out, na_w_qkv, na_q_gain, na_k_gain, na_rpb, na_w_o):
    n_batch, seq, d = x.shape
    n_ctx = ctx.shape[1]
    depth = w_mod.shape[0]
    assert d == D and n_batch < MOD_ROWS
    assert seq % TM_FFN == 0 and (n_batch * n_ctx) % TM_FFN == 0
    n_lat = n_batch * seq

    cond = jnp.concatenate([c, c_ctx[None, :], jnp.zeros((MOD_ROWS - n_batch - 1, D), F32)], axis=0)
    mod = _mod_call(cond, w_mod, b_mod).reshape(depth, MOD_ROWS, N_MOD, D)
    n_tok = n_lat + n_batch * n_ctx
    hs = jnp.concatenate([x.reshape(n_lat, D), ctx.reshape(n_batch * n_ctx, D),
                          jnp.zeros((-n_tok % seq, D), F32)], axis=0)
    lat_tiles = n_lat // TM_FFN
    all_tiles = n_tok // TM_FFN
    w_gu, w_o = w_ff_in, w_ff_out

    for i in range(depth):
        kind, slot = i % 3, i // 3
        ctx_post = i < depth - 1
        ctx_pre = ctx_post or kind == 2
        mod_l = mod[i]
        hs = _ffn_call(hs, mod_l, w_gu, w_o, i, 0, 0, all_tiles if ctx_pre else lat_tiles,
                       n_batch, seq, True)
        if kind == 0:
            hs = _fourier_mixer(hs, mod_l, fnet_w_out[slot], fnet_b_out[slot], n_batch, seq, n_ctx, ctx_post)
        elif kind == 1:
            assert ctx_post
            hs = _conv_mixer(hs, mod_l, conv_w_in[slot], conv_b_in[slot], conv_w_dw[slot], conv_b_dw[slot],
                             conv_ln_g[slot], conv_ln_b[slot], conv_w_out[slot], conv_b_out[slot],
                             n_batch, seq, n_ctx)
        else:
            assert ctx_post
            hs = _natten_mixer(hs, mod_l, na_w_qkv[slot], na_q_gain[slot], na_k_gain[slot], na_rpb[slot],
                               na_w_o[slot], n_batch, seq, n_ctx)
        final = i == depth - 1
        hs = _ffn_call(hs, mod_l, w_gu, w_o, i, 1, 6, all_tiles if ctx_post else lat_tiles,
                       n_batch, seq, not final)
    return hs[:n_lat].reshape(n_batch, seq, D)
```

```python
import functools

import numpy as np
import jax
import jax.numpy as jnp
from jax import lax
from jax.experimental import pallas as pl
from jax.experimental.pallas import tpu as pltpu

F32 = jnp.float32
BF16 = jnp.bfloat16

D = 1024
D_FF = 2816
N_MOD = 9
EPS = 1e-6
GRID_W = 64
LANE = 128
SUBLANE = 8
N_GROUPS = 4
GROUP_C = D // N_GROUPS
CONV_K = 31
CONV_HALO = 16
N_HEADS = 16
HEAD_DIM = D // N_HEADS
WIN_R = 8
WIN_C = 16
NEG_INF = -1e30
MOD_ROWS = 8

FF_TILE = 256
TM_FFN = 1024
PROJ_ROWS = 512
GLU_COLS = 512
FFN_ROWS = 256
TM_LIN = 512
TM_CONV = 256
CONV_ROWS = 64
COL_TILE = 16
VMEM_LIMIT = 52 * 1024 * 1024


def _cparams(sem):
    return pltpu.CompilerParams(dimension_semantics=sem, vmem_limit_bytes=VMEM_LIMIT)


def _silu(x):
    return x / (1.0 + jnp.exp(-x))


def _norm_mod(x, shift, scale):
    ms = jnp.mean(x * x, axis=-1, keepdims=True)
    return x * lax.rsqrt(ms + EPS) * (1.0 + scale) + shift


def _dot(a, b):
    return jnp.dot(a, b, preferred_element_type=F32)


def _mod_kernel(c_ref, w_ref, b_ref, o_ref):
    s = _silu(c_ref[...]).astype(BF16)
    o_ref[...] = _dot(s, w_ref[...].astype(BF16)) + b_ref[...]


def _mod_call(cond, w_mod, b_mod):
    depth = w_mod.shape[0]
    return pl.pallas_call(
        _mod_kernel,
        grid=(depth, N_MOD),
        in_specs=[
            pl.BlockSpec((MOD_ROWS, D), lambda l, k: (0, 0)),
            pl.BlockSpec((None, D, D), lambda l, k: (l, 0, k)),
            pl.BlockSpec((None, 1, D), lambda l, k: (l, 0, k)),
        ],
        out_specs=pl.BlockSpec((None, MOD_ROWS, D), lambda l, k: (l, 0, k)),
        out_shape=jax.ShapeDtypeStruct((depth, MOD_ROWS, N_MOD * D), F32),
        compiler_params=_cparams(("arbitrary", "arbitrary")),
        name="mod",
    )(cond, w_mod, b_mod.reshape(depth, 1, N_MOD * D))


def _mod_row(tile, tiles_per_batch, n_batch):
    return jnp.minimum(tile // tiles_per_batch, n_batch)


def _ffn_kernel(*refs, k0, n_ff, lat_tiles):
    if lat_tiles is None:
        h_ref, mod_ref, wg_ref, wu_ref, wo_ref, o_ref, xn_ref, acc_ref = refs
        tokens = lambda rows: h_ref[rows, :]
    else:
        lat_ref, ctx_ref, mod_ref, wg_ref, wu_ref, wo_ref, o_ref, xn_ref, acc_ref = refs
        is_lat = pl.program_id(0) < lat_tiles
        tokens = lambda rows: jnp.where(is_lat, lat_ref[rows, :], ctx_ref[rows, :])
    j = pl.program_id(1)
    n_chunks = TM_FFN // FFN_ROWS
    assert n_ff >= 2

    def step(first, last):
        wg, wu, wo = wg_ref[...].astype(BF16), wu_ref[...].astype(BF16), wo_ref[...].astype(BF16)
        gu = {}
        for c in range(n_chunks + 1):
            if c < n_chunks:
                rows = slice(c * FFN_ROWS, (c + 1) * FFN_ROWS)
                if first:
                    xn = _norm_mod(tokens(rows), mod_ref[k0:k0 + 1, :], mod_ref[k0 + 1:k0 + 2, :]).astype(BF16)
                    xn_ref[rows, :] = xn
                else:
                    xn = xn_ref[rows, :]
                gu[c] = (_dot(xn, wg), _dot(xn, wu))
            if c >= 1:
                rows = slice((c - 1) * FFN_ROWS, c * FFN_ROWS)
                g, u = gu.pop(c - 1)
                y = _dot((_silu(g) * u).astype(BF16), wo)
                if first:
                    acc_ref[rows, :] = y
                elif last:
                    o_ref[rows, :] = tokens(rows) + (0.5 * mod_ref[k0 + 2:k0 + 3, :]) * (acc_ref[rows, :] + y)
                else:
                    acc_ref[rows, :] += y

    pl.when(j == 0)(functools.partial(step, True, False))
    pl.when(jnp.logical_and(j > 0, j < n_ff - 1))(functools.partial(step, False, False))
    pl.when(j == n_ff - 1)(functools.partial(step, False, True))


def _ffn_call(hs, mod_l, w_gu, w_o, layer, which, k0, n_tiles, n_batch, seq, alias, out_rows=None):
    n_ff = D_FF // FF_TILE
    tpb = seq // TM_FFN
    if isinstance(hs, tuple):
        lat, ctx = hs
        lat_tiles = lat.shape[0] // TM_FFN
        assert not alias and ctx.shape[0] == TM_FFN and n_tiles == lat_tiles + 1
        tokens = (lat, ctx)
        token_specs = [pl.BlockSpec((TM_FFN, D), lambda i, j: (jnp.minimum(i, lat_tiles - 1), 0)),
                       pl.BlockSpec((TM_FFN, D), lambda i, j: (0, 0))]
    else:
        lat_tiles = None
        tokens = (hs,)
        token_specs = [pl.BlockSpec((TM_FFN, D), lambda i, j: (i, 0))]
        out_rows = hs.shape[0] if alias else n_tiles * TM_FFN
    return pl.pallas_call(
        functools.partial(_ffn_kernel, k0=k0, n_ff=n_ff, lat_tiles=lat_tiles),
        grid=(n_tiles, n_ff),
        in_specs=token_specs + [
            pl.BlockSpec((None, N_MOD, D), lambda i, j: (_mod_row(i, tpb, n_batch), 0, 0)),
            pl.BlockSpec((None, None, D, FF_TILE), lambda i, j: (layer, which, 0, j)),
            pl.BlockSpec((None, None, D, FF_TILE), lambda i, j: (layer, which, 0, n_ff + j)),
            pl.BlockSpec((None, None, FF_TILE, D), lambda i, j: (layer, which, j, 0)),
        ],
        out_specs=pl.BlockSpec((TM_FFN, D), lambda i, j: (i, 0)),
        out_shape=jax.ShapeDtypeStruct((out_rows, D), F32),
        scratch_shapes=[pltpu.VMEM((TM_FFN, D), BF16), pltpu.VMEM((TM_FFN, D), F32)],
        input_output_aliases={0: 0} if alias else {},
        compiler_params=_cparams(("arbitrary", "arbitrary")),
        name="ffn",
    )(*tokens, mod_l, w_gu, w_gu, w_o)


def _lin_res_kernel(*refs, fourier):
    if fourier:
        xr_ref, xi_ref, cc_ref, sc_ref, w_ref, b_ref, h_ref, mod_ref, o_ref = refs
        parts = []
        for g in range(N_GROUPS):
            sl = slice(g * GROUP_C, (g + 1) * GROUP_C)
            parts.append(_dot(xr_ref[:, sl], cc_ref[...]) + _dot(xi_ref[:, sl], sc_ref[...]))
        x = jnp.concatenate(parts, axis=-1).astype(BF16)
    else:
        x_ref, w_ref, b_ref, h_ref, mod_ref, o_ref = refs
        x = x_ref[...]
    y = _dot(x, w_ref[...]) + b_ref[...]
    o_ref[...] = h_ref[...] + mod_ref[5:6, :] * y


def _lin_res_call(hs, xs, tables, w, b, mod_l, tile0, n_tiles, mod_row_fn):
    fourier = len(xs) == 2
    x_spec = pl.BlockSpec((TM_LIN, D), lambda i: (i, 0))
    t_spec = pl.BlockSpec((GROUP_C, GROUP_C), lambda i: (0, 0))
    in_specs = [x_spec] * len(xs) + [t_spec] * len(tables) + [
        pl.BlockSpec((D, D), lambda i: (0, 0)),
        pl.BlockSpec((1, D), lambda i: (0, 0)),
        pl.BlockSpec((TM_LIN, D), lambda i: (i + tile0, 0)),
        pl.BlockSpec((None, N_MOD, D), lambda i: (mod_row_fn(i + tile0), 0, 0)),
    ]
    h_index = len(xs) + len(tables) + 2
    return pl.pallas_call(
        functools.partial(_lin_res_kernel, fourier=fourier),
        grid=(n_tiles,),
        in_specs=in_specs,
        out_specs=pl.BlockSpec((TM_LIN, D), lambda i: (i + tile0, 0)),
        out_shape=jax.ShapeDtypeStruct(hs.shape, F32),
        input_output_aliases={h_index: 0},
        compiler_params=_cparams(("arbitrary",)),
        name="lin_res_fourier" if fourier else "lin_res",
    )(*xs, *tables, w, b.reshape(1, D), hs, mod_l)


def _fnet_rows_kernel(h_ref, mod_ref, kf_ref, zr_ref, zi_ref):
    rows = h_ref.shape[0]
    xn = _norm_mod(h_ref[...], mod_ref[3:4, :], mod_ref[4:5, :])
    n = rows * SUBLANE
    zr, zi = [], []
    for c0 in range(0, COL_TILE, SUBLANE):
        a = xn[:, c0:c0 + SUBLANE, :].reshape(n, D).astype(BF16)
        z = _dot(kf_ref[...], a)
        zr.append(z[:n].reshape(rows, SUBLANE, D))
        zi.append(z[n:].reshape(rows, SUBLANE, D))
    zr_ref[...] = jnp.concatenate(zr, axis=1).astype(BF16)
    zi_ref[...] = jnp.concatenate(zi, axis=1).astype(BF16)


def _fnet_cols_kernel(zr_ref, zi_ref, m2_ref, cc_ref, sc_ref, w_ref, b_ref, h_ref, mod_ref, o_ref,
                      xr_ref, xi_ref, y_ref):
    n = zr_ref.shape[1]
    for j in range(SUBLANE):
        z = jnp.concatenate([zr_ref[j], zi_ref[j]], axis=0)
        x = _dot(m2_ref[j], z)
        xr_ref[j * n:(j + 1) * n, :] = x[:n].astype(BF16)
        xi_ref[j * n:(j + 1) * n, :] = x[n:].astype(BF16)
    parts = []
    for g in range(N_GROUPS):
        sl = slice(g * GROUP_C, (g + 1) * GROUP_C)
        parts.append(_dot(xr_ref[:, sl], cc_ref[...]) + _dot(xi_ref[:, sl], sc_ref[...]))
    y = _dot(jnp.concatenate(parts, axis=-1).astype(BF16), w_ref[...]) + b_ref[...]
    for j in range(SUBLANE):
        rows = pl.ds(j, n, stride=SUBLANE)
        for lt in range(D // LANE):
            y_ref[lt, rows, :] = y[j * n:(j + 1) * n, lt * LANE:(lt + 1) * LANE]
    y_t = jnp.concatenate([y_ref[lt] for lt in range(D // LANE)], axis=-1).reshape(n, SUBLANE, D)
    o_ref[...] = h_ref[...] + mod_ref[5:6, :] * y_t


def _fnet_ctx_kernel(h_ref, mod_ref, f_ref, xr_ref, xi_ref):
    n = h_ref.shape[0]
    xn = _norm_mod(h_ref[...], mod_ref[3:4, :], mod_ref[4:5, :]).astype(BF16)
    x = _dot(f_ref[...], xn)
    xr_ref[...] = x[:n].astype(BF16)
    xi_ref[...] = x[n:].astype(BF16)


def _dft_tables(rows):
    n_seq = rows * GRID_W
    a = np.arange(rows)
    th1 = 2.0 * np.pi * np.outer(a, a) / rows
    f1 = np.concatenate([np.cos(th1), -np.sin(th1)], axis=0)
    f1 = np.kron(f1, np.eye(SUBLANE))
    k2 = np.arange(GRID_W)[:, None, None]
    k1 = np.arange(rows)[None, :, None]
    n1 = np.arange(GRID_W)[None, None, :]
    th2 = 2.0 * np.pi * (n1 * (GRID_W * k1 + k2) % n_seq) / n_seq
    c2, s2 = np.cos(th2), np.sin(th2)
    m2 = np.concatenate([np.concatenate([c2, s2], axis=2), np.concatenate([-s2, c2], axis=2)], axis=1)
    return f1.astype(np.float32), m2.astype(np.float32)


def _channel_tables(n_seq):
    c = np.arange(GROUP_C)
    th = 2.0 * np.pi * np.outer(c, c) / GROUP_C
    s = 1.0 / np.sqrt(float(n_seq) * GROUP_C)
    return (np.cos(th) * s).astype(np.float32), (np.sin(th) * s).astype(np.float32)


def _fourier_mixer(hs, mod_l, w_out, b_out, n_batch, seq, n_ctx, with_ctx):
    rows = seq // GRID_W
    assert rows == GRID_W, "the two-stage DFT is written for a square token grid"
    n_lat = n_batch * seq
    f1, m2 = _dft_tables(rows)
    cc, sc = _channel_tables(seq)
    f1, m2, cc, sc = (jnp.asarray(t).astype(BF16) for t in (f1, m2, cc, sc))
    w = w_out.astype(BF16)
    n_ct = GRID_W // COL_TILE
    grid3 = hs.reshape(hs.shape[0] // GRID_W, GRID_W, D)
    z_shape = jax.ShapeDtypeStruct((n_batch * rows, GRID_W, D), BF16)
    blk_cols = pl.BlockSpec((rows, COL_TILE, D), lambda b, c: (b, c, 0))
    mod_b = pl.BlockSpec((None, N_MOD, D), lambda b, c: (b, 0, 0))
    zr, zi = pl.pallas_call(
        _fnet_rows_kernel,
        grid=(n_batch, n_ct),
        in_specs=[blk_cols, mod_b, pl.BlockSpec(f1.shape, lambda b, c: (0, 0))],
        out_specs=[blk_cols, blk_cols],
        out_shape=[z_shape, z_shape],
        compiler_params=_cparams(("arbitrary", "arbitrary")),
        name="fnet_rows",
    )(grid3, mod_l, f1)
    n_kt = GRID_W // SUBLANE
    blk_k2 = pl.BlockSpec((SUBLANE, GRID_W, D), lambda b, k: (b * n_kt + k, 0, 0))
    blk_tok = pl.BlockSpec((rows, SUBLANE, D), lambda b, k: (b, k, 0))
    table = pl.BlockSpec((GROUP_C, GROUP_C), lambda b, k: (0, 0))
    n_tok = rows * SUBLANE
    grid3 = pl.pallas_call(
        _fnet_cols_kernel,
        grid=(n_batch, n_kt),
        in_specs=[blk_k2, blk_k2,
                  pl.BlockSpec((SUBLANE, 2 * rows, 2 * GRID_W), lambda b, k: (k, 0, 0)),
                  table, table,
                  pl.BlockSpec((D, D), lambda b, k: (0, 0)),
                  pl.BlockSpec((1, D), lambda b, k: (0, 0)),
                  blk_tok, mod_b],
        out_specs=blk_tok,
        out_shape=jax.ShapeDtypeStruct(grid3.shape, F32),
        scratch_shapes=[pltpu.VMEM((n_tok, D), BF16), pltpu.VMEM((n_tok, D), BF16),
                        pltpu.VMEM((D // LANE, n_tok, LANE), F32)],
        input_output_aliases={7: 0},
        compiler_params=_cparams(("arbitrary", "arbitrary")),
        name="fnet_cols",
    )(zr, zi, m2, cc, sc, w, b_out.reshape(1, D), grid3, mod_l)
    hs = grid3.reshape(hs.shape)
    tpb = seq // TM_LIN
    row_fn = lambda t: _mod_row(t, tpb, n_batch)
    if with_ctx:
        a = np.arange(n_ctx)
        th = 2.0 * np.pi * np.outer(a, a) / n_ctx
        s = np.sqrt(float(seq) / n_ctx)
        fc = jnp.asarray(np.concatenate([np.cos(th) * s, -np.sin(th) * s], axis=0).astype(np.float32)).astype(BF16)
        c_shape = jax.ShapeDtypeStruct((n_batch * n_ctx, D), BF16)
        tile0 = n_lat // n_ctx
        xr, xi = pl.pallas_call(
            _fnet_ctx_kernel,
            grid=(n_batch,),
            in_specs=[pl.BlockSpec((n_ctx, D), lambda b: (tile0 + b, 0)),
                      pl.BlockSpec((None, N_MOD, D), lambda b: (n_batch, 0, 0)),
                      pl.BlockSpec((2 * n_ctx, n_ctx), lambda b: (0, 0))],
            out_specs=[pl.BlockSpec((n_ctx, D), lambda b: (b, 0))] * 2,
            out_shape=[c_shape, c_shape],
            compiler_params=_cparams(("arbitrary",)),
            name="fnet_ctx",
        )(hs, mod_l, fc)
        hs = _lin_res_call(hs, (xr, xi), (cc, sc), w, b_out, mod_l,
                           n_lat // TM_LIN, n_batch * n_ctx // TM_LIN, row_fn)
    return hs


def _mixer_input_chunks(h_ref, mod_ref, xn_ref, first):
    for r0 in range(0, h_ref.shape[0], PROJ_ROWS):
        rows = slice(r0, r0 + PROJ_ROWS)
        if first:
            xn = _norm_mod(h_ref[rows, :], mod_ref[3:4, :], mod_ref[4:5, :]).astype(BF16)
            xn_ref[rows, :] = xn
        else:
            xn = xn_ref[rows, :]
        yield rows, xn


def _glu_kernel(h_ref, mod_ref, wa_ref, wg_ref, ba_ref, bg_ref, o_ref, xn_ref):
    def step(first):
        for rows, xn in _mixer_input_chunks(h_ref, mod_ref, xn_ref, first):
            a = _dot(xn, wa_ref[...]) + ba_ref[...]
            g = _dot(xn, wg_ref[...]) + bg_ref[...]
            o_ref[rows, :] = a / (1.0 + jnp.exp(-g))

    j = pl.program_id(1)
    pl.when(j == 0)(functools.partial(step, True))
    pl.when(j > 0)(functools.partial(step, False))


def _conv_kernel(u_ref, up_ref, un_ref, wdw_ref, bdw_ref, lng_ref, lnb_ref, w_ref, b_ref, h_ref, mod_ref,
                 o_ref, buf_ref, conv_ref, v_ref, *, lat_tiles, tiles_per_seq):
    i = pl.program_id(0)
    is_ctx = i >= lat_tiles
    first = jnp.logical_or(is_ctx, i % tiles_per_seq == 0)
    last = jnp.logical_or(is_ctx, i % tiles_per_seq == tiles_per_seq - 1)
    tm = u_ref.shape[0]
    n_buf = tm + 2 * CONV_HALO
    buf_ref[0, 0:CONV_HALO, :] = jnp.where(first, 0.0, up_ref[...])
    buf_ref[0, CONV_HALO:CONV_HALO + tm, :] = u_ref[...]
    buf_ref[0, CONV_HALO + tm:, :] = jnp.where(last, 0.0, un_ref[...])
    for s in range(1, SUBLANE):
        buf_ref[s] = pltpu.roll(buf_ref[0], n_buf - s, axis=0)
    off = CONV_HALO - CONV_K // 2
    for lt in range(D // LANE):
        lanes = slice(lt * LANE, (lt + 1) * LANE)
        for c0 in range(0, tm, CONV_ROWS):
            acc = jnp.zeros((CONV_ROWS, LANE), F32) + bdw_ref[:, lanes]
            for k in range(CONV_K):
                row = c0 + (off + k) // SUBLANE * SUBLANE
                acc = acc + wdw_ref[k:k + 1, lanes] * buf_ref[(off + k) % SUBLANE, row:row + CONV_ROWS, lanes]
            conv_ref[c0:c0 + CONV_ROWS, lanes] = acc
    chunk = 32
    for c0 in range(0, tm, chunk):
        acc = conv_ref[c0:c0 + chunk, :]
        mu = jnp.mean(acc, axis=-1, keepdims=True)
        d = acc - mu
        var = jnp.mean(d * d, axis=-1, keepdims=True)
        v = d * lax.rsqrt(var + EPS) * lng_ref[...] + lnb_ref[...]
        v_ref[c0:c0 + chunk, :] = _silu(v).astype(BF16)
    y = _dot(v_ref[...], w_ref[...]) + b_ref[...]
    o_ref[...] = h_ref[...] + mod_ref[5:6, :] * y


def _conv_mixer(hs, mod_l, w_in, b_in, w_dw, b_dw, ln_g, ln_b, w_out, b_out, n_batch, seq, n_ctx):
    n_lat = n_batch * seq
    n_all = n_lat + n_batch * n_ctx
    assert n_ctx == TM_CONV, "context sequences are one conv tile long"
    n_tiles = n_all // TM_FFN
    n_col = D // GLU_COLS
    tpb = seq // TM_FFN
    col = pl.BlockSpec((D, GLU_COLS), lambda i, j: (0, j))
    gcol = pl.BlockSpec((D, GLU_COLS), lambda i, j: (0, n_col + j))
    u = pl.pallas_call(
        _glu_kernel,
        grid=(n_tiles, n_col),
        in_specs=[pl.BlockSpec((TM_FFN, D), lambda i, j: (i, 0)),
                  pl.BlockSpec((None, N_MOD, D), lambda i, j: (_mod_row(i, tpb, n_batch), 0, 0)),
                  col, gcol,
                  pl.BlockSpec((1, GLU_COLS), lambda i, j: (0, j)),
                  pl.BlockSpec((1, GLU_COLS), lambda i, j: (0, n_col + j))],
        out_specs=pl.BlockSpec((TM_FFN, GLU_COLS), lambda i, j: (i, j)),
        out_shape=jax.ShapeDtypeStruct((n_all, D), F32),
        scratch_shapes=[pltpu.VMEM((TM_FFN, D), BF16)],
        compiler_params=_cparams(("arbitrary", "arbitrary")),
        name="conv_glu",
    )(hs, mod_l, w_in.astype(BF16), w_in.astype(BF16), b_in.reshape(1, 2 * D), b_in.reshape(1, 2 * D))

    n_ct = n_all // TM_CONV
    hpt = TM_CONV // CONV_HALO
    n_halo = n_all // CONV_HALO
    tps = seq // TM_CONV
    vec = pl.BlockSpec((1, D), lambda i: (0, 0))
    tile = pl.BlockSpec((TM_CONV, D), lambda i: (i, 0))
    return pl.pallas_call(
        functools.partial(_conv_kernel, lat_tiles=n_lat // TM_CONV, tiles_per_seq=tps),
        grid=(n_ct,),
        in_specs=[tile,
                  pl.BlockSpec((CONV_HALO, D), lambda i: (jnp.maximum(i * hpt - 1, 0), 0)),
                  pl.BlockSpec((CONV_HALO, D), lambda i: (jnp.minimum((i + 1) * hpt, n_halo - 1), 0)),
                  pl.BlockSpec((CONV_K, D), lambda i: (0, 0)),
                  vec, vec, vec,
                  pl.BlockSpec((D, D), lambda i: (0, 0)),
                  vec, tile,
                  pl.BlockSpec((None, N_MOD, D), lambda i: (_mod_row(i, tps, n_batch), 0, 0))],
        out_specs=tile,
        out_shape=jax.ShapeDtypeStruct(hs.shape, F32),
        scratch_shapes=[pltpu.VMEM((SUBLANE, TM_CONV + 2 * CONV_HALO, D), F32), pltpu.VMEM((TM_CONV, D), F32),
                        pltpu.VMEM((TM_CONV, D), BF16)],
        input_output_aliases={9: 0},
        compiler_params=_cparams(("arbitrary",)),
        name="conv_dw",
    )(u, u, u, w_dw, b_dw.reshape(1, D), ln_g.reshape(1, D), ln_b.reshape(1, D), w_out.astype(BF16),
      b_out.reshape(1, D), hs, mod_l)


def _qkv_kernel(h_ref, mod_ref, w_ref, gain_ref, e_ref, o_ref, xn_ref):
    n_e = e_ref.shape[0]

    def step(first, head_norm):
        for rows, xn in _mixer_input_chunks(h_ref, mod_ref, xn_ref, first):
            y = _dot(xn, w_ref[...])
            if not head_norm:
                o_ref[rows, :] = y.astype(BF16)
                continue
            for c0 in range(0, D, n_e):
                cols = slice(c0, c0 + n_e)
                yc = y[:, cols]
                ss = _dot((yc * yc).astype(BF16), e_ref[...])
                rs = lax.rsqrt(ss * (1.0 / HEAD_DIM) + EPS) * gain_ref[:, cols]
                o_ref[rows, cols] = (yc * rs).astype(BF16)

    j = pl.program_id(1)
    pl.when(j == 0)(functools.partial(step, True, True))
    pl.when(j == 1)(functools.partial(step, False, True))
    pl.when(j == 2)(functools.partial(step, False, False))


def _softmax_pv(s_list, v_list):
    m = s_list[0].max(axis=-1, keepdims=True)
    for s in s_list[1:]:
        m = jnp.maximum(m, s.max(axis=-1, keepdims=True))
    den = 0.0
    acc = 0.0
    for s, v in zip(s_list, v_list):
        p = jnp.exp(s - m)
        den = den + p.sum(axis=-1, keepdims=True)
        acc = acc + _dot(p.astype(BF16), v)
    return acc / den


def _qk(q, k):
    return lax.dot_general(q, k, (((1,), (1,)), ((), ())), preferred_element_type=F32)


def _natten_kernel(q_ref, k_ref, v_ref, kc_ref, vc_ref, bias_ref, o_ref, *, rows, kr):
    r = pl.program_id(1)
    rs = jnp.clip(r - kr // 2, 0, rows - kr)
    dr0 = rs - r + WIN_R - 1
    win = pl.ds(pl.multiple_of(rs * GRID_W, GRID_W), kr * GRID_W)
    left = lax.broadcasted_iota(jnp.int32, (GRID_W, LANE), 1) < HEAD_DIM
    for p in range(N_HEADS // 2):
        cols = slice(p * LANE, (p + 1) * LANE)
        q = q_ref[:, cols]
        zero = jnp.zeros_like(q)
        q2 = jnp.concatenate([jnp.where(left, q, zero), jnp.where(left, zero, q)], axis=0)
        bias = jnp.concatenate([bias_ref[p, dr0 + 2 * t] for t in range(kr // 2)], axis=-1)
        s_lat = _qk(q2, k_ref[win, cols]) + bias
        s_ctx = _qk(q2, kc_ref[:, cols])
        o = _softmax_pv([s_lat, s_ctx], [v_ref[win, cols], vc_ref[:, cols]])
        o_ref[:, cols] = jnp.where(left, o[:GRID_W], o[GRID_W:]).astype(BF16)


def _bias_kernel(rpb_ref, o_ref):
    qc = lax.broadcasted_iota(jnp.int32, (GRID_W, LANE), 0)
    lane = lax.broadcasted_iota(jnp.int32, (GRID_W, LANE), 1)
    kc = lane % GRID_W
    w0 = jnp.clip(qc - WIN_C // 2, 0, GRID_W - WIN_C)
    ok = jnp.logical_and(kc >= w0, kc < w0 + WIN_C)
    tiles = []
    for dr in range(2 * WIN_R - 1):
        t = jnp.broadcast_to(rpb_ref[dr:dr + 1, :], (GRID_W, LANE))
        t = pltpu.roll(t, LANE - (WIN_C - 1), axis=1)
        for bit in range(GRID_W.bit_length() - 1):
            t = jnp.where((qc >> bit) & 1 == 1, pltpu.roll(t, 1 << bit, axis=1), t)
        tiles.append(t)
    for p in range(2 * WIN_R - 2):
        pair = jnp.where(lane < GRID_W, tiles[p], pltpu.roll(tiles[p + 1], GRID_W, axis=1))
        o_ref[p] = jnp.where(ok, pair, NEG_INF)


def _ctx_attn_kernel(q_ref, k_ref, v_ref, o_ref):
    outs = []
    for h in range(N_HEADS):
        hd = slice(h * HEAD_DIM, (h + 1) * HEAD_DIM)
        outs.append(_softmax_pv([_qk(q_ref[:, hd], k_ref[:, hd])], [v_ref[:, hd]]))
    o_ref[...] = jnp.concatenate(outs, axis=-1).astype(BF16)


def _natten_bias(rpb):
    n_dr, n_dc = 2 * WIN_R - 1, 2 * WIN_C - 1
    assert rpb.shape == (N_HEADS, n_dr, n_dc) and GRID_W * 2 == LANE
    padded = jnp.pad(rpb.astype(F32), ((0, 0), (0, 0), (0, LANE - n_dc)))
    return pl.pallas_call(
        _bias_kernel,
        grid=(N_HEADS,),
        in_specs=[pl.BlockSpec((None, n_dr, LANE), lambda h: (h, 0, 0))],
        out_specs=pl.BlockSpec((None, n_dr - 1, GRID_W, LANE), lambda h: (h // 2, 0, h % 2, 0)),
        out_shape=jax.ShapeDtypeStruct((N_HEADS // 2, n_dr - 1, 2 * GRID_W, LANE), F32),
        compiler_params=_cparams(("arbitrary",)),
        name="natten_bias",
    )(padded)


def _natten_mixer(hs, mod_l, w_qkv, q_gain, k_gain, rpb, w_o, n_batch, seq, n_ctx):
    n_lat = n_batch * seq
    n_all = n_lat + n_batch * n_ctx
    rows = seq // GRID_W
    kr = min(WIN_R, rows)
    n_tiles = n_all // TM_FFN
    tpb = seq // TM_FFN
    scale = HEAD_DIM ** -0.5
    gain = jnp.concatenate([jnp.tile(q_gain, N_HEADS) * scale, jnp.tile(k_gain, N_HEADS),
                            jnp.ones((D,), F32)]).reshape(1, 3 * D)
    lane = np.arange(FF_TILE)
    e = jnp.asarray((lane[:, None] // HEAD_DIM == lane[None, :] // HEAD_DIM).astype(np.float32)).astype(BF16)
    qkv = pl.pallas_call(
        _qkv_kernel,
        grid=(n_tiles, 3),
        in_specs=[pl.BlockSpec((TM_FFN, D), lambda i, j: (i, 0)),
                  pl.BlockSpec((None, N_MOD, D), lambda i, j: (_mod_row(i, tpb, n_batch), 0, 0)),
                  pl.BlockSpec((D, D), lambda i, j: (0, j)),
                  pl.BlockSpec((1, D), lambda i, j: (0, j)),
                  pl.BlockSpec((FF_TILE, FF_TILE), lambda i, j: (0, 0))],
        out_specs=pl.BlockSpec((TM_FFN, D), lambda i, j: (i, j)),
        out_shape=jax.ShapeDtypeStruct((n_all, 3 * D), BF16),
        scratch_shapes=[pltpu.VMEM((TM_FFN, D), BF16)],
        compiler_params=_cparams(("arbitrary", "arbitrary")),
        name="qkv",
    )(hs, mod_l, w_qkv.astype(BF16), gain, e)

    assert kr == WIN_R, "bias tiles pair up the key rows of a full window"
    bias = _natten_bias(rpb)
    ctx0 = n_lat // n_ctx
    once = pl.Buffered(1)
    o_lat = pl.pallas_call(
        functools.partial(_natten_kernel, rows=rows, kr=kr),
        grid=(n_batch, rows),
        in_specs=[pl.BlockSpec((GRID_W, D), lambda b, r: (b * rows + r, 0)),
                  pl.BlockSpec((seq, D), lambda b, r: (b, 1)),
                  pl.BlockSpec((seq, D), lambda b, r: (b, 2)),
                  pl.BlockSpec((n_ctx, D), lambda b, r: (ctx0 + b, 1)),
                  pl.BlockSpec((n_ctx, D), lambda b, r: (ctx0 + b, 2)),
                  pl.BlockSpec(bias.shape, lambda b, r: (0, 0, 0, 0), pipeline_mode=once)],
        out_specs=pl.BlockSpec((GRID_W, D), lambda b, r: (b * rows + r, 0)),
        out_shape=jax.ShapeDtypeStruct((n_lat, D), BF16),
        compiler_params=_cparams(("arbitrary", "arbitrary")),
        name="natten",
    )(qkv, qkv, qkv, qkv, qkv, bias)
    o_ctx = pl.pallas_call(
        _ctx_attn_kernel,
        grid=(n_batch,),
        in_specs=[pl.BlockSpec((n_ctx, D), lambda b: (ctx0 + b, 0)),
                  pl.BlockSpec((n_ctx, D), lambda b: (ctx0 + b, 1)),
                  pl.BlockSpec((n_ctx, D), lambda b: (ctx0 + b, 2))],
        out_specs=pl.BlockSpec((n_ctx, D), lambda b: (b, 0)),
        out_shape=jax.ShapeDtypeStruct((n_batch * n_ctx, D), BF16),
        compiler_params=_cparams(("arbitrary",)),
        name="ctx_attn",
    )(qkv, qkv, qkv)
    w = w_o.astype(BF16)
    zero_b = jnp.zeros((D,), F32)
    row_fn = lambda t: _mod_row(t, seq // TM_LIN, n_batch)
    hs = _lin_res_call(hs, (o_lat,), (), w, zero_b, mod_l, 0, n_lat // TM_LIN, row_fn)
    return _lin_res_call(hs, (o_ctx,), (), w, zero_b, mod_l, n_lat // TM_LIN, n_batch * n_ctx // TM_LIN, row_fn)


def kernel(x, c, ctx, c_ctx, w_mod, b_mod, w_ff_in, w_ff_out, fnet_w_out, fnet_b_out, conv_w_in, conv_b_in, conv_w_dw, conv_b_dw, conv_ln_g, conv_ln_b, conv_w_out, conv_b_out, na_w_qkv, na_q_gain, na_k_gain, na_rpb, na_w_o):
    n_batch, seq, d = x.shape
    n_ctx = ctx.shape[1]
    depth = w_mod.shape[0]
    assert d == D and n_batch < MOD_ROWS
    assert seq % TM_FFN == 0 and (n_batch * n_ctx) % TM_FFN == 0
    n_lat = n_batch * seq

    cond = jnp.concatenate([c, c_ctx[None, :], jnp.zeros((MOD_ROWS - n_batch - 1, D), F32)], axis=0)
    mod = _mod_call(cond, w_mod, b_mod).reshape(depth, MOD_ROWS, N_MOD, D)
    n_tok = n_lat + n_batch * n_ctx
    n_stream = n_tok + (-n_tok % seq)
    assert depth > 1, "the first half-step is written for a layer that updates the context tokens"
    hs = (x.reshape(n_lat, D), ctx.reshape(n_batch * n_ctx, D))
    lat_tiles = n_lat // TM_FFN
    all_tiles = n_tok // TM_FFN
    w_gu, w_o = w_ff_in, w_ff_out

    for i in range(depth):
        kind, slot = i % 3, i // 3
        ctx_post = i < depth - 1
        ctx_pre = ctx_post or kind == 2
        mod_l = mod[i]
        hs = _ffn_call(hs, mod_l, w_gu, w_o, i, 0, 0, all_tiles if ctx_pre else lat_tiles,
                       n_batch, seq, i > 0, n_stream)
        if kind == 0:
            hs = _fourier_mixer(hs, mod_l, fnet_w_out[slot], fnet_b_out[slot], n_batch, seq, n_ctx, ctx_post)
        elif kind == 1:
            assert ctx_post
            hs = _conv_mixer(hs, mod_l, conv_w_in[slot], conv_b_in[slot], conv_w_dw[slot], conv_b_dw[slot],
                             conv_ln_g[slot], conv_ln_b[slot], conv_w_out[slot], conv_b_out[slot],
                             n_batch, seq, n_ctx)
        else:
            assert ctx_post
            hs = _natten_mixer(hs, mod_l, na_w_qkv[slot], na_q_gain[slot], na_k_gain[slot], na_rpb[slot],
                               na_w_o[slot], n_batch, seq, n_ctx)
        final = i == depth - 1
        hs = _ffn_call(hs, mod_l, w_gu, w_o, i, 1, 6, all_tiles if ctx_post else lat_tiles,
                       n_batch, seq, not final)
    return hs[:n_lat].reshape(n_batch, seq, D)
```

```python
import functools

import numpy as np
import jax
import jax.numpy as jnp
from jax import lax
from jax.experimental import pallas as pl
from jax.experimental.pallas import tpu as pltpu

F32 = jnp.float32
BF16 = jnp.bfloat16

D = 1024
D_FF = 2816
N_MOD = 9
EPS = 1e-6
GRID_W = 64
LANE = 128
SUBLANE = 8
N_GROUPS = 4
GROUP_C = D // N_GROUPS
CONV_K = 31
CONV_HALO = 16
N_HEADS = 16
HEAD_DIM = D // N_HEADS
WIN_R = 8
WIN_C = 16
NEG_INF = -1e30
MOD_ROWS = 8

FF_TILE = 256
TM_FFN = 1024
PROJ_ROWS = 512
GLU_COLS = 512
FFN_ROWS = 256
TM_LIN = 512
TM_CONV = 256
CONV_ROWS = 64
COL_TILE = 16
NATTEN_ROWS = 4
VMEM_LIMIT = 52 * 1024 * 1024


def _cparams(sem):
    return pltpu.CompilerParams(dimension_semantics=sem, vmem_limit_bytes=VMEM_LIMIT)


def _silu(x):
    return x / (1.0 + jnp.exp(-x))


def _norm_mod(x, shift, scale):
    ms = jnp.mean(x * x, axis=-1, keepdims=True)
    return x * lax.rsqrt(ms + EPS) * (1.0 + scale) + shift


def _dot(a, b):
    return jnp.dot(a, b, preferred_element_type=F32)


def _mod_kernel(c_ref, w_ref, b_ref, o_ref):
    s = _silu(c_ref[...]).astype(BF16)
    o_ref[...] = _dot(s, w_ref[...].astype(BF16)) + b_ref[...]


def _mod_call(cond, w_mod, b_mod):
    depth = w_mod.shape[0]
    return pl.pallas_call(
        _mod_kernel,
        grid=(depth, N_MOD),
        in_specs=[
            pl.BlockSpec((MOD_ROWS, D), lambda l, k: (0, 0)),
            pl.BlockSpec((None, D, D), lambda l, k: (l, 0, k)),
            pl.BlockSpec((None, 1, D), lambda l, k: (l, 0, k)),
        ],
        out_specs=pl.BlockSpec((None, MOD_ROWS, D), lambda l, k: (l, 0, k)),
        out_shape=jax.ShapeDtypeStruct((depth, MOD_ROWS, N_MOD * D), F32),
        compiler_params=_cparams(("arbitrary", "arbitrary")),
        name="mod",
    )(cond, w_mod, b_mod.reshape(depth, 1, N_MOD * D))


def _mod_row(tile, tiles_per_batch, n_batch):
    return jnp.minimum(tile // tiles_per_batch, n_batch)


def _ffn_kernel(*refs, k0, n_ff, lat_tiles):
    if lat_tiles is None:
        h_ref, mod_ref, wg_ref, wu_ref, wo_ref, o_ref, xn_ref, acc_ref = refs
        tokens = lambda rows: h_ref[rows, :]
    else:
        lat_ref, ctx_ref, mod_ref, wg_ref, wu_ref, wo_ref, o_ref, xn_ref, acc_ref = refs
        is_lat = pl.program_id(0) < lat_tiles
        tokens = lambda rows: jnp.where(is_lat, lat_ref[rows, :], ctx_ref[rows, :])
    j = pl.program_id(1)
    n_chunks = TM_FFN // FFN_ROWS
    assert n_ff >= 2

    def step(first, last):
        wg, wu, wo = wg_ref[...].astype(BF16), wu_ref[...].astype(BF16), wo_ref[...].astype(BF16)
        gu = {}
        for c in range(n_chunks + 1):
            if c < n_chunks:
                rows = slice(c * FFN_ROWS, (c + 1) * FFN_ROWS)
                if first:
                    xn = _norm_mod(tokens(rows), mod_ref[k0:k0 + 1, :], mod_ref[k0 + 1:k0 + 2, :]).astype(BF16)
                    xn_ref[rows, :] = xn
                else:
                    xn = xn_ref[rows, :]
                gu[c] = (_dot(xn, wg), _dot(xn, wu))
            if c >= 1:
                rows = slice((c - 1) * FFN_ROWS, c * FFN_ROWS)
                g, u = gu.pop(c - 1)
                y = _dot((_silu(g) * u).astype(BF16), wo)
                if first:
                    acc_ref[rows, :] = y
                elif last:
                    o_ref[rows, :] = tokens(rows) + (0.5 * mod_ref[k0 + 2:k0 + 3, :]) * (acc_ref[rows, :] + y)
                else:
                    acc_ref[rows, :] += y

    pl.when(j == 0)(functools.partial(step, True, False))
    pl.when(jnp.logical_and(j > 0, j < n_ff - 1))(functools.partial(step, False, False))
    pl.when(j == n_ff - 1)(functools.partial(step, False, True))


def _ffn_call(hs, mod_l, w_gu, w_o, layer, which, k0, n_tiles, n_batch, seq, alias, out_rows=None):
    n_ff = D_FF // FF_TILE
    tpb = seq // TM_FFN
    if isinstance(hs, tuple):
        lat, ctx = hs
        lat_tiles = lat.shape[0] // TM_FFN
        assert not alias and ctx.shape[0] == TM_FFN and n_tiles == lat_tiles + 1
        tokens = (lat, ctx)
        token_specs = [pl.BlockSpec((TM_FFN, D), lambda i, j: (jnp.minimum(i, lat_tiles - 1), 0)),
                       pl.BlockSpec((TM_FFN, D), lambda i, j: (0, 0))]
    else:
        lat_tiles = None
        tokens = (hs,)
        token_specs = [pl.BlockSpec((TM_FFN, D), lambda i, j: (i, 0))]
        out_rows = hs.shape[0] if alias else n_tiles * TM_FFN
    return pl.pallas_call(
        functools.partial(_ffn_kernel, k0=k0, n_ff=n_ff, lat_tiles=lat_tiles),
        grid=(n_tiles, n_ff),
        in_specs=token_specs + [
            pl.BlockSpec((None, N_MOD, D), lambda i, j: (_mod_row(i, tpb, n_batch), 0, 0)),
            pl.BlockSpec((None, None, D, FF_TILE), lambda i, j: (layer, which, 0, j)),
            pl.BlockSpec((None, None, D, FF_TILE), lambda i, j: (layer, which, 0, n_ff + j)),
            pl.BlockSpec((None, None, FF_TILE, D), lambda i, j: (layer, which, j, 0)),
        ],
        out_specs=pl.BlockSpec((TM_FFN, D), lambda i, j: (i, 0)),
        out_shape=jax.ShapeDtypeStruct((out_rows, D), F32),
        scratch_shapes=[pltpu.VMEM((TM_FFN, D), BF16), pltpu.VMEM((TM_FFN, D), F32)],
        input_output_aliases={0: 0} if alias else {},
        compiler_params=_cparams(("arbitrary", "arbitrary")),
        name="ffn",
    )(*tokens, mod_l, w_gu, w_gu, w_o)


def _lin_res_kernel(*refs, fourier):
    if fourier:
        xr_ref, xi_ref, cc_ref, sc_ref, w_ref, b_ref, h_ref, mod_ref, o_ref = refs
        parts = []
        for g in range(N_GROUPS):
            sl = slice(g * GROUP_C, (g + 1) * GROUP_C)
            parts.append(_dot(xr_ref[:, sl], cc_ref[...]) + _dot(xi_ref[:, sl], sc_ref[...]))
        x = jnp.concatenate(parts, axis=-1).astype(BF16)
    else:
        x_ref, w_ref, b_ref, h_ref, mod_ref, o_ref = refs
        x = x_ref[...]
    y = _dot(x, w_ref[...]) + b_ref[...]
    o_ref[...] = h_ref[...] + mod_ref[5:6, :] * y


def _lin_res_call(hs, xs, tables, w, b, mod_l, tile0, n_tiles, mod_row_fn):
    fourier = len(xs) == 2
    x_spec = pl.BlockSpec((TM_LIN, D), lambda i: (i, 0))
    t_spec = pl.BlockSpec((GROUP_C, GROUP_C), lambda i: (0, 0))
    in_specs = [x_spec] * len(xs) + [t_spec] * len(tables) + [
        pl.BlockSpec((D, D), lambda i: (0, 0)),
        pl.BlockSpec((1, D), lambda i: (0, 0)),
        pl.BlockSpec((TM_LIN, D), lambda i: (i + tile0, 0)),
        pl.BlockSpec((None, N_MOD, D), lambda i: (mod_row_fn(i + tile0), 0, 0)),
    ]
    h_index = len(xs) + len(tables) + 2
    return pl.pallas_call(
        functools.partial(_lin_res_kernel, fourier=fourier),
        grid=(n_tiles,),
        in_specs=in_specs,
        out_specs=pl.BlockSpec((TM_LIN, D), lambda i: (i + tile0, 0)),
        out_shape=jax.ShapeDtypeStruct(hs.shape, F32),
        input_output_aliases={h_index: 0},
        compiler_params=_cparams(("arbitrary",)),
        name="lin_res_fourier" if fourier else "lin_res",
    )(*xs, *tables, w, b.reshape(1, D), hs, mod_l)


def _fnet_rows_kernel(h_ref, mod_ref, kf_ref, zr_ref, zi_ref):
    rows = h_ref.shape[0]
    xn = _norm_mod(h_ref[...], mod_ref[3:4, :], mod_ref[4:5, :])
    n = rows * SUBLANE
    zr, zi = [], []
    for c0 in range(0, COL_TILE, SUBLANE):
        a = xn[:, c0:c0 + SUBLANE, :].reshape(n, D).astype(BF16)
        z = _dot(kf_ref[...], a)
        zr.append(z[:n].reshape(rows, SUBLANE, D))
        zi.append(z[n:].reshape(rows, SUBLANE, D))
    zr_ref[...] = jnp.concatenate(zr, axis=1).astype(BF16)
    zi_ref[...] = jnp.concatenate(zi, axis=1).astype(BF16)


def _fnet_cols_kernel(zr_ref, zi_ref, m2_ref, cc_ref, sc_ref, w_ref, b_ref, h_ref, mod_ref, o_ref,
                      xr_ref, xi_ref, y_ref):
    n = zr_ref.shape[1]
    for j in range(SUBLANE):
        z = jnp.concatenate([zr_ref[j], zi_ref[j]], axis=0)
        x = _dot(m2_ref[j], z)
        xr_ref[j * n:(j + 1) * n, :] = x[:n].astype(BF16)
        xi_ref[j * n:(j + 1) * n, :] = x[n:].astype(BF16)
    parts = []
    for g in range(N_GROUPS):
        sl = slice(g * GROUP_C, (g + 1) * GROUP_C)
        parts.append(_dot(xr_ref[:, sl], cc_ref[...]) + _dot(xi_ref[:, sl], sc_ref[...]))
    y = _dot(jnp.concatenate(parts, axis=-1).astype(BF16), w_ref[...]) + b_ref[...]
    for j in range(SUBLANE):
        rows = pl.ds(j, n, stride=SUBLANE)
        for lt in range(D // LANE):
            y_ref[lt, rows, :] = y[j * n:(j + 1) * n, lt * LANE:(lt + 1) * LANE]
    y_t = jnp.concatenate([y_ref[lt] for lt in range(D // LANE)], axis=-1).reshape(n, SUBLANE, D)
    o_ref[...] = h_ref[...] + mod_ref[5:6, :] * y_t


def _fnet_ctx_kernel(h_ref, mod_ref, f_ref, xr_ref, xi_ref):
    n = h_ref.shape[0]
    xn = _norm_mod(h_ref[...], mod_ref[3:4, :], mod_ref[4:5, :]).astype(BF16)
    x = _dot(f_ref[...], xn)
    xr_ref[...] = x[:n].astype(BF16)
    xi_ref[...] = x[n:].astype(BF16)


def _dft_tables(rows):
    n_seq = rows * GRID_W
    a = np.arange(rows)
    th1 = 2.0 * np.pi * np.outer(a, a) / rows
    f1 = np.concatenate([np.cos(th1), -np.sin(th1)], axis=0)
    f1 = np.kron(f1, np.eye(SUBLANE))
    k2 = np.arange(GRID_W)[:, None, None]
    k1 = np.arange(rows)[None, :, None]
    n1 = np.arange(GRID_W)[None, None, :]
    th2 = 2.0 * np.pi * (n1 * (GRID_W * k1 + k2) % n_seq) / n_seq
    c2, s2 = np.cos(th2), np.sin(th2)
    m2 = np.concatenate([np.concatenate([c2, s2], axis=2), np.concatenate([-s2, c2], axis=2)], axis=1)
    return f1.astype(np.float32), m2.astype(np.float32)


def _channel_tables(n_seq):
    c = np.arange(GROUP_C)
    th = 2.0 * np.pi * np.outer(c, c) / GROUP_C
    s = 1.0 / np.sqrt(float(n_seq) * GROUP_C)
    return (np.cos(th) * s).astype(np.float32), (np.sin(th) * s).astype(np.float32)


def _fourier_mixer(hs, mod_l, w_out, b_out, n_batch, seq, n_ctx, with_ctx):
    rows = seq // GRID_W
    assert rows == GRID_W, "the two-stage DFT is written for a square token grid"
    n_lat = n_batch * seq
    f1, m2 = _dft_tables(rows)
    cc, sc = _channel_tables(seq)
    f1, m2, cc, sc = (jnp.asarray(t).astype(BF16) for t in (f1, m2, cc, sc))
    w = w_out.astype(BF16)
    n_ct = GRID_W // COL_TILE
    grid3 = hs.reshape(hs.shape[0] // GRID_W, GRID_W, D)
    z_shape = jax.ShapeDtypeStruct((n_batch * rows, GRID_W, D), BF16)
    blk_cols = pl.BlockSpec((rows, COL_TILE, D), lambda b, c: (b, c, 0))
    mod_b = pl.BlockSpec((None, N_MOD, D), lambda b, c: (b, 0, 0))
    zr, zi = pl.pallas_call(
        _fnet_rows_kernel,
        grid=(n_batch, n_ct),
        in_specs=[blk_cols, mod_b, pl.BlockSpec(f1.shape, lambda b, c: (0, 0))],
        out_specs=[blk_cols, blk_cols],
        out_shape=[z_shape, z_shape],
        compiler_params=_cparams(("arbitrary", "arbitrary")),
        name="fnet_rows",
    )(grid3, mod_l, f1)
    n_kt = GRID_W // SUBLANE
    blk_k2 = pl.BlockSpec((SUBLANE, GRID_W, D), lambda b, k: (b * n_kt + k, 0, 0))
    blk_tok = pl.BlockSpec((rows, SUBLANE, D), lambda b, k: (b, k, 0))
    table = pl.BlockSpec((GROUP_C, GROUP_C), lambda b, k: (0, 0))
    n_tok = rows * SUBLANE
    grid3 = pl.pallas_call(
        _fnet_cols_kernel,
        grid=(n_batch, n_kt),
        in_specs=[blk_k2, blk_k2,
                  pl.BlockSpec((SUBLANE, 2 * rows, 2 * GRID_W), lambda b, k: (k, 0, 0)),
                  table, table,
                  pl.BlockSpec((D, D), lambda b, k: (0, 0)),
                  pl.BlockSpec((1, D), lambda b, k: (0, 0)),
                  blk_tok, mod_b],
        out_specs=blk_tok,
        out_shape=jax.ShapeDtypeStruct(grid3.shape, F32),
        scratch_shapes=[pltpu.VMEM((n_tok, D), BF16), pltpu.VMEM((n_tok, D), BF16),
                        pltpu.VMEM((D // LANE, n_tok, LANE), F32)],
        input_output_aliases={7: 0},
        compiler_params=_cparams(("arbitrary", "arbitrary")),
        name="fnet_cols",
    )(zr, zi, m2, cc, sc, w, b_out.reshape(1, D), grid3, mod_l)
    hs = grid3.reshape(hs.shape)
    tpb = seq // TM_LIN
    row_fn = lambda t: _mod_row(t, tpb, n_batch)
    if with_ctx:
        a = np.arange(n_ctx)
        th = 2.0 * np.pi * np.outer(a, a) / n_ctx
        s = np.sqrt(float(seq) / n_ctx)
        fc = jnp.asarray(np.concatenate([np.cos(th) * s, -np.sin(th) * s], axis=0).astype(np.float32)).astype(BF16)
        c_shape = jax.ShapeDtypeStruct((n_batch * n_ctx, D), BF16)
        tile0 = n_lat // n_ctx
        xr, xi = pl.pallas_call(
            _fnet_ctx_kernel,
            grid=(n_batch,),
            in_specs=[pl.BlockSpec((n_ctx, D), lambda b: (tile0 + b, 0)),
                      pl.BlockSpec((None, N_MOD, D), lambda b: (n_batch, 0, 0)),
                      pl.BlockSpec((2 * n_ctx, n_ctx), lambda b: (0, 0))],
            out_specs=[pl.BlockSpec((n_ctx, D), lambda b: (b, 0))] * 2,
            out_shape=[c_shape, c_shape],
            compiler_params=_cparams(("arbitrary",)),
            name="fnet_ctx",
        )(hs, mod_l, fc)
        hs = _lin_res_call(hs, (xr, xi), (cc, sc), w, b_out, mod_l,
                           n_lat // TM_LIN, n_batch * n_ctx // TM_LIN, row_fn)
    return hs


def _mixer_input_chunks(h_ref, mod_ref, xn_ref, first):
    for r0 in range(0, h_ref.shape[0], PROJ_ROWS):
        rows = slice(r0, r0 + PROJ_ROWS)
        if first:
            xn = _norm_mod(h_ref[rows, :], mod_ref[3:4, :], mod_ref[4:5, :]).astype(BF16)
            xn_ref[rows, :] = xn
        else:
            xn = xn_ref[rows, :]
        yield rows, xn


def _glu_kernel(h_ref, mod_ref, wa_ref, wg_ref, ba_ref, bg_ref, o_ref, xn_ref):
    def step(first):
        for rows, xn in _mixer_input_chunks(h_ref, mod_ref, xn_ref, first):
            a = _dot(xn, wa_ref[...]) + ba_ref[...]
            g = _dot(xn, wg_ref[...]) + bg_ref[...]
            o_ref[rows, :] = a / (1.0 + jnp.exp(-g))

    j = pl.program_id(1)
    pl.when(j == 0)(functools.partial(step, True))
    pl.when(j > 0)(functools.partial(step, False))


def _conv_kernel(u_ref, up_ref, un_ref, wdw_ref, bdw_ref, lng_ref, lnb_ref, w_ref, b_ref, h_ref, mod_ref,
                 o_ref, buf_ref, conv_ref, v_ref, *, lat_tiles, tiles_per_seq):
    i = pl.program_id(0)
    is_ctx = i >= lat_tiles
    first = jnp.logical_or(is_ctx, i % tiles_per_seq == 0)
    last = jnp.logical_or(is_ctx, i % tiles_per_seq == tiles_per_seq - 1)
    tm = u_ref.shape[0]
    n_buf = tm + 2 * CONV_HALO
    buf_ref[0, 0:CONV_HALO, :] = jnp.where(first, 0.0, up_ref[...])
    buf_ref[0, CONV_HALO:CONV_HALO + tm, :] = u_ref[...]
    buf_ref[0, CONV_HALO + tm:, :] = jnp.where(last, 0.0, un_ref[...])
    for s in range(1, SUBLANE):
        buf_ref[s] = pltpu.roll(buf_ref[0], n_buf - s, axis=0)
    off = CONV_HALO - CONV_K // 2
    for lt in range(D // LANE):
        lanes = slice(lt * LANE, (lt + 1) * LANE)
        for c0 in range(0, tm, CONV_ROWS):
            acc = jnp.zeros((CONV_ROWS, LANE), F32) + bdw_ref[:, lanes]
            for k in range(CONV_K):
                row = c0 + (off + k) // SUBLANE * SUBLANE
                acc = acc + wdw_ref[k:k + 1, lanes] * buf_ref[(off + k) % SUBLANE, row:row + CONV_ROWS, lanes]
            conv_ref[c0:c0 + CONV_ROWS, lanes] = acc
    chunk = 32
    for c0 in range(0, tm, chunk):
        acc = conv_ref[c0:c0 + chunk, :]
        mu = jnp.mean(acc, axis=-1, keepdims=True)
        d = acc - mu
        var = jnp.mean(d * d, axis=-1, keepdims=True)
        v = d * lax.rsqrt(var + EPS) * lng_ref[...] + lnb_ref[...]
        v_ref[c0:c0 + chunk, :] = _silu(v).astype(BF16)
    y = _dot(v_ref[...], w_ref[...]) + b_ref[...]
    o_ref[...] = h_ref[...] + mod_ref[5:6, :] * y


def _conv_mixer(hs, mod_l, w_in, b_in, w_dw, b_dw, ln_g, ln_b, w_out, b_out, n_batch, seq, n_ctx):
    n_lat = n_batch * seq
    n_all = n_lat + n_batch * n_ctx
    assert n_ctx == TM_CONV, "context sequences are one conv tile long"
    n_tiles = n_all // TM_FFN
    n_col = D // GLU_COLS
    tpb = seq // TM_FFN
    col = pl.BlockSpec((D, GLU_COLS), lambda i, j: (0, j))
    gcol = pl.BlockSpec((D, GLU_COLS), lambda i, j: (0, n_col + j))
    u = pl.pallas_call(
        _glu_kernel,
        grid=(n_tiles, n_col),
        in_specs=[pl.BlockSpec((TM_FFN, D), lambda i, j: (i, 0)),
                  pl.BlockSpec((None, N_MOD, D), lambda i, j: (_mod_row(i, tpb, n_batch), 0, 0)),
                  col, gcol,
                  pl.BlockSpec((1, GLU_COLS), lambda i, j: (0, j)),
                  pl.BlockSpec((1, GLU_COLS), lambda i, j: (0, n_col + j))],
        out_specs=pl.BlockSpec((TM_FFN, GLU_COLS), lambda i, j: (i, j)),
        out_shape=jax.ShapeDtypeStruct((n_all, D), F32),
        scratch_shapes=[pltpu.VMEM((TM_FFN, D), BF16)],
        compiler_params=_cparams(("arbitrary", "arbitrary")),
        name="conv_glu",
    )(hs, mod_l, w_in.astype(BF16), w_in.astype(BF16), b_in.reshape(1, 2 * D), b_in.reshape(1, 2 * D))

    n_ct = n_all // TM_CONV
    hpt = TM_CONV // CONV_HALO
    n_halo = n_all // CONV_HALO
    tps = seq // TM_CONV
    vec = pl.BlockSpec((1, D), lambda i: (0, 0))
    tile = pl.BlockSpec((TM_CONV, D), lambda i: (i, 0))
    return pl.pallas_call(
        functools.partial(_conv_kernel, lat_tiles=n_lat // TM_CONV, tiles_per_seq=tps),
        grid=(n_ct,),
        in_specs=[tile,
                  pl.BlockSpec((CONV_HALO, D), lambda i: (jnp.maximum(i * hpt - 1, 0), 0)),
                  pl.BlockSpec((CONV_HALO, D), lambda i: (jnp.minimum((i + 1) * hpt, n_halo - 1), 0)),
                  pl.BlockSpec((CONV_K, D), lambda i: (0, 0)),
                  vec, vec, vec,
                  pl.BlockSpec((D, D), lambda i: (0, 0)),
                  vec, tile,
                  pl.BlockSpec((None, N_MOD, D), lambda i: (_mod_row(i, tps, n_batch), 0, 0))],
        out_specs=tile,
        out_shape=jax.ShapeDtypeStruct(hs.shape, F32),
        scratch_shapes=[pltpu.VMEM((SUBLANE, TM_CONV + 2 * CONV_HALO, D), F32), pltpu.VMEM((TM_CONV, D), F32),
                        pltpu.VMEM((TM_CONV, D), BF16)],
        input_output_aliases={9: 0},
        compiler_params=_cparams(("arbitrary",)),
        name="conv_dw",
    )(u, u, u, w_dw, b_dw.reshape(1, D), ln_g.reshape(1, D), ln_b.reshape(1, D), w_out.astype(BF16),
      b_out.reshape(1, D), hs, mod_l)


def _qkv_kernel(h_ref, mod_ref, w_ref, gain_ref, e_ref, o_ref, xn_ref):
    n_e = e_ref.shape[0]

    def step(first, head_norm):
        for rows, xn in _mixer_input_chunks(h_ref, mod_ref, xn_ref, first):
            y = _dot(xn, w_ref[...])
            if not head_norm:
                o_ref[rows, :] = y.astype(BF16)
                continue
            for c0 in range(0, D, n_e):
                cols = slice(c0, c0 + n_e)
                yc = y[:, cols]
                ss = _dot((yc * yc).astype(BF16), e_ref[...])
                rs = lax.rsqrt(ss * (1.0 / HEAD_DIM) + EPS) * gain_ref[:, cols]
                o_ref[rows, cols] = (yc * rs).astype(BF16)

    j = pl.program_id(1)
    pl.when(j == 0)(functools.partial(step, True, True))
    pl.when(j == 1)(functools.partial(step, False, True))
    pl.when(j == 2)(functools.partial(step, False, False))


def _softmax_pv(s_list, v_list):
    m = s_list[0].max(axis=-1, keepdims=True)
    for s in s_list[1:]:
        m = jnp.maximum(m, s.max(axis=-1, keepdims=True))
    den = 0.0
    acc = 0.0
    for s, v in zip(s_list, v_list):
        p = jnp.exp(s - m)
        den = den + p.sum(axis=-1, keepdims=True)
        acc = acc + _dot(p.astype(BF16), v)
    return acc / den


def _qk(q, k):
    return lax.dot_general(q, k, (((1,), (1,)), ((), ())), preferred_element_type=F32)


def _natten_kernel(q_ref, k_ref, v_ref, kc_ref, vc_ref, bias_ref, o_ref, *, rows, kr):
    r0 = pl.program_id(1) * NATTEN_ROWS
    left = lax.broadcasted_iota(jnp.int32, (GRID_W, LANE), 1) < HEAD_DIM
    n2 = 2 * GRID_W
    wins, dr0s = [], []
    for i in range(NATTEN_ROWS):
        rs = jnp.clip(r0 + i - kr // 2, 0, rows - kr)
        wins.append(pl.ds(pl.multiple_of(rs * GRID_W, GRID_W), kr * GRID_W))
        dr0s.append(rs - (r0 + i) + WIN_R - 1)
    for p in range(N_HEADS // 2):
        cols = slice(p * LANE, (p + 1) * LANE)
        q2 = []
        for i in range(NATTEN_ROWS):
            q = q_ref[i * GRID_W:(i + 1) * GRID_W, cols]
            zero = jnp.zeros_like(q)
            q2 += [jnp.where(left, q, zero), jnp.where(left, zero, q)]
        q2 = jnp.concatenate(q2, axis=0)
        s_ctx = _qk(q2, kc_ref[:, cols])
        m_ctx = s_ctx.max(axis=-1, keepdims=True)
        m, den, acc = [], [], []
        for i in range(NATTEN_ROWS):
            mine = slice(i * n2, (i + 1) * n2)
            bias = jnp.concatenate([bias_ref[p, dr0s[i] + 2 * t] for t in range(kr // 2)], axis=-1)
            s_lat = _qk(q2[mine], k_ref[wins[i], cols]) + bias
            m.append(jnp.maximum(s_lat.max(axis=-1, keepdims=True), m_ctx[mine]))
            p_lat = jnp.exp(s_lat - m[i])
            den.append(p_lat.sum(axis=-1, keepdims=True))
            acc.append(_dot(p_lat.astype(BF16), v_ref[wins[i], cols]))
        p_ctx = jnp.exp(s_ctx - jnp.concatenate(m, axis=0))
        den_ctx = p_ctx.sum(axis=-1, keepdims=True)
        acc_ctx = _dot(p_ctx.astype(BF16), vc_ref[:, cols])
        for i in range(NATTEN_ROWS):
            mine = slice(i * n2, (i + 1) * n2)
            o = (acc[i] + acc_ctx[mine]) / (den[i] + den_ctx[mine])
            o_ref[i * GRID_W:(i + 1) * GRID_W, cols] = jnp.where(left, o[:GRID_W], o[GRID_W:]).astype(BF16)


def _bias_kernel(rpb_ref, o_ref):
    qc = lax.broadcasted_iota(jnp.int32, (GRID_W, LANE), 0)
    lane = lax.broadcasted_iota(jnp.int32, (GRID_W, LANE), 1)
    kc = lane % GRID_W
    w0 = jnp.clip(qc - WIN_C // 2, 0, GRID_W - WIN_C)
    ok = jnp.logical_and(kc >= w0, kc < w0 + WIN_C)
    sub = lax.broadcasted_iota(jnp.int32, (SUBLANE, LANE), 0)
    tiles = []
    for dr in range(2 * WIN_R - 1):
        t = jnp.broadcast_to(rpb_ref[dr:dr + 1, :], (SUBLANE, LANE))
        t = pltpu.roll(t, LANE - (WIN_C - 1), axis=1)
        for bit in range(SUBLANE.bit_length() - 1):
            t = jnp.where((sub >> bit) & 1 == 1, pltpu.roll(t, 1 << bit, axis=1), t)
        tiles.append(jnp.concatenate([t] + [pltpu.roll(t, SUBLANE * a, axis=1)
                                            for a in range(1, GRID_W // SUBLANE)], axis=0))
    for p in range(2 * WIN_R - 2):
        pair = jnp.where(lane < GRID_W, tiles[p], pltpu.roll(tiles[p + 1], GRID_W, axis=1))
        o_ref[p] = jnp.where(ok, pair, NEG_INF)


def _ctx_attn_kernel(q_ref, k_ref, v_ref, o_ref):
    outs = []
    for h in range(N_HEADS):
        hd = slice(h * HEAD_DIM, (h + 1) * HEAD_DIM)
        outs.append(_softmax_pv([_qk(q_ref[:, hd], k_ref[:, hd])], [v_ref[:, hd]]))
    o_ref[...] = jnp.concatenate(outs, axis=-1).astype(BF16)


def _natten_bias(rpb):
    n_dr, n_dc = 2 * WIN_R - 1, 2 * WIN_C - 1
    assert rpb.shape == (N_HEADS, n_dr, n_dc) and GRID_W * 2 == LANE
    padded = jnp.pad(rpb.astype(F32), ((0, 0), (0, 0), (0, LANE - n_dc)))
    return pl.pallas_call(
        _bias_kernel,
        grid=(N_HEADS,),
        in_specs=[pl.BlockSpec((None, n_dr, LANE), lambda h: (h, 0, 0))],
        out_specs=pl.BlockSpec((None, n_dr - 1, GRID_W, LANE), lambda h: (h // 2, 0, h % 2, 0)),
        out_shape=jax.ShapeDtypeStruct((N_HEADS // 2, n_dr - 1, 2 * GRID_W, LANE), F32),
        compiler_params=_cparams(("arbitrary",)),
        name="natten_bias",
    )(padded)


def _natten_mixer(hs, mod_l, w_qkv, q_gain, k_gain, rpb, w_o, n_batch, seq, n_ctx):
    n_lat = n_batch * seq
    n_all = n_lat + n_batch * n_ctx
    rows = seq // GRID_W
    kr = min(WIN_R, rows)
    n_tiles = n_all // TM_FFN
    tpb = seq // TM_FFN
    scale = HEAD_DIM ** -0.5
    gain = jnp.concatenate([jnp.tile(q_gain, N_HEADS) * scale, jnp.tile(k_gain, N_HEADS),
                            jnp.ones((D,), F32)]).reshape(1, 3 * D)
    lane = np.arange(FF_TILE)
    e = jnp.asarray((lane[:, None] // HEAD_DIM == lane[None, :] // HEAD_DIM).astype(np.float32)).astype(BF16)
    qkv = pl.pallas_call(
        _qkv_kernel,
        grid=(n_tiles, 3),
        in_specs=[pl.BlockSpec((TM_FFN, D), lambda i, j: (i, 0)),
                  pl.BlockSpec((None, N_MOD, D), lambda i, j: (_mod_row(i, tpb, n_batch), 0, 0)),
                  pl.BlockSpec((D, D), lambda i, j: (0, j)),
                  pl.BlockSpec((1, D), lambda i, j: (0, j)),
                  pl.BlockSpec((FF_TILE, FF_TILE), lambda i, j: (0, 0))],
        out_specs=pl.BlockSpec((TM_FFN, D), lambda i, j: (i, j)),
        out_shape=jax.ShapeDtypeStruct((n_all, 3 * D), BF16),
        scratch_shapes=[pltpu.VMEM((TM_FFN, D), BF16)],
        compiler_params=_cparams(("arbitrary", "arbitrary")),
        name="qkv",
    )(hs, mod_l, w_qkv.astype(BF16), gain, e)

    assert kr == WIN_R, "bias tiles pair up the key rows of a full window"
    bias = _natten_bias(rpb)
    ctx0 = n_lat // n_ctx
    once = pl.Buffered(1)
    n_rb = rows // NATTEN_ROWS
    o_lat = pl.pallas_call(
        functools.partial(_natten_kernel, rows=rows, kr=kr),
        grid=(n_batch, n_rb),
        in_specs=[pl.BlockSpec((NATTEN_ROWS * GRID_W, D), lambda b, r: (b * n_rb + r, 0)),
                  pl.BlockSpec((seq, D), lambda b, r: (b, 1)),
                  pl.BlockSpec((seq, D), lambda b, r: (b, 2), pipeline_mode=once),
                  pl.BlockSpec((n_ctx, D), lambda b, r: (ctx0 + b, 1)),
                  pl.BlockSpec((n_ctx, D), lambda b, r: (ctx0 + b, 2)),
                  pl.BlockSpec(bias.shape, lambda b, r: (0, 0, 0, 0), pipeline_mode=once)],
        out_specs=pl.BlockSpec((NATTEN_ROWS * GRID_W, D), lambda b, r: (b * n_rb + r, 0)),
        out_shape=jax.ShapeDtypeStruct((n_lat, D), BF16),
        compiler_params=_cparams(("arbitrary", "arbitrary")),
        name="natten",
    )(qkv, qkv, qkv, qkv, qkv, bias)
    o_ctx = pl.pallas_call(
        _ctx_attn_kernel,
        grid=(n_batch,),
        in_specs=[pl.BlockSpec((n_ctx, D), lambda b: (ctx0 + b, 0)),
                  pl.BlockSpec((n_ctx, D), lambda b: (ctx0 + b, 1)),
                  pl.BlockSpec((n_ctx, D), lambda b: (ctx0 + b, 2))],
        out_specs=pl.BlockSpec((n_ctx, D), lambda b: (b, 0)),
        out_shape=jax.ShapeDtypeStruct((n_batch * n_ctx, D), BF16),
        compiler_params=_cparams(("arbitrary",)),
        name="ctx_attn",
    )(qkv, qkv, qkv)
    w = w_o.astype(BF16)
    zero_b = jnp.zeros((D,), F32)
    row_fn = lambda t: _mod_row(t, seq // TM_LIN, n_batch)
    hs = _lin_res_call(hs, (o_lat,), (), w, zero_b, mod_l, 0, n_lat // TM_LIN, row_fn)
    return _lin_res_call(hs, (o_ctx,), (), w, zero_b, mod_l, n_lat // TM_LIN, n_batch * n_ctx // TM_LIN, row_fn)


def kernel(x, c, ctx, c_ctx, w_mod, b_mod, w_ff_in, w_ff_out, fnet_w_out, fnet_b_out, conv_w_in, conv_b_in, conv_w_dw, conv_b_dw, conv_ln_g, conv_ln_b, conv_w_out, conv_b_out, na_w_qkv, na_q_gain, na_k_gain, na_rpb, na_w_o):
    n_batch, seq, d = x.shape
    n_ctx = ctx.shape[1]
    depth = w_mod.shape[0]
    assert d == D and n_batch < MOD_ROWS
    assert seq % TM_FFN == 0 and (n_batch * n_ctx) % TM_FFN == 0
    n_lat = n_batch * seq

    cond = jnp.concatenate([c, c_ctx[None, :], jnp.zeros((MOD_ROWS - n_batch - 1, D), F32)], axis=0)
    mod = _mod_call(cond, w_mod, b_mod).reshape(depth, MOD_ROWS, N_MOD, D)
    n_tok = n_lat + n_batch * n_ctx
    n_stream = n_tok + (-n_tok % seq)
    assert depth > 1, "the first half-step is written for a layer that updates the context tokens"
    hs = (x.reshape(n_lat, D), ctx.reshape(n_batch * n_ctx, D))
    lat_tiles = n_lat // TM_FFN
    all_tiles = n_tok // TM_FFN
    w_gu, w_o = w_ff_in, w_ff_out

    for i in range(depth):
        kind, slot = i % 3, i // 3
        ctx_post = i < depth - 1
        ctx_pre = ctx_post or kind == 2
        mod_l = mod[i]
        hs = _ffn_call(hs, mod_l, w_gu, w_o, i, 0, 0, all_tiles if ctx_pre else lat_tiles,
                       n_batch, seq, i > 0, n_stream)
        if kind == 0:
            hs = _fourier_mixer(hs, mod_l, fnet_w_out[slot], fnet_b_out[slot], n_batch, seq, n_ctx, ctx_post)
        elif kind == 1:
            assert ctx_post
            hs = _conv_mixer(hs, mod_l, conv_w_in[slot], conv_b_in[slot], conv_w_dw[slot], conv_b_dw[slot],
                             conv_ln_g[slot], conv_ln_b[slot], conv_w_out[slot], conv_b_out[slot],
                             n_batch, seq, n_ctx)
        else:
            assert ctx_post
            hs = _natten_mixer(hs, mod_l, na_w_qkv[slot], na_q_gain[slot], na_k_gain[slot], na_rpb[slot],
                               na_w_o[slot], n_batch, seq, n_ctx)
        final = i == depth - 1
        hs = _ffn_call(hs, mod_l, w_gu, w_o, i, 1, 6, all_tiles if ctx_post else lat_tiles,
                       n_batch, seq, not final)
    return hs[:n_lat].reshape(n_batch, seq, D)
```

```python
import functools

import numpy as np
import jax
import jax.numpy as jnp
from jax import lax
from jax.experimental import pallas as pl
from jax.experimental.pallas import tpu as pltpu

F32 = jnp.float32
BF16 = jnp.bfloat16

D = 1024
D_FF = 2816
N_MOD = 9
EPS = 1e-6
GRID_W = 64
LANE = 128
SUBLANE = 8
N_GROUPS = 4
GROUP_C = D // N_GROUPS
CONV_K = 31
CONV_HALO = 16
N_HEADS = 16
HEAD_DIM = D // N_HEADS
WIN_R = 8
WIN_C = 16
NEG_INF = -1e30
MOD_ROWS = 8

FF_TILE = 256
TM_FFN = 1024
PROJ_ROWS = 512
GLU_COLS = 512
FFN_ROWS = 256
TM_LIN = 512
TM_CONV = 256
CONV_ROWS = 64
COL_TILE = 16
NATTEN_ROWS = 4
VMEM_LIMIT = 52 * 1024 * 1024


def _cparams(sem):
    return pltpu.CompilerParams(dimension_semantics=sem, vmem_limit_bytes=VMEM_LIMIT)


def _silu(x):
    return x / (1.0 + jnp.exp(-x))


def _norm_mod(x, shift, scale):
    ms = jnp.mean(x * x, axis=-1, keepdims=True)
    return x * lax.rsqrt(ms + EPS) * (1.0 + scale) + shift


def _dot(a, b):
    return jnp.dot(a, b, preferred_element_type=F32)


def _mod_kernel(c_ref, w_ref, b_ref, o_ref):
    s = _silu(c_ref[...]).astype(BF16)
    o_ref[...] = _dot(s, w_ref[...].astype(BF16)) + b_ref[...]


def _mod_call(cond, w_mod, b_mod):
    depth = w_mod.shape[0]
    return pl.pallas_call(
        _mod_kernel,
        grid=(depth, N_MOD),
        in_specs=[
            pl.BlockSpec((MOD_ROWS, D), lambda l, k: (0, 0)),
            pl.BlockSpec((None, D, D), lambda l, k: (l, 0, k)),
            pl.BlockSpec((None, 1, D), lambda l, k: (l, 0, k)),
        ],
        out_specs=pl.BlockSpec((None, MOD_ROWS, D), lambda l, k: (l, 0, k)),
        out_shape=jax.ShapeDtypeStruct((depth, MOD_ROWS, N_MOD * D), F32),
        compiler_params=_cparams(("arbitrary", "arbitrary")),
        name="mod",
    )(cond, w_mod, b_mod.reshape(depth, 1, N_MOD * D))


def _mod_row(tile, tiles_per_batch, n_batch):
    return jnp.minimum(tile // tiles_per_batch, n_batch)


def _ffn_kernel(*refs, k0, n_ff, lat_tiles):
    if lat_tiles is None:
        h_ref, mod_ref, wg_ref, wu_ref, wo_ref, o_ref, xn_ref, acc_ref = refs
        tokens = lambda rows: h_ref[rows, :]
    else:
        lat_ref, ctx_ref, mod_ref, wg_ref, wu_ref, wo_ref, o_ref, xn_ref, acc_ref = refs
        is_lat = pl.program_id(0) < lat_tiles
        tokens = lambda rows: jnp.where(is_lat, lat_ref[rows, :], ctx_ref[rows, :])
    j = pl.program_id(1)
    n_chunks = TM_FFN // FFN_ROWS
    assert n_ff >= 2

    def step(first, last):
        wg, wu, wo = wg_ref[...].astype(BF16), wu_ref[...].astype(BF16), wo_ref[...].astype(BF16)
        gu = {}
        for c in range(n_chunks + 1):
            if c < n_chunks:
                rows = slice(c * FFN_ROWS, (c + 1) * FFN_ROWS)
                if first:
                    xn = _norm_mod(tokens(rows), mod_ref[k0:k0 + 1, :], mod_ref[k0 + 1:k0 + 2, :]).astype(BF16)
                    xn_ref[rows, :] = xn
                else:
                    xn = xn_ref[rows, :]
                gu[c] = (_dot(xn, wg), _dot(xn, wu))
            if c >= 1:
                rows = slice((c - 1) * FFN_ROWS, c * FFN_ROWS)
                g, u = gu.pop(c - 1)
                y = _dot((_silu(g) * u).astype(BF16), wo)
                if first:
                    acc_ref[rows, :] = y
                elif last:
                    o_ref[rows, :] = tokens(rows) + (0.5 * mod_ref[k0 + 2:k0 + 3, :]) * (acc_ref[rows, :] + y)
                else:
                    acc_ref[rows, :] += y

    real = True if lat_tiles is None else pl.program_id(0) <= lat_tiles
    pl.when(jnp.logical_and(real, j == 0))(functools.partial(step, True, False))
    pl.when(jnp.logical_and(real, jnp.logical_and(j > 0, j < n_ff - 1)))(functools.partial(step, False, False))
    pl.when(jnp.logical_and(real, j == n_ff - 1))(functools.partial(step, False, True))
    if lat_tiles is not None:
        @pl.when(jnp.logical_and(jnp.logical_not(real), j == n_ff - 1))
        def _():
            o_ref[...] = jnp.zeros_like(o_ref)


def _ffn_call(hs, mod_l, w_gu, w_o, layer, which, k0, n_tiles, n_batch, seq, alias, out_rows=None):
    n_ff = D_FF // FF_TILE
    tpb = seq // TM_FFN
    ff = lambda i, j: j
    if isinstance(hs, tuple):
        lat, ctx = hs
        lat_tiles = lat.shape[0] // TM_FFN
        assert not alias and ctx.shape[0] == TM_FFN and n_tiles == lat_tiles + 1
        tokens = (lat, ctx)
        token_specs = [pl.BlockSpec((TM_FFN, D), lambda i, j: (jnp.minimum(i, lat_tiles - 1), 0)),
                       pl.BlockSpec((TM_FFN, D), lambda i, j: (0, 0))]
        n_real = n_tiles
        ff = lambda i, j: jnp.where(i < n_real, j, n_ff - 1)
        n_tiles = out_rows // TM_FFN
    else:
        lat_tiles = None
        tokens = (hs,)
        token_specs = [pl.BlockSpec((TM_FFN, D), lambda i, j: (i, 0))]
        out_rows = hs.shape[0] if alias else n_tiles * TM_FFN
    return pl.pallas_call(
        functools.partial(_ffn_kernel, k0=k0, n_ff=n_ff, lat_tiles=lat_tiles),
        grid=(n_tiles, n_ff),
        in_specs=token_specs + [
            pl.BlockSpec((None, N_MOD, D), lambda i, j: (_mod_row(i, tpb, n_batch), 0, 0)),
            pl.BlockSpec((None, None, D, FF_TILE), lambda i, j: (layer, which, 0, ff(i, j))),
            pl.BlockSpec((None, None, D, FF_TILE), lambda i, j: (layer, which, 0, n_ff + ff(i, j))),
            pl.BlockSpec((None, None, FF_TILE, D), lambda i, j: (layer, which, ff(i, j), 0)),
        ],
        out_specs=pl.BlockSpec((TM_FFN, D), lambda i, j: (i, 0)),
        out_shape=jax.ShapeDtypeStruct((out_rows, D), F32),
        scratch_shapes=[pltpu.VMEM((TM_FFN, D), BF16), pltpu.VMEM((TM_FFN, D), F32)],
        input_output_aliases={0: 0} if alias else {},
        compiler_params=_cparams(("arbitrary", "arbitrary")),
        name="ffn",
    )(*tokens, mod_l, w_gu, w_gu, w_o)


def _lin_res_kernel(*refs, fourier):
    if fourier:
        xr_ref, xi_ref, cc_ref, sc_ref, w_ref, b_ref, h_ref, mod_ref, o_ref = refs
        parts = []
        for g in range(N_GROUPS):
            sl = slice(g * GROUP_C, (g + 1) * GROUP_C)
            parts.append(_dot(xr_ref[:, sl], cc_ref[...]) + _dot(xi_ref[:, sl], sc_ref[...]))
        x = jnp.concatenate(parts, axis=-1).astype(BF16)
    else:
        x_ref, w_ref, b_ref, h_ref, mod_ref, o_ref = refs
        x = x_ref[...]
    y = _dot(x, w_ref[...]) + b_ref[...]
    o_ref[...] = h_ref[...] + mod_ref[5:6, :] * y


def _lin_res_call(hs, xs, tables, w, b, mod_l, tile0, n_tiles, mod_row_fn):
    fourier = len(xs) == 2
    x_spec = pl.BlockSpec((TM_LIN, D), lambda i: (i, 0))
    t_spec = pl.BlockSpec((GROUP_C, GROUP_C), lambda i: (0, 0))
    in_specs = [x_spec] * len(xs) + [t_spec] * len(tables) + [
        pl.BlockSpec((D, D), lambda i: (0, 0)),
        pl.BlockSpec((1, D), lambda i: (0, 0)),
        pl.BlockSpec((TM_LIN, D), lambda i: (i + tile0, 0)),
        pl.BlockSpec((None, N_MOD, D), lambda i: (mod_row_fn(i + tile0), 0, 0)),
    ]
    h_index = len(xs) + len(tables) + 2
    return pl.pallas_call(
        functools.partial(_lin_res_kernel, fourier=fourier),
        grid=(n_tiles,),
        in_specs=in_specs,
        out_specs=pl.BlockSpec((TM_LIN, D), lambda i: (i + tile0, 0)),
        out_shape=jax.ShapeDtypeStruct(hs.shape, F32),
        input_output_aliases={h_index: 0},
        compiler_params=_cparams(("arbitrary",)),
        name="lin_res_fourier" if fourier else "lin_res",
    )(*xs, *tables, w, b.reshape(1, D), hs, mod_l)


def _fnet_rows_kernel(h_ref, mod_ref, kf_ref, zr_ref, zi_ref):
    rows = h_ref.shape[0]
    half = rows // 2
    xn = _norm_mod(h_ref[...], mod_ref[3:4, :], mod_ref[4:5, :])
    n = rows * SUBLANE
    n_cos = (half + 1) * SUBLANE
    zr, zi = [], []
    for c0 in range(0, COL_TILE, SUBLANE):
        a = xn[:, c0:c0 + SUBLANE, :].reshape(n, D).astype(BF16)
        z = _dot(kf_ref[...], a)
        zc = z[:n_cos].reshape(half + 1, SUBLANE, D)
        zs = z[n_cos:].reshape(half - 1, SUBLANE, D)
        zero = jnp.zeros((1, SUBLANE, D), F32)
        zr.append(jnp.concatenate([zc] + [zc[rows - k:rows - k + 1] for k in range(half + 1, rows)], axis=0))
        zi.append(jnp.concatenate([zero, zs, zero] + [-zs[rows - k - 1:rows - k] for k in range(half + 1, rows)],
                                  axis=0))
    zr_ref[...] = jnp.concatenate(zr, axis=1).astype(BF16)
    zi_ref[...] = jnp.concatenate(zi, axis=1).astype(BF16)


def _fnet_cols_kernel(zr_ref, zi_ref, m2_ref, cc_ref, sc_ref, w_ref, b_ref, h_ref, mod_ref, o_ref,
                      xr_ref, xi_ref, y_ref):
    n = zr_ref.shape[1]
    for j in range(SUBLANE):
        z = jnp.concatenate([zr_ref[j], zi_ref[j]], axis=0)
        x = _dot(m2_ref[j], z)
        xr_ref[j * n:(j + 1) * n, :] = x[:n].astype(BF16)
        xi_ref[j * n:(j + 1) * n, :] = x[n:].astype(BF16)
    parts = []
    for g in range(N_GROUPS):
        sl = slice(g * GROUP_C, (g + 1) * GROUP_C)
        parts.append(_dot(xr_ref[:, sl], cc_ref[...]) + _dot(xi_ref[:, sl], sc_ref[...]))
    y = _dot(jnp.concatenate(parts, axis=-1).astype(BF16), w_ref[...]) + b_ref[...]
    for j in range(SUBLANE):
        rows = pl.ds(j, n, stride=SUBLANE)
        for lt in range(D // LANE):
            y_ref[lt, rows, :] = y[j * n:(j + 1) * n, lt * LANE:(lt + 1) * LANE]
    y_t = jnp.concatenate([y_ref[lt] for lt in range(D // LANE)], axis=-1).reshape(n, SUBLANE, D)
    o_ref[...] = h_ref[...] + mod_ref[5:6, :] * y_t


def _fnet_ctx_kernel(h_ref, mod_ref, f_ref, xr_ref, xi_ref):
    n = h_ref.shape[0]
    xn = _norm_mod(h_ref[...], mod_ref[3:4, :], mod_ref[4:5, :]).astype(BF16)
    x = _dot(f_ref[...], xn)
    xr_ref[...] = x[:n].astype(BF16)
    xi_ref[...] = x[n:].astype(BF16)


def _dft_tables(rows):
    n_seq = rows * GRID_W
    a = np.arange(rows)
    th1 = 2.0 * np.pi * np.outer(a, a) / rows
    half = rows // 2
    f1 = np.concatenate([np.cos(th1)[:half + 1], -np.sin(th1)[1:half]], axis=0)
    f1 = np.kron(f1, np.eye(SUBLANE))
    k2 = np.arange(GRID_W)[:, None, None]
    k1 = np.arange(rows)[None, :, None]
    n1 = np.arange(GRID_W)[None, None, :]
    th2 = 2.0 * np.pi * (n1 * (GRID_W * k1 + k2) % n_seq) / n_seq
    c2, s2 = np.cos(th2), np.sin(th2)
    m2 = np.concatenate([np.concatenate([c2, s2], axis=2), np.concatenate([-s2, c2], axis=2)], axis=1)
    return f1.astype(np.float32), m2.astype(np.float32)


def _channel_tables(n_seq):
    c = np.arange(GROUP_C)
    th = 2.0 * np.pi * np.outer(c, c) / GROUP_C
    s = 1.0 / np.sqrt(float(n_seq) * GROUP_C)
    return (np.cos(th) * s).astype(np.float32), (np.sin(th) * s).astype(np.float32)


def _fourier_mixer(hs, mod_l, w_out, b_out, n_batch, seq, n_ctx, with_ctx):
    rows = seq // GRID_W
    assert rows == GRID_W, "the two-stage DFT is written for a square token grid"
    n_lat = n_batch * seq
    f1, m2 = _dft_tables(rows)
    cc, sc = _channel_tables(seq)
    f1, m2, cc, sc = (jnp.asarray(t).astype(BF16) for t in (f1, m2, cc, sc))
    w = w_out.astype(BF16)
    n_ct = GRID_W // COL_TILE
    grid3 = hs.reshape(hs.shape[0] // GRID_W, GRID_W, D)
    z_shape = jax.ShapeDtypeStruct((n_batch * rows, GRID_W, D), BF16)
    blk_cols = pl.BlockSpec((rows, COL_TILE, D), lambda b, c: (b, c, 0))
    mod_b = pl.BlockSpec((None, N_MOD, D), lambda b, c: (b, 0, 0))
    zr, zi = pl.pallas_call(
        _fnet_rows_kernel,
        grid=(n_batch, n_ct),
        in_specs=[blk_cols, mod_b, pl.BlockSpec(f1.shape, lambda b, c: (0, 0))],
        out_specs=[blk_cols, blk_cols],
        out_shape=[z_shape, z_shape],
        compiler_params=_cparams(("arbitrary", "arbitrary")),
        name="fnet_rows",
    )(grid3, mod_l, f1)
    n_kt = GRID_W // SUBLANE
    blk_k2 = pl.BlockSpec((SUBLANE, GRID_W, D), lambda b, k: (b * n_kt + k, 0, 0))
    blk_tok = pl.BlockSpec((rows, SUBLANE, D), lambda b, k: (b, k, 0))
    table = pl.BlockSpec((GROUP_C, GROUP_C), lambda b, k: (0, 0))
    n_tok = rows * SUBLANE
    grid3 = pl.pallas_call(
        _fnet_cols_kernel,
        grid=(n_batch, n_kt),
        in_specs=[blk_k2, blk_k2,
                  pl.BlockSpec((SUBLANE, 2 * rows, 2 * GRID_W), lambda b, k: (k, 0, 0)),
                  table, table,
                  pl.BlockSpec((D, D), lambda b, k: (0, 0)),
                  pl.BlockSpec((1, D), lambda b, k: (0, 0)),
                  blk_tok, mod_b],
        out_specs=blk_tok,
        out_shape=jax.ShapeDtypeStruct(grid3.shape, F32),
        scratch_shapes=[pltpu.VMEM((n_tok, D), BF16), pltpu.VMEM((n_tok, D), BF16),
                        pltpu.VMEM((D // LANE, n_tok, LANE), F32)],
        input_output_aliases={7: 0},
        compiler_params=_cparams(("arbitrary", "arbitrary")),
        name="fnet_cols",
    )(zr, zi, m2, cc, sc, w, b_out.reshape(1, D), grid3, mod_l)
    hs = grid3.reshape(hs.shape)
    tpb = seq // TM_LIN
    row_fn = lambda t: _mod_row(t, tpb, n_batch)
    if with_ctx:
        a = np.arange(n_ctx)
        th = 2.0 * np.pi * np.outer(a, a) / n_ctx
        s = np.sqrt(float(seq) / n_ctx)
        fc = jnp.asarray(np.concatenate([np.cos(th) * s, -np.sin(th) * s], axis=0).astype(np.float32)).astype(BF16)
        c_shape = jax.ShapeDtypeStruct((n_batch * n_ctx, D), BF16)
        tile0 = n_lat // n_ctx
        xr, xi = pl.pallas_call(
            _fnet_ctx_kernel,
            grid=(n_batch,),
            in_specs=[pl.BlockSpec((n_ctx, D), lambda b: (tile0 + b, 0)),
                      pl.BlockSpec((None, N_MOD, D), lambda b: (n_batch, 0, 0)),
                      pl.BlockSpec((2 * n_ctx, n_ctx), lambda b: (0, 0))],
            out_specs=[pl.BlockSpec((n_ctx, D), lambda b: (b, 0))] * 2,
            out_shape=[c_shape, c_shape],
            compiler_params=_cparams(("arbitrary",)),
            name="fnet_ctx",
        )(hs, mod_l, fc)
        hs = _lin_res_call(hs, (xr, xi), (cc, sc), w, b_out, mod_l,
                           n_lat // TM_LIN, n_batch * n_ctx // TM_LIN, row_fn)
    return hs


def _mixer_input_chunks(h_ref, mod_ref, xn_ref, first):
    for r0 in range(0, h_ref.shape[0], PROJ_ROWS):
        rows = slice(r0, r0 + PROJ_ROWS)
        if first:
            xn = _norm_mod(h_ref[rows, :], mod_ref[3:4, :], mod_ref[4:5, :]).astype(BF16)
            xn_ref[rows, :] = xn
        else:
            xn = xn_ref[rows, :]
        yield rows, xn


def _glu_kernel(h_ref, mod_ref, wa_ref, wg_ref, ba_ref, bg_ref, o_ref, xn_ref):
    def step(first):
        for rows, xn in _mixer_input_chunks(h_ref, mod_ref, xn_ref, first):
            a = _dot(xn, wa_ref[...]) + ba_ref[...]
            g = _dot(xn, wg_ref[...]) + bg_ref[...]
            o_ref[rows, :] = a / (1.0 + jnp.exp(-g))

    j = pl.program_id(1)
    pl.when(j == 0)(functools.partial(step, True))
    pl.when(j > 0)(functools.partial(step, False))


def _conv_kernel(u_ref, up_ref, un_ref, wdw_ref, bdw_ref, lng_ref, lnb_ref, w_ref, b_ref, h_ref, mod_ref,
                 o_ref, buf_ref, conv_ref, v_ref, *, lat_tiles, tiles_per_seq):
    i = pl.program_id(0)
    is_ctx = i >= lat_tiles
    first = jnp.logical_or(is_ctx, i % tiles_per_seq == 0)
    last = jnp.logical_or(is_ctx, i % tiles_per_seq == tiles_per_seq - 1)
    tm = u_ref.shape[0]
    n_buf = tm + 2 * CONV_HALO
    buf_ref[0, 0:CONV_HALO, :] = jnp.where(first, 0.0, up_ref[...])
    buf_ref[0, CONV_HALO:CONV_HALO + tm, :] = u_ref[...]
    buf_ref[0, CONV_HALO + tm:, :] = jnp.where(last, 0.0, un_ref[...])
    for s in range(1, SUBLANE):
        buf_ref[s] = pltpu.roll(buf_ref[0], n_buf - s, axis=0)
    off = CONV_HALO - CONV_K // 2
    for lt in range(D // LANE):
        lanes = slice(lt * LANE, (lt + 1) * LANE)
        for c0 in range(0, tm, CONV_ROWS):
            acc = jnp.zeros((CONV_ROWS, LANE), F32) + bdw_ref[:, lanes]
            for k in range(CONV_K):
                row = c0 + (off + k) // SUBLANE * SUBLANE
                acc = acc + wdw_ref[k:k + 1, lanes] * buf_ref[(off + k) % SUBLANE, row:row + CONV_ROWS, lanes]
            conv_ref[c0:c0 + CONV_ROWS, lanes] = acc
    chunk = 32
    for c0 in range(0, tm, chunk):
        acc = conv_ref[c0:c0 + chunk, :]
        mu = jnp.mean(acc, axis=-1, keepdims=True)
        d = acc - mu
        var = jnp.mean(d * d, axis=-1, keepdims=True)
        v = d * lax.rsqrt(var + EPS) * lng_ref[...] + lnb_ref[...]
        v_ref[c0:c0 + chunk, :] = _silu(v).astype(BF16)
    y = _dot(v_ref[...], w_ref[...]) + b_ref[...]
    o_ref[...] = h_ref[...] + mod_ref[5:6, :] * y


def _conv_mixer(hs, mod_l, w_in, b_in, w_dw, b_dw, ln_g, ln_b, w_out, b_out, n_batch, seq, n_ctx):
    n_lat = n_batch * seq
    n_all = n_lat + n_batch * n_ctx
    assert n_ctx == TM_CONV, "context sequences are one conv tile long"
    n_tiles = n_all // TM_FFN
    n_col = D // GLU_COLS
    tpb = seq // TM_FFN
    col = pl.BlockSpec((D, GLU_COLS), lambda i, j: (0, j))
    gcol = pl.BlockSpec((D, GLU_COLS), lambda i, j: (0, n_col + j))
    u = pl.pallas_call(
        _glu_kernel,
        grid=(n_tiles, n_col),
        in_specs=[pl.BlockSpec((TM_FFN, D), lambda i, j: (i, 0)),
                  pl.BlockSpec((None, N_MOD, D), lambda i, j: (_mod_row(i, tpb, n_batch), 0, 0)),
                  col, gcol,
                  pl.BlockSpec((1, GLU_COLS), lambda i, j: (0, j)),
                  pl.BlockSpec((1, GLU_COLS), lambda i, j: (0, n_col + j))],
        out_specs=pl.BlockSpec((TM_FFN, GLU_COLS), lambda i, j: (i, j)),
        out_shape=jax.ShapeDtypeStruct((n_all, D), F32),
        scratch_shapes=[pltpu.VMEM((TM_FFN, D), BF16)],
        compiler_params=_cparams(("arbitrary", "arbitrary")),
        name="conv_glu",
    )(hs, mod_l, w_in.astype(BF16), w_in.astype(BF16), b_in.reshape(1, 2 * D), b_in.reshape(1, 2 * D))

    n_ct = n_all // TM_CONV
    hpt = TM_CONV // CONV_HALO
    n_halo = n_all // CONV_HALO
    tps = seq // TM_CONV
    vec = pl.BlockSpec((1, D), lambda i: (0, 0))
    tile = pl.BlockSpec((TM_CONV, D), lambda i: (i, 0))
    return pl.pallas_call(
        functools.partial(_conv_kernel, lat_tiles=n_lat // TM_CONV, tiles_per_seq=tps),
        grid=(n_ct,),
        in_specs=[tile,
                  pl.BlockSpec((CONV_HALO, D), lambda i: (jnp.maximum(i * hpt - 1, 0), 0)),
                  pl.BlockSpec((CONV_HALO, D), lambda i: (jnp.minimum((i + 1) * hpt, n_halo - 1), 0)),
                  pl.BlockSpec((CONV_K, D), lambda i: (0, 0)),
                  vec, vec, vec,
                  pl.BlockSpec((D, D), lambda i: (0, 0)),
                  vec, tile,
                  pl.BlockSpec((None, N_MOD, D), lambda i: (_mod_row(i, tps, n_batch), 0, 0))],
        out_specs=tile,
        out_shape=jax.ShapeDtypeStruct(hs.shape, F32),
        scratch_shapes=[pltpu.VMEM((SUBLANE, TM_CONV + 2 * CONV_HALO, D), F32), pltpu.VMEM((TM_CONV, D), F32),
                        pltpu.VMEM((TM_CONV, D), BF16)],
        input_output_aliases={9: 0},
        compiler_params=_cparams(("arbitrary",)),
        name="conv_dw",
    )(u, u, u, w_dw, b_dw.reshape(1, D), ln_g.reshape(1, D), ln_b.reshape(1, D), w_out.astype(BF16),
      b_out.reshape(1, D), hs, mod_l)


def _qkv_kernel(h_ref, mod_ref, w_ref, gain_ref, e_ref, o_ref, xn_ref):
    n_e = e_ref.shape[0]

    def step(first, head_norm):
        for rows, xn in _mixer_input_chunks(h_ref, mod_ref, xn_ref, first):
            y = _dot(xn, w_ref[...])
            if not head_norm:
                o_ref[rows, :] = y.astype(BF16)
                continue
            for c0 in range(0, D, n_e):
                cols = slice(c0, c0 + n_e)
                yc = y[:, cols]
                ss = _dot((yc * yc).astype(BF16), e_ref[...])
                rs = lax.rsqrt(ss * (1.0 / HEAD_DIM) + EPS) * gain_ref[:, cols]
                o_ref[rows, cols] = (yc * rs).astype(BF16)

    j = pl.program_id(1)
    pl.when(j == 0)(functools.partial(step, True, True))
    pl.when(j == 1)(functools.partial(step, False, True))
    pl.when(j == 2)(functools.partial(step, False, False))


def _softmax_pv(s_list, v_list):
    m = s_list[0].max(axis=-1, keepdims=True)
    for s in s_list[1:]:
        m = jnp.maximum(m, s.max(axis=-1, keepdims=True))
    den = 0.0
    acc = 0.0
    for s, v in zip(s_list, v_list):
        p = jnp.exp(s - m)
        den = den + p.sum(axis=-1, keepdims=True)
        acc = acc + _dot(p.astype(BF16), v)
    return acc / den


def _qk(q, k):
    return lax.dot_general(q, k, (((1,), (1,)), ((), ())), preferred_element_type=F32)


def _natten_kernel(q_ref, k_ref, v_ref, kc_ref, vc_ref, bias_ref, o_ref, *, rows, kr):
    r0 = pl.program_id(1) * NATTEN_ROWS
    left = lax.broadcasted_iota(jnp.int32, (GRID_W, LANE), 1) < HEAD_DIM
    n2 = 2 * GRID_W
    wins, dr0s = [], []
    for i in range(NATTEN_ROWS):
        rs = jnp.clip(r0 + i - kr // 2, 0, rows - kr)
        wins.append(pl.ds(pl.multiple_of(rs * GRID_W, GRID_W), kr * GRID_W))
        dr0s.append(rs - (r0 + i) + WIN_R - 1)
    for p in range(N_HEADS // 2):
        cols = slice(p * LANE, (p + 1) * LANE)
        q2 = []
        for i in range(NATTEN_ROWS):
            q = q_ref[i * GRID_W:(i + 1) * GRID_W, cols]
            zero = jnp.zeros_like(q)
            q2 += [jnp.where(left, q, zero), jnp.where(left, zero, q)]
        q2 = jnp.concatenate(q2, axis=0)
        s_ctx = _qk(q2, kc_ref[:, cols])
        m_ctx = s_ctx.max(axis=-1, keepdims=True)
        m, den, acc = [], [], []
        for i in range(NATTEN_ROWS):
            mine = slice(i * n2, (i + 1) * n2)
            bias = jnp.concatenate([bias_ref[p, dr0s[i] + 2 * t] for t in range(kr // 2)], axis=-1)
            s_lat = _qk(q2[mine], k_ref[wins[i], cols]) + bias
            m.append(jnp.maximum(s_lat.max(axis=-1, keepdims=True), m_ctx[mine]))
            p_lat = jnp.exp(s_lat - m[i])
            den.append(p_lat.sum(axis=-1, keepdims=True))
            acc.append(_dot(p_lat.astype(BF16), v_ref[wins[i], cols]))
        p_ctx = jnp.exp(s_ctx - jnp.concatenate(m, axis=0))
        den_ctx = p_ctx.sum(axis=-1, keepdims=True)
        acc_ctx = _dot(p_ctx.astype(BF16), vc_ref[:, cols])
        for i in range(NATTEN_ROWS):
            mine = slice(i * n2, (i + 1) * n2)
            o = (acc[i] + acc_ctx[mine]) / (den[i] + den_ctx[mine])
            o_ref[i * GRID_W:(i + 1) * GRID_W, cols] = jnp.where(left, o[:GRID_W], o[GRID_W:]).astype(BF16)


def _bias_kernel(rpb_ref, o_ref):
    qc = lax.broadcasted_iota(jnp.int32, (GRID_W, LANE), 0)
    lane = lax.broadcasted_iota(jnp.int32, (GRID_W, LANE), 1)
    kc = lane % GRID_W
    w0 = jnp.clip(qc - WIN_C // 2, 0, GRID_W - WIN_C)
    ok = jnp.logical_and(kc >= w0, kc < w0 + WIN_C)
    sub = lax.broadcasted_iota(jnp.int32, (SUBLANE, LANE), 0)
    tiles = []
    for dr in range(2 * WIN_R - 1):
        t = jnp.broadcast_to(rpb_ref[dr:dr + 1, :], (SUBLANE, LANE))
        t = pltpu.roll(t, LANE - (WIN_C - 1), axis=1)
        for bit in range(SUBLANE.bit_length() - 1):
            t = jnp.where((sub >> bit) & 1 == 1, pltpu.roll(t, 1 << bit, axis=1), t)
        tiles.append(jnp.concatenate([t] + [pltpu.roll(t, SUBLANE * a, axis=1)
                                            for a in range(1, GRID_W // SUBLANE)], axis=0))
    for p in range(2 * WIN_R - 2):
        pair = jnp.where(lane < GRID_W, tiles[p], pltpu.roll(tiles[p + 1], GRID_W, axis=1))
        o_ref[p] = jnp.where(ok, pair, NEG_INF)


def _ctx_attn_kernel(q_ref, k_ref, v_ref, o_ref):
    outs = []
    for h in range(N_HEADS):
        hd = slice(h * HEAD_DIM, (h + 1) * HEAD_DIM)
        outs.append(_softmax_pv([_qk(q_ref[:, hd], k_ref[:, hd])], [v_ref[:, hd]]))
    o_ref[...] = jnp.concatenate(outs, axis=-1).astype(BF16)


def _natten_bias(rpb):
    n_dr, n_dc = 2 * WIN_R - 1, 2 * WIN_C - 1
    assert rpb.shape == (N_HEADS, n_dr, n_dc) and GRID_W * 2 == LANE
    padded = jnp.pad(rpb.astype(F32), ((0, 0), (0, 0), (0, LANE - n_dc)))
    return pl.pallas_call(
        _bias_kernel,
        grid=(N_HEADS,),
        in_specs=[pl.BlockSpec((None, n_dr, LANE), lambda h: (h, 0, 0))],
        out_specs=pl.BlockSpec((None, n_dr - 1, GRID_W, LANE), lambda h: (h // 2, 0, h % 2, 0)),
        out_shape=jax.ShapeDtypeStruct((N_HEADS // 2, n_dr - 1, 2 * GRID_W, LANE), F32),
        compiler_params=_cparams(("arbitrary",)),
        name="natten_bias",
    )(padded)


def _natten_mixer(hs, mod_l, w_qkv, q_gain, k_gain, rpb, w_o, n_batch, seq, n_ctx):
    n_lat = n_batch * seq
    n_all = n_lat + n_batch * n_ctx
    rows = seq // GRID_W
    kr = min(WIN_R, rows)
    n_tiles = n_all // TM_FFN
    tpb = seq // TM_FFN
    scale = HEAD_DIM ** -0.5
    gain = jnp.concatenate([jnp.tile(q_gain, N_HEADS) * scale, jnp.tile(k_gain, N_HEADS),
                            jnp.ones((D,), F32)]).reshape(1, 3 * D)
    lane = np.arange(FF_TILE)
    e = jnp.asarray((lane[:, None] // HEAD_DIM == lane[None, :] // HEAD_DIM).astype(np.float32)).astype(BF16)
    qkv = pl.pallas_call(
        _qkv_kernel,
        grid=(n_tiles, 3),
        in_specs=[pl.BlockSpec((TM_FFN, D), lambda i, j: (i, 0)),
                  pl.BlockSpec((None, N_MOD, D), lambda i, j: (_mod_row(i, tpb, n_batch), 0, 0)),
                  pl.BlockSpec((D, D), lambda i, j: (0, j)),
                  pl.BlockSpec((1, D), lambda i, j: (0, j)),
                  pl.BlockSpec((FF_TILE, FF_TILE), lambda i, j: (0, 0))],
        out_specs=pl.BlockSpec((TM_FFN, D), lambda i, j: (i, j)),
        out_shape=jax.ShapeDtypeStruct((n_all, 3 * D), BF16),
        scratch_shapes=[pltpu.VMEM((TM_FFN, D), BF16)],
        compiler_params=_cparams(("arbitrary", "arbitrary")),
        name="qkv",
    )(hs, mod_l, w_qkv.astype(BF16), gain, e)

    assert kr == WIN_R, "bias tiles pair up the key rows of a full window"
    bias = _natten_bias(rpb)
    ctx0 = n_lat // n_ctx
    once = pl.Buffered(1)
    n_rb = rows // NATTEN_ROWS
    o_lat = pl.pallas_call(
        functools.partial(_natten_kernel, rows=rows, kr=kr),
        grid=(n_batch, n_rb),
        in_specs=[pl.BlockSpec((NATTEN_ROWS * GRID_W, D), lambda b, r: (b * n_rb + r, 0)),
                  pl.BlockSpec((seq, D), lambda b, r: (b, 1)),
                  pl.BlockSpec((seq, D), lambda b, r: (b, 2), pipeline_mode=once),
                  pl.BlockSpec((n_ctx, D), lambda b, r: (ctx0 + b, 1)),
                  pl.BlockSpec((n_ctx, D), lambda b, r: (ctx0 + b, 2)),
                  pl.BlockSpec(bias.shape, lambda b, r: (0, 0, 0, 0), pipeline_mode=once)],
        out_specs=pl.BlockSpec((NATTEN_ROWS * GRID_W, D), lambda b, r: (b * n_rb + r, 0)),
        out_shape=jax.ShapeDtypeStruct((n_lat, D), BF16),
        compiler_params=_cparams(("arbitrary", "arbitrary")),
        name="natten",
    )(qkv, qkv, qkv, qkv, qkv, bias)
    o_ctx = pl.pallas_call(
        _ctx_attn_kernel,
        grid=(n_batch,),
        in_specs=[pl.BlockSpec((n_ctx, D), lambda b: (ctx0 + b, 0)),
                  pl.BlockSpec((n_ctx, D), lambda b: (ctx0 + b, 1)),
                  pl.BlockSpec((n_ctx, D), lambda b: (ctx0 + b, 2))],
        out_specs=pl.BlockSpec((n_ctx, D), lambda b: (b, 0)),
        out_shape=jax.ShapeDtypeStruct((n_batch * n_ctx, D), BF16),
        compiler_params=_cparams(("arbitrary",)),
        name="ctx_attn",
    )(qkv, qkv, qkv)
    w = w_o.astype(BF16)
    zero_b = jnp.zeros((D,), F32)
    row_fn = lambda t: _mod_row(t, seq // TM_LIN, n_batch)
    hs = _lin_res_call(hs, (o_lat,), (), w, zero_b, mod_l, 0, n_lat // TM_LIN, row_fn)
    return _lin_res_call(hs, (o_ctx,), (), w, zero_b, mod_l, n_lat // TM_LIN, n_batch * n_ctx // TM_LIN, row_fn)


def kernel(x, c, ctx, c_ctx, w_mod, b_mod, w_ff_in, w_ff_out, fnet_w_out, fnet_b_out, conv_w_in, conv_b_in, conv_w_dw, conv_b_dw, conv_ln_g, conv_ln_b, conv_w_out, conv_b_out, na_w_qkv, na_q_gain, na_k_gain, na_rpb, na_w_o):
    n_batch, seq, d = x.shape
    n_ctx = ctx.shape[1]
    depth = w_mod.shape[0]
    assert d == D and n_batch < MOD_ROWS
    assert seq % TM_FFN == 0 and (n_batch * n_ctx) % TM_FFN == 0
    n_lat = n_batch * seq

    cond = jnp.concatenate([c, c_ctx[None, :], jnp.zeros((MOD_ROWS - n_batch - 1, D), F32)], axis=0)
    mod = _mod_call(cond, w_mod, b_mod).reshape(depth, MOD_ROWS, N_MOD, D)
    n_tok = n_lat + n_batch * n_ctx
    n_stream = n_tok + (-n_tok % seq)
    assert depth > 1, "the first half-step is written for a layer that updates the context tokens"
    hs = (x.reshape(n_lat, D), ctx.reshape(n_batch * n_ctx, D))
    lat_tiles = n_lat // TM_FFN
    all_tiles = n_tok // TM_FFN
    w_gu, w_o = w_ff_in, w_ff_out

    for i in range(depth):
        kind, slot = i % 3, i // 3
        ctx_post = i < depth - 1
        ctx_pre = ctx_post or kind == 2
        mod_l = mod[i]
        hs = _ffn_call(hs, mod_l, w_gu, w_o, i, 0, 0, all_tiles if ctx_pre else lat_tiles,
                       n_batch, seq, i > 0, n_stream)
        if kind == 0:
            hs = _fourier_mixer(hs, mod_l, fnet_w_out[slot], fnet_b_out[slot], n_batch, seq, n_ctx, ctx_post)
        elif kind == 1:
            assert ctx_post
            hs = _conv_mixer(hs, mod_l, conv_w_in[slot], conv_b_in[slot], conv_w_dw[slot], conv_b_dw[slot],
                             conv_ln_g[slot], conv_ln_b[slot], conv_w_out[slot], conv_b_out[slot],
                             n_batch, seq, n_ctx)
        else:
            assert ctx_post
            hs = _natten_mixer(hs, mod_l, na_w_qkv[slot], na_q_gain[slot], na_k_gain[slot], na_rpb[slot],
                               na_w_o[slot], n_batch, seq, n_ctx)
        final = i == depth - 1
        hs = _ffn_call(hs, mod_l, w_gu, w_o, i, 1, 6, all_tiles if ctx_post else lat_tiles,
                       n_batch, seq, not final)
    return hs[:n_lat].reshape(n_batch, seq, D)
```

```python
import functools

import numpy as np
import jax
import jax.numpy as jnp
from jax import lax
from jax.experimental import pallas as pl
from jax.experimental.pallas import tpu as pltpu

F32 = jnp.float32
BF16 = jnp.bfloat16

D = 1024
D_FF = 2816
N_MOD = 9
EPS = 1e-6
GRID_W = 64
LANE = 128
SUBLANE = 8
N_GROUPS = 4
GROUP_C = D // N_GROUPS
CONV_K = 31
CONV_HALO = 16
N_HEADS = 16
HEAD_DIM = D // N_HEADS
WIN_R = 8
WIN_C = 16
NEG_INF = -1e30
MOD_ROWS = 8

FF_TILE = 256
TM_FFN = 1024
PROJ_ROWS = 256
GLU_COLS = 1024
FFN_ROWS = 256
TM_LIN = 512
TM_CONV = 256
CONV_ROWS = 32
COL_TILE = 16
NATTEN_ROWS = 4
VMEM_LIMIT = 52 * 1024 * 1024
VMEM_LIMIT_WIDE = 58 * 1024 * 1024
TM_FFN_WIDE = 2048


def _cparams(sem):
    return pltpu.CompilerParams(dimension_semantics=sem, vmem_limit_bytes=VMEM_LIMIT)


def _silu(x):
    return x / (1.0 + jnp.exp(-x))


def _norm_mod(x, shift, scale):
    ms = jnp.mean(x * x, axis=-1, keepdims=True)
    return x * lax.rsqrt(ms + EPS) * (1.0 + scale) + shift


def _dot(a, b):
    return jnp.dot(a, b, preferred_element_type=F32)


def _mod_kernel(c_ref, w_ref, b_ref, o_ref):
    s = _silu(c_ref[...]).astype(BF16)
    o_ref[...] = _dot(s, w_ref[...].astype(BF16)) + b_ref[...]


def _mod_call(cond, w_mod, b_mod):
    depth = w_mod.shape[0]
    return pl.pallas_call(
        _mod_kernel,
        grid=(depth, N_MOD),
        in_specs=[
            pl.BlockSpec((MOD_ROWS, D), lambda l, k: (0, 0)),
            pl.BlockSpec((None, D, D), lambda l, k: (l, 0, k)),
            pl.BlockSpec((None, 1, D), lambda l, k: (l, 0, k)),
        ],
        out_specs=pl.BlockSpec((None, MOD_ROWS, D), lambda l, k: (l, 0, k)),
        out_shape=jax.ShapeDtypeStruct((depth, MOD_ROWS, N_MOD * D), F32),
        compiler_params=_cparams(("arbitrary", "arbitrary")),
        name="mod",
    )(cond, w_mod, b_mod.reshape(depth, 1, N_MOD * D))


def _mod_row(tile, tiles_per_batch, n_batch):
    return jnp.minimum(tile // tiles_per_batch, n_batch)


def _ffn_kernel(*refs, k0, n_ff, lat_tiles):
    if lat_tiles is None:
        h_ref, mod_ref, wg_ref, wu_ref, wo_ref, o_ref, xn_ref, acc_ref = refs
        tokens = lambda rows: h_ref[rows, :]
    else:
        lat_ref, ctx_ref, mod_ref, wg_ref, wu_ref, wo_ref, o_ref, xn_ref, acc_ref = refs
        is_lat = pl.program_id(0) < lat_tiles
        tokens = lambda rows: jnp.where(is_lat, lat_ref[rows, :], ctx_ref[rows, :])
    j = pl.program_id(1)
    n_chunks = o_ref.shape[0] // FFN_ROWS
    assert n_ff >= 2

    def step(first, last):
        wg, wu, wo = wg_ref[...].astype(BF16), wu_ref[...].astype(BF16), wo_ref[...].astype(BF16)
        gu = {}
        for c in range(n_chunks + 1):
            if c < n_chunks:
                rows = slice(c * FFN_ROWS, (c + 1) * FFN_ROWS)
                if first:
                    xn = _norm_mod(tokens(rows), mod_ref[k0:k0 + 1, :], mod_ref[k0 + 1:k0 + 2, :]).astype(BF16)
                    xn_ref[rows, :] = xn
                else:
                    xn = xn_ref[rows, :]
                gu[c] = (_dot(xn, wg), _dot(xn, wu))
            if c >= 1:
                rows = slice((c - 1) * FFN_ROWS, c * FFN_ROWS)
                g, u = gu.pop(c - 1)
                y = _dot((_silu(g) * u).astype(BF16), wo)
                if first:
                    acc_ref[rows, :] = y
                elif last:
                    o_ref[rows, :] = tokens(rows) + (0.5 * mod_ref[k0 + 2:k0 + 3, :]) * (acc_ref[rows, :] + y)
                else:
                    acc_ref[rows, :] += y

    real = True if lat_tiles is None else pl.program_id(0) <= lat_tiles
    pl.when(jnp.logical_and(real, j == 0))(functools.partial(step, True, False))
    pl.when(jnp.logical_and(real, jnp.logical_and(j > 0, j < n_ff - 1)))(functools.partial(step, False, False))
    pl.when(jnp.logical_and(real, j == n_ff - 1))(functools.partial(step, False, True))
    if lat_tiles is not None:
        @pl.when(jnp.logical_and(jnp.logical_not(real), j == n_ff - 1))
        def _():
            o_ref[...] = jnp.zeros_like(o_ref)


def _ffn_call(hs, mod_l, w_gu, w_o, layer, which, k0, n_tiles, n_batch, seq, alias, out_rows=None,
              tm=TM_FFN, tile0=0):
    n_ff = D_FF // FF_TILE
    tpb = seq // tm
    ff = lambda i, j: j
    if isinstance(hs, tuple):
        assert tm == TM_FFN and tile0 == 0
        lat, ctx = hs
        lat_tiles = lat.shape[0] // TM_FFN
        assert not alias and ctx.shape[0] == TM_FFN and n_tiles == lat_tiles + 1
        tokens = (lat, ctx)
        token_specs = [pl.BlockSpec((TM_FFN, D), lambda i, j: (jnp.minimum(i, lat_tiles - 1), 0)),
                       pl.BlockSpec((TM_FFN, D), lambda i, j: (0, 0))]
        n_real = n_tiles
        ff = lambda i, j: jnp.where(i < n_real, j, n_ff - 1)
        n_tiles = out_rows // TM_FFN
    else:
        lat_tiles = None
        tokens = (hs,)
        token_specs = [pl.BlockSpec((tm, D), lambda i, j: (i + tile0, 0))]
        out_rows = hs.shape[0] if alias else n_tiles * tm
        assert alias or tile0 == 0
    return pl.pallas_call(
        functools.partial(_ffn_kernel, k0=k0, n_ff=n_ff, lat_tiles=lat_tiles),
        grid=(n_tiles, n_ff),
        in_specs=token_specs + [
            pl.BlockSpec((None, N_MOD, D), lambda i, j: (_mod_row(i + tile0, tpb, n_batch), 0, 0)),
            pl.BlockSpec((None, None, D, FF_TILE), lambda i, j: (layer, which, 0, ff(i, j))),
            pl.BlockSpec((None, None, D, FF_TILE), lambda i, j: (layer, which, 0, n_ff + ff(i, j))),
            pl.BlockSpec((None, None, FF_TILE, D), lambda i, j: (layer, which, ff(i, j), 0)),
        ],
        out_specs=pl.BlockSpec((tm, D), lambda i, j: (i + tile0, 0)),
        out_shape=jax.ShapeDtypeStruct((out_rows, D), F32),
        scratch_shapes=[pltpu.VMEM((tm, D), BF16), pltpu.VMEM((tm, D), F32)],
        input_output_aliases={0: 0} if alias else {},
        compiler_params=pltpu.CompilerParams(dimension_semantics=("arbitrary", "arbitrary"),
                                             vmem_limit_bytes=VMEM_LIMIT_WIDE if tm > TM_FFN else VMEM_LIMIT),
        name="ffn",
    )(*tokens, mod_l, w_gu, w_gu, w_o)


def _lin_res_kernel(*refs, fourier):
    if fourier:
        xr_ref, xi_ref, cc_ref, sc_ref, w_ref, b_ref, h_ref, mod_ref, o_ref = refs
        parts = []
        for g in range(N_GROUPS):
            sl = slice(g * GROUP_C, (g + 1) * GROUP_C)
            parts.append(_dot(xr_ref[:, sl], cc_ref[...]) + _dot(xi_ref[:, sl], sc_ref[...]))
        x = jnp.concatenate(parts, axis=-1).astype(BF16)
    else:
        x_ref, w_ref, b_ref, h_ref, mod_ref, o_ref = refs
        x = x_ref[...]
    y = _dot(x, w_ref[...]) + b_ref[...]
    o_ref[...] = h_ref[...] + mod_ref[5:6, :] * y


def _lin_res_call(hs, xs, tables, w, b, mod_l, tile0, n_tiles, mod_row_fn):
    fourier = len(xs) == 2
    x_spec = pl.BlockSpec((TM_LIN, D), lambda i: (i, 0))
    t_spec = pl.BlockSpec((GROUP_C, GROUP_C), lambda i: (0, 0))
    in_specs = [x_spec] * len(xs) + [t_spec] * len(tables) + [
        pl.BlockSpec((D, D), lambda i: (0, 0)),
        pl.BlockSpec((1, D), lambda i: (0, 0)),
        pl.BlockSpec((TM_LIN, D), lambda i: (i + tile0, 0)),
        pl.BlockSpec((None, N_MOD, D), lambda i: (mod_row_fn(i + tile0), 0, 0)),
    ]
    h_index = len(xs) + len(tables) + 2
    return pl.pallas_call(
        functools.partial(_lin_res_kernel, fourier=fourier),
        grid=(n_tiles,),
        in_specs=in_specs,
        out_specs=pl.BlockSpec((TM_LIN, D), lambda i: (i + tile0, 0)),
        out_shape=jax.ShapeDtypeStruct(hs.shape, F32),
        input_output_aliases={h_index: 0},
        compiler_params=_cparams(("arbitrary",)),
        name="lin_res_fourier" if fourier else "lin_res",
    )(*xs, *tables, w, b.reshape(1, D), hs, mod_l)


def _fnet_rows_kernel(h_ref, mod_ref, kf_ref, zr_ref, zi_ref):
    rows = h_ref.shape[0]
    half = rows // 2
    xn = _norm_mod(h_ref[...], mod_ref[3:4, :], mod_ref[4:5, :])
    n = rows * SUBLANE
    n_cos = (half + 1) * SUBLANE
    zr, zi = [], []
    for c0 in range(0, COL_TILE, SUBLANE):
        a = xn[:, c0:c0 + SUBLANE, :].reshape(n, D).astype(BF16)
        z = _dot(kf_ref[...], a)
        zc = z[:n_cos].reshape(half + 1, SUBLANE, D)
        zs = z[n_cos:].reshape(half - 1, SUBLANE, D)
        zero = jnp.zeros((1, SUBLANE, D), F32)
        zr.append(jnp.concatenate([zc] + [zc[rows - k:rows - k + 1] for k in range(half + 1, rows)], axis=0))
        zi.append(jnp.concatenate([zero, zs, zero] + [-zs[rows - k - 1:rows - k] for k in range(half + 1, rows)],
                                  axis=0))
    zr_ref[...] = jnp.concatenate(zr, axis=1).astype(BF16)
    zi_ref[...] = jnp.concatenate(zi, axis=1).astype(BF16)


def _fnet_cols_kernel(zr_ref, zi_ref, m2_ref, cc_ref, sc_ref, w_ref, b_ref, h_ref, mod_ref, o_ref,
                      xr_ref, xi_ref, y_ref):
    n = zr_ref.shape[1]
    for j in range(SUBLANE):
        z = jnp.concatenate([zr_ref[j], zi_ref[j]], axis=0)
        x = _dot(m2_ref[j], z)
        xr_ref[j * n:(j + 1) * n, :] = x[:n].astype(BF16)
        xi_ref[j * n:(j + 1) * n, :] = x[n:].astype(BF16)
    parts = []
    for g in range(N_GROUPS):
        sl = slice(g * GROUP_C, (g + 1) * GROUP_C)
        parts.append(_dot(xr_ref[:, sl], cc_ref[...]) + _dot(xi_ref[:, sl], sc_ref[...]))
    y = _dot(jnp.concatenate(parts, axis=-1).astype(BF16), w_ref[...]) + b_ref[...]
    for j in range(SUBLANE):
        rows = pl.ds(j, n, stride=SUBLANE)
        for lt in range(D // LANE):
            y_ref[lt, rows, :] = y[j * n:(j + 1) * n, lt * LANE:(lt + 1) * LANE]
    y_t = jnp.concatenate([y_ref[lt] for lt in range(D // LANE)], axis=-1).reshape(n, SUBLANE, D)
    o_ref[...] = h_ref[...] + mod_ref[5:6, :] * y_t


def _fnet_ctx_kernel(h_ref, mod_ref, f_ref, xr_ref, xi_ref):
    n = h_ref.shape[0]
    xn = _norm_mod(h_ref[...], mod_ref[3:4, :], mod_ref[4:5, :]).astype(BF16)
    x = _dot(f_ref[...], xn)
    xr_ref[...] = x[:n].astype(BF16)
    xi_ref[...] = x[n:].astype(BF16)


def _dft_tables(rows):
    n_seq = rows * GRID_W
    a = np.arange(rows)
    th1 = 2.0 * np.pi * np.outer(a, a) / rows
    half = rows // 2
    f1 = np.concatenate([np.cos(th1)[:half + 1], -np.sin(th1)[1:half]], axis=0)
    f1 = np.kron(f1, np.eye(SUBLANE))
    k2 = np.arange(GRID_W)[:, None, None]
    k1 = np.arange(rows)[None, :, None]
    n1 = np.arange(GRID_W)[None, None, :]
    th2 = 2.0 * np.pi * (n1 * (GRID_W * k1 + k2) % n_seq) / n_seq
    c2, s2 = np.cos(th2), np.sin(th2)
    m2 = np.concatenate([np.concatenate([c2, s2], axis=2), np.concatenate([-s2, c2], axis=2)], axis=1)
    return f1.astype(np.float32), m2.astype(np.float32)


def _channel_tables(n_seq):
    c = np.arange(GROUP_C)
    th = 2.0 * np.pi * np.outer(c, c) / GROUP_C
    s = 1.0 / np.sqrt(float(n_seq) * GROUP_C)
    return (np.cos(th) * s).astype(np.float32), (np.sin(th) * s).astype(np.float32)


def _fourier_mixer(hs, mod_l, w_out, b_out, n_batch, seq, n_ctx, with_ctx):
    rows = seq // GRID_W
    assert rows == GRID_W, "the two-stage DFT is written for a square token grid"
    n_lat = n_batch * seq
    f1, m2 = _dft_tables(rows)
    cc, sc = _channel_tables(seq)
    f1, m2, cc, sc = (jnp.asarray(t).astype(BF16) for t in (f1, m2, cc, sc))
    w = w_out.astype(BF16)
    n_ct = GRID_W // COL_TILE
    grid3 = hs.reshape(hs.shape[0] // GRID_W, GRID_W, D)
    z_shape = jax.ShapeDtypeStruct((n_batch * rows, GRID_W, D), BF16)
    blk_cols = pl.BlockSpec((rows, COL_TILE, D), lambda b, c: (b, c, 0))
    mod_b = pl.BlockSpec((None, N_MOD, D), lambda b, c: (b, 0, 0))
    zr, zi = pl.pallas_call(
        _fnet_rows_kernel,
        grid=(n_batch, n_ct),
        in_specs=[blk_cols, mod_b, pl.BlockSpec(f1.shape, lambda b, c: (0, 0))],
        out_specs=[blk_cols, blk_cols],
        out_shape=[z_shape, z_shape],
        compiler_params=_cparams(("arbitrary", "arbitrary")),
        name="fnet_rows",
    )(grid3, mod_l, f1)
    n_kt = GRID_W // SUBLANE
    blk_k2 = pl.BlockSpec((SUBLANE, GRID_W, D), lambda b, k: (b * n_kt + k, 0, 0))
    blk_tok = pl.BlockSpec((rows, SUBLANE, D), lambda b, k: (b, k, 0))
    table = pl.BlockSpec((GROUP_C, GROUP_C), lambda b, k: (0, 0))
    n_tok = rows * SUBLANE
    grid3 = pl.pallas_call(
        _fnet_cols_kernel,
        grid=(n_batch, n_kt),
        in_specs=[blk_k2, blk_k2,
                  pl.BlockSpec((SUBLANE, 2 * rows, 2 * GRID_W), lambda b, k: (k, 0, 0)),
                  table, table,
                  pl.BlockSpec((D, D), lambda b, k: (0, 0)),
                  pl.BlockSpec((1, D), lambda b, k: (0, 0)),
                  blk_tok, mod_b],
        out_specs=blk_tok,
        out_shape=jax.ShapeDtypeStruct(grid3.shape, F32),
        scratch_shapes=[pltpu.VMEM((n_tok, D), BF16), pltpu.VMEM((n_tok, D), BF16),
                        pltpu.VMEM((D // LANE, n_tok, LANE), F32)],
        input_output_aliases={7: 0},
        compiler_params=_cparams(("arbitrary", "arbitrary")),
        name="fnet_cols",
    )(zr, zi, m2, cc, sc, w, b_out.reshape(1, D), grid3, mod_l)
    hs = grid3.reshape(hs.shape)
    tpb = seq // TM_LIN
    row_fn = lambda t: _mod_row(t, tpb, n_batch)
    if with_ctx:
        a = np.arange(n_ctx)
        th = 2.0 * np.pi * np.outer(a, a) / n_ctx
        s = np.sqrt(float(seq) / n_ctx)
        fc = jnp.asarray(np.concatenate([np.cos(th) * s, -np.sin(th) * s], axis=0).astype(np.float32)).astype(BF16)
        c_shape = jax.ShapeDtypeStruct((n_batch * n_ctx, D), BF16)
        tile0 = n_lat // n_ctx
        xr, xi = pl.pallas_call(
            _fnet_ctx_kernel,
            grid=(n_batch,),
            in_specs=[pl.BlockSpec((n_ctx, D), lambda b: (tile0 + b, 0)),
                      pl.BlockSpec((None, N_MOD, D), lambda b: (n_batch, 0, 0)),
                      pl.BlockSpec((2 * n_ctx, n_ctx), lambda b: (0, 0))],
            out_specs=[pl.BlockSpec((n_ctx, D), lambda b: (b, 0))] * 2,
            out_shape=[c_shape, c_shape],
            compiler_params=_cparams(("arbitrary",)),
            name="fnet_ctx",
        )(hs, mod_l, fc)
        hs = _lin_res_call(hs, (xr, xi), (cc, sc), w, b_out, mod_l,
                           n_lat // TM_LIN, n_batch * n_ctx // TM_LIN, row_fn)
    return hs


def _mixer_input_chunks(h_ref, mod_ref, xn_ref, first):
    for r0 in range(0, h_ref.shape[0], PROJ_ROWS):
        rows = slice(r0, r0 + PROJ_ROWS)
        if first:
            xn = _norm_mod(h_ref[rows, :], mod_ref[3:4, :], mod_ref[4:5, :]).astype(BF16)
            xn_ref[rows, :] = xn
        else:
            xn = xn_ref[rows, :]
        yield rows, xn


def _glu_kernel(h_ref, mod_ref, wa_ref, wg_ref, ba_ref, bg_ref, o_ref, xn_ref):
    def step(first):
        for rows, xn in _mixer_input_chunks(h_ref, mod_ref, xn_ref, first):
            a = _dot(xn, wa_ref[...]) + ba_ref[...]
            g = _dot(xn, wg_ref[...]) + bg_ref[...]
            o_ref[rows, :] = a / (1.0 + jnp.exp(-g))

    j = pl.program_id(1)
    pl.when(j == 0)(functools.partial(step, True))
    pl.when(j > 0)(functools.partial(step, False))


def _conv_kernel(u_ref, up_ref, un_ref, wdw_ref, bdw_ref, lng_ref, lnb_ref, w_ref, b_ref, h_ref, mod_ref,
                 o_ref, buf_ref, conv_ref, v_ref, *, lat_tiles, tiles_per_seq):
    i = pl.program_id(0)
    is_ctx = i >= lat_tiles
    first = jnp.logical_or(is_ctx, i % tiles_per_seq == 0)
    last = jnp.logical_or(is_ctx, i % tiles_per_seq == tiles_per_seq - 1)
    tm = u_ref.shape[0]
    n_buf = tm + 2 * CONV_HALO
    buf_ref[0, 0:CONV_HALO, :] = jnp.where(first, 0.0, up_ref[...])
    buf_ref[0, CONV_HALO:CONV_HALO + tm, :] = u_ref[...]
    buf_ref[0, CONV_HALO + tm:, :] = jnp.where(last, 0.0, un_ref[...])
    for s in range(1, SUBLANE):
        buf_ref[s] = pltpu.roll(buf_ref[0], n_buf - s, axis=0)
    off = CONV_HALO - CONV_K // 2
    for lt in range(D // LANE):
        lanes = slice(lt * LANE, (lt + 1) * LANE)
        for c0 in range(0, tm, CONV_ROWS):
            acc = jnp.zeros((CONV_ROWS, LANE), F32) + bdw_ref[:, lanes]
            for k in range(CONV_K):
                row = c0 + (off + k) // SUBLANE * SUBLANE
                acc = acc + wdw_ref[k:k + 1, lanes] * buf_ref[(off + k) % SUBLANE, row:row + CONV_ROWS, lanes]
            conv_ref[c0:c0 + CONV_ROWS, lanes] = acc
    chunk = 32
    for c0 in range(0, tm, chunk):
        acc = conv_ref[c0:c0 + chunk, :]
        mu = jnp.mean(acc, axis=-1, keepdims=True)
        d = acc - mu
        var = jnp.mean(d * d, axis=-1, keepdims=True)
        v = d * lax.rsqrt(var + EPS) * lng_ref[...] + lnb_ref[...]
        v_ref[c0:c0 + chunk, :] = _silu(v).astype(BF16)
    y = _dot(v_ref[...], w_ref[...]) + b_ref[...]
    o_ref[...] = h_ref[...] + mod_ref[5:6, :] * y


def _conv_mixer(hs, mod_l, w_in, b_in, w_dw, b_dw, ln_g, ln_b, w_out, b_out, n_batch, seq, n_ctx):
    n_lat = n_batch * seq
    n_all = n_lat + n_batch * n_ctx
    assert n_ctx == TM_CONV, "context sequences are one conv tile long"
    n_tiles = n_all // TM_FFN
    n_col = D // GLU_COLS
    tpb = seq // TM_FFN
    col = pl.BlockSpec((D, GLU_COLS), lambda i, j: (0, j))
    gcol = pl.BlockSpec((D, GLU_COLS), lambda i, j: (0, n_col + j))
    u = pl.pallas_call(
        _glu_kernel,
        grid=(n_tiles, n_col),
        in_specs=[pl.BlockSpec((TM_FFN, D), lambda i, j: (i, 0)),
                  pl.BlockSpec((None, N_MOD, D), lambda i, j: (_mod_row(i, tpb, n_batch), 0, 0)),
                  col, gcol,
                  pl.BlockSpec((1, GLU_COLS), lambda i, j: (0, j)),
                  pl.BlockSpec((1, GLU_COLS), lambda i, j: (0, n_col + j))],
        out_specs=pl.BlockSpec((TM_FFN, GLU_COLS), lambda i, j: (i, j)),
        out_shape=jax.ShapeDtypeStruct((n_all, D), F32),
        scratch_shapes=[pltpu.VMEM((TM_FFN, D), BF16)],
        compiler_params=_cparams(("arbitrary", "arbitrary")),
        name="conv_glu",
    )(hs, mod_l, w_in.astype(BF16), w_in.astype(BF16), b_in.reshape(1, 2 * D), b_in.reshape(1, 2 * D))

    n_ct = n_all // TM_CONV
    hpt = TM_CONV // CONV_HALO
    n_halo = n_all // CONV_HALO
    tps = seq // TM_CONV
    vec = pl.BlockSpec((1, D), lambda i: (0, 0))
    tile = pl.BlockSpec((TM_CONV, D), lambda i: (i, 0))
    return pl.pallas_call(
        functools.partial(_conv_kernel, lat_tiles=n_lat // TM_CONV, tiles_per_seq=tps),
        grid=(n_ct,),
        in_specs=[tile,
                  pl.BlockSpec((CONV_HALO, D), lambda i: (jnp.maximum(i * hpt - 1, 0), 0)),
                  pl.BlockSpec((CONV_HALO, D), lambda i: (jnp.minimum((i + 1) * hpt, n_halo - 1), 0)),
                  pl.BlockSpec((CONV_K, D), lambda i: (0, 0)),
                  vec, vec, vec,
                  pl.BlockSpec((D, D), lambda i: (0, 0)),
                  vec, tile,
                  pl.BlockSpec((None, N_MOD, D), lambda i: (_mod_row(i, tps, n_batch), 0, 0))],
        out_specs=tile,
        out_shape=jax.ShapeDtypeStruct(hs.shape, F32),
        scratch_shapes=[pltpu.VMEM((SUBLANE, TM_CONV + 2 * CONV_HALO, D), F32), pltpu.VMEM((TM_CONV, D), F32),
                        pltpu.VMEM((TM_CONV, D), BF16)],
        input_output_aliases={9: 0},
        compiler_params=_cparams(("arbitrary",)),
        name="conv_dw",
    )(u, u, u, w_dw, b_dw.reshape(1, D), ln_g.reshape(1, D), ln_b.reshape(1, D), w_out.astype(BF16),
      b_out.reshape(1, D), hs, mod_l)


def _qkv_kernel(h_ref, mod_ref, w_ref, gain_ref, e_ref, o_ref, xn_ref):
    n_e = e_ref.shape[0]

    def step(first, head_norm):
        for rows, xn in _mixer_input_chunks(h_ref, mod_ref, xn_ref, first):
            y = _dot(xn, w_ref[...])
            if not head_norm:
                o_ref[rows, :] = y.astype(BF16)
                continue
            for c0 in range(0, D, n_e):
                cols = slice(c0, c0 + n_e)
                yc = y[:, cols]
                ss = _dot((yc * yc).astype(BF16), e_ref[...])
                rs = lax.rsqrt(ss * (1.0 / HEAD_DIM) + EPS) * gain_ref[:, cols]
                o_ref[rows, cols] = (yc * rs).astype(BF16)

    j = pl.program_id(1)
    pl.when(j == 0)(functools.partial(step, True, True))
    pl.when(j == 1)(functools.partial(step, False, True))
    pl.when(j == 2)(functools.partial(step, False, False))


def _softmax_pv(s_list, v_list):
    m = s_list[0].max(axis=-1, keepdims=True)
    for s in s_list[1:]:
        m = jnp.maximum(m, s.max(axis=-1, keepdims=True))
    den = 0.0
    acc = 0.0
    for s, v in zip(s_list, v_list):
        p = jnp.exp(s - m)
        den = den + p.sum(axis=-1, keepdims=True)
        acc = acc + _dot(p.astype(BF16), v)
    return acc / den


def _qk(q, k):
    return lax.dot_general(q, k, (((1,), (1,)), ((), ())), preferred_element_type=F32)


def _natten_kernel(q_ref, k_ref, v_ref, kc_ref, vc_ref, bias_ref, o_ref, *, rows, kr):
    r0 = pl.program_id(1) * NATTEN_ROWS
    left = lax.broadcasted_iota(jnp.int32, (GRID_W, LANE), 1) < HEAD_DIM
    n2 = 2 * GRID_W
    wins, dr0s = [], []
    for i in range(NATTEN_ROWS):
        rs = jnp.clip(r0 + i - kr // 2, 0, rows - kr)
        wins.append(pl.ds(pl.multiple_of(rs * GRID_W, GRID_W), kr * GRID_W))
        dr0s.append(rs - (r0 + i) + WIN_R - 1)
    for p in range(N_HEADS // 2):
        cols = slice(p * LANE, (p + 1) * LANE)
        q2 = []
        for i in range(NATTEN_ROWS):
            q = q_ref[i * GRID_W:(i + 1) * GRID_W, cols]
            zero = jnp.zeros_like(q)
            q2 += [jnp.where(left, q, zero), jnp.where(left, zero, q)]
        q2 = jnp.concatenate(q2, axis=0)
        s_ctx = _qk(q2, kc_ref[:, cols])
        m_ctx = s_ctx.max(axis=-1, keepdims=True)
        m, den, acc = [], [], []
        for i in range(NATTEN_ROWS):
            mine = slice(i * n2, (i + 1) * n2)
            bias = jnp.concatenate([bias_ref[p, dr0s[i] + 2 * t] for t in range(kr // 2)], axis=-1)
            s_lat = _qk(q2[mine], k_ref[wins[i], cols]) + bias
            m.append(jnp.maximum(s_lat.max(axis=-1, keepdims=True), m_ctx[mine]))
            p_lat = jnp.exp(s_lat - m[i])
            den.append(p_lat.sum(axis=-1, keepdims=True))
            acc.append(_dot(p_lat.astype(BF16), v_ref[wins[i], cols]))
        p_ctx = jnp.exp(s_ctx - jnp.concatenate(m, axis=0))
        den_ctx = p_ctx.sum(axis=-1, keepdims=True)
        acc_ctx = _dot(p_ctx.astype(BF16), vc_ref[:, cols])
        for i in range(NATTEN_ROWS):
            mine = slice(i * n2, (i + 1) * n2)
            o = (acc[i] + acc_ctx[mine]) / (den[i] + den_ctx[mine])
            o_ref[i * GRID_W:(i + 1) * GRID_W, cols] = jnp.where(left, o[:GRID_W], o[GRID_W:]).astype(BF16)


def _bias_kernel(rpb_ref, o_ref):
    qc = lax.broadcasted_iota(jnp.int32, (GRID_W, LANE), 0)
    lane = lax.broadcasted_iota(jnp.int32, (GRID_W, LANE), 1)
    kc = lane % GRID_W
    w0 = jnp.clip(qc - WIN_C // 2, 0, GRID_W - WIN_C)
    ok = jnp.logical_and(kc >= w0, kc < w0 + WIN_C)
    sub = lax.broadcasted_iota(jnp.int32, (SUBLANE, LANE), 0)
    tiles = []
    for dr in range(2 * WIN_R - 1):
        t = jnp.broadcast_to(rpb_ref[dr:dr + 1, :], (SUBLANE, LANE))
        t = pltpu.roll(t, LANE - (WIN_C - 1), axis=1)
        for bit in range(SUBLANE.bit_length() - 1):
            t = jnp.where((sub >> bit) & 1 == 1, pltpu.roll(t, 1 << bit, axis=1), t)
        tiles.append(jnp.concatenate([t] + [pltpu.roll(t, SUBLANE * a, axis=1)
                                            for a in range(1, GRID_W // SUBLANE)], axis=0))
    for p in range(2 * WIN_R - 2):
        pair = jnp.where(lane < GRID_W, tiles[p], pltpu.roll(tiles[p + 1], GRID_W, axis=1))
        o_ref[p] = jnp.where(ok, pair, NEG_INF)


def _ctx_attn_kernel(q_ref, k_ref, v_ref, o_ref):
    outs = []
    for h in range(N_HEADS):
        hd = slice(h * HEAD_DIM, (h + 1) * HEAD_DIM)
        outs.append(_softmax_pv([_qk(q_ref[:, hd], k_ref[:, hd])], [v_ref[:, hd]]))
    o_ref[...] = jnp.concatenate(outs, axis=-1).astype(BF16)


def _natten_bias(rpb):
    n_dr, n_dc = 2 * WIN_R - 1, 2 * WIN_C - 1
    assert rpb.shape == (N_HEADS, n_dr, n_dc) and GRID_W * 2 == LANE
    padded = jnp.pad(rpb.astype(F32), ((0, 0), (0, 0), (0, LANE - n_dc)))
    return pl.pallas_call(
        _bias_kernel,
        grid=(N_HEADS,),
        in_specs=[pl.BlockSpec((None, n_dr, LANE), lambda h: (h, 0, 0))],
        out_specs=pl.BlockSpec((None, n_dr - 1, GRID_W, LANE), lambda h: (h // 2, 0, h % 2, 0)),
        out_shape=jax.ShapeDtypeStruct((N_HEADS // 2, n_dr - 1, 2 * GRID_W, LANE), F32),
        compiler_params=_cparams(("arbitrary",)),
        name="natten_bias",
    )(padded)


def _natten_mixer(hs, mod_l, w_qkv, q_gain, k_gain, rpb, w_o, n_batch, seq, n_ctx):
    n_lat = n_batch * seq
    n_all = n_lat + n_batch * n_ctx
    rows = seq // GRID_W
    kr = min(WIN_R, rows)
    n_tiles = n_all // TM_FFN
    tpb = seq // TM_FFN
    scale = HEAD_DIM ** -0.5
    gain = jnp.concatenate([jnp.tile(q_gain, N_HEADS) * scale, jnp.tile(k_gain, N_HEADS),
                            jnp.ones((D,), F32)]).reshape(1, 3 * D)
    lane = np.arange(FF_TILE)
    e = jnp.asarray((lane[:, None] // HEAD_DIM == lane[None, :] // HEAD_DIM).astype(np.float32)).astype(BF16)
    qkv = pl.pallas_call(
        _qkv_kernel,
        grid=(n_tiles, 3),
        in_specs=[pl.BlockSpec((TM_FFN, D), lambda i, j: (i, 0)),
                  pl.BlockSpec((None, N_MOD, D), lambda i, j: (_mod_row(i, tpb, n_batch), 0, 0)),
                  pl.BlockSpec((D, D), lambda i, j: (0, j)),
                  pl.BlockSpec((1, D), lambda i, j: (0, j)),
                  pl.BlockSpec((FF_TILE, FF_TILE), lambda i, j: (0, 0))],
        out_specs=pl.BlockSpec((TM_FFN, D), lambda i, j: (i, j)),
        out_shape=jax.ShapeDtypeStruct((n_all, 3 * D), BF16),
        scratch_shapes=[pltpu.VMEM((TM_FFN, D), BF16)],
        compiler_params=_cparams(("arbitrary", "arbitrary")),
        name="qkv",
    )(hs, mod_l, w_qkv.astype(BF16), gain, e)

    assert kr == WIN_R, "bias tiles pair up the key rows of a full window"
    bias = _natten_bias(rpb)
    ctx0 = n_lat // n_ctx
    once = pl.Buffered(1)
    n_rb = rows // NATTEN_ROWS
    o_lat = pl.pallas_call(
        functools.partial(_natten_kernel, rows=rows, kr=kr),
        grid=(n_batch, n_rb),
        in_specs=[pl.BlockSpec((NATTEN_ROWS * GRID_W, D), lambda b, r: (b * n_rb + r, 0)),
                  pl.BlockSpec((seq, D), lambda b, r: (b, 1)),
                  pl.BlockSpec((seq, D), lambda b, r: (b, 2), pipeline_mode=once),
                  pl.BlockSpec((n_ctx, D), lambda b, r: (ctx0 + b, 1)),
                  pl.BlockSpec((n_ctx, D), lambda b, r: (ctx0 + b, 2)),
                  pl.BlockSpec(bias.shape, lambda b, r: (0, 0, 0, 0), pipeline_mode=once)],
        out_specs=pl.BlockSpec((NATTEN_ROWS * GRID_W, D), lambda b, r: (b * n_rb + r, 0)),
        out_shape=jax.ShapeDtypeStruct((n_lat, D), BF16),
        compiler_params=_cparams(("arbitrary", "arbitrary")),
        name="natten",
    )(qkv, qkv, qkv, qkv, qkv, bias)
    o_ctx = pl.pallas_call(
        _ctx_attn_kernel,
        grid=(n_batch,),
        in_specs=[pl.BlockSpec((n_ctx, D), lambda b: (ctx0 + b, 0)),
                  pl.BlockSpec((n_ctx, D), lambda b: (ctx0 + b, 1)),
                  pl.BlockSpec((n_ctx, D), lambda b: (ctx0 + b, 2))],
        out_specs=pl.BlockSpec((n_ctx, D), lambda b: (b, 0)),
        out_shape=jax.ShapeDtypeStruct((n_batch * n_ctx, D), BF16),
        compiler_params=_cparams(("arbitrary",)),
        name="ctx_attn",
    )(qkv, qkv, qkv)
    w = w_o.astype(BF16)
    zero_b = jnp.zeros((D,), F32)
    row_fn = lambda t: _mod_row(t, seq // TM_LIN, n_batch)
    hs = _lin_res_call(hs, (o_lat,), (), w, zero_b, mod_l, 0, n_lat // TM_LIN, row_fn)
    return _lin_res_call(hs, (o_ctx,), (), w, zero_b, mod_l, n_lat // TM_LIN, n_batch * n_ctx // TM_LIN, row_fn)


def kernel(x, c, ctx, c_ctx, w_mod, b_mod, w_ff_in, w_ff_out, fnet_w_out, fnet_b_out, conv_w_in, conv_b_in, conv_w_dw, conv_b_dw, conv_ln_g, conv_ln_b, conv_w_out, conv_b_out, na_w_qkv, na_q_gain, na_k_gain, na_rpb, na_w_o):
    n_batch, seq, d = x.shape
    n_ctx = ctx.shape[1]
    depth = w_mod.shape[0]
    assert d == D and n_batch < MOD_ROWS
    assert seq % TM_FFN == 0 and (n_batch * n_ctx) % TM_FFN == 0
    n_lat = n_batch * seq

    cond = jnp.concatenate([c, c_ctx[None, :], jnp.zeros((MOD_ROWS - n_batch - 1, D), F32)], axis=0)
    mod = _mod_call(cond, w_mod, b_mod).reshape(depth, MOD_ROWS, N_MOD, D)
    n_tok = n_lat + n_batch * n_ctx
    n_stream = n_tok + (-n_tok % seq)
    assert depth > 1, "the first half-step is written for a layer that updates the context tokens"
    hs = (x.reshape(n_lat, D), ctx.reshape(n_batch * n_ctx, D))
    lat_tiles = n_lat // TM_FFN
    all_tiles = n_tok // TM_FFN
    w_gu, w_o = w_ff_in, w_ff_out

    def half_step(hs, layer, which, k0, with_ctx, in_place):
        hs = _ffn_call(hs, mod[layer], w_gu, w_o, layer, which, k0, n_lat // TM_FFN_WIDE, n_batch, seq, in_place,
                       tm=TM_FFN_WIDE)
        if with_ctx:
            hs = _ffn_call(hs, mod[layer], w_gu, w_o, layer, which, k0, all_tiles - lat_tiles, n_batch, seq, True,
                           tile0=lat_tiles)
        return hs

    for i in range(depth):
        kind, slot = i % 3, i // 3
        ctx_post = i < depth - 1
        ctx_pre = ctx_post or kind == 2
        mod_l = mod[i]
        if i == 0:
            hs = _ffn_call(hs, mod_l, w_gu, w_o, i, 0, 0, all_tiles, n_batch, seq, False, n_stream)
        else:
            hs = half_step(hs, i, 0, 0, ctx_pre, True)
        if kind == 0:
            hs = _fourier_mixer(hs, mod_l, fnet_w_out[slot], fnet_b_out[slot], n_batch, seq, n_ctx, ctx_post)
        elif kind == 1:
            assert ctx_post
            hs = _conv_mixer(hs, mod_l, conv_w_in[slot], conv_b_in[slot], conv_w_dw[slot], conv_b_dw[slot],
                             conv_ln_g[slot], conv_ln_b[slot], conv_w_out[slot], conv_b_out[slot],
                             n_batch, seq, n_ctx)
        else:
            assert ctx_post
            hs = _natten_mixer(hs, mod_l, na_w_qkv[slot], na_q_gain[slot], na_k_gain[slot], na_rpb[slot],
                               na_w_o[slot], n_batch, seq, n_ctx)
        final = i == depth - 1
        assert ctx_post != final
        hs = half_step(hs, i, 1, 6, ctx_post, not final)
    return hs[:n_lat].reshape(n_batch, seq, D)
```

```python
import functools

import numpy as np
import jax
import jax.numpy as jnp
from jax import lax
from jax.experimental import pallas as pl
from jax.experimental.pallas import tpu as pltpu

F32 = jnp.float32
BF16 = jnp.bfloat16

D = 1024
D_FF = 2816
N_MOD = 9
EPS = 1e-6
GRID_W = 64
LANE = 128
SUBLANE = 8
N_GROUPS = 4
GROUP_C = D // N_GROUPS
CONV_K = 31
CONV_HALO = 16
N_HEADS = 16
HEAD_DIM = D // N_HEADS
WIN_R = 8
WIN_C = 16
NEG_INF = -1e30
MOD_ROWS = 8

FF_TILE = 256
TM_FFN = 1024
PROJ_ROWS = 256
GLU_COLS = 1024
FFN_ROWS = 256
TM_LIN = 512
TM_CONV = 256
CONV_ROWS = 32
COL_TILE = 16
NATTEN_ROWS = 4
VMEM_LIMIT = 52 * 1024 * 1024
VMEM_LIMIT_WIDE = 58 * 1024 * 1024
TM_FFN_WIDE = 2048


def _cparams(sem):
    return pltpu.CompilerParams(dimension_semantics=sem, vmem_limit_bytes=VMEM_LIMIT)


def _silu(x):
    return x / (1.0 + jnp.exp(-x))


def _norm_mod(x, shift, scale):
    ms = jnp.mean(x * x, axis=-1, keepdims=True)
    return x * lax.rsqrt(ms + EPS) * (1.0 + scale) + shift


def _dot(a, b):
    return jnp.dot(a, b, preferred_element_type=F32)


def _mod_kernel(c_ref, w_ref, b_ref, o_ref):
    s = _silu(c_ref[...]).astype(BF16)
    o_ref[...] = _dot(s, w_ref[...].astype(BF16)) + b_ref[...]


def _mod_call(cond, w_mod, b_mod):
    depth = w_mod.shape[0]
    return pl.pallas_call(
        _mod_kernel,
        grid=(depth, N_MOD),
        in_specs=[
            pl.BlockSpec((MOD_ROWS, D), lambda l, k: (0, 0)),
            pl.BlockSpec((None, D, D), lambda l, k: (l, 0, k)),
            pl.BlockSpec((None, 1, D), lambda l, k: (l, 0, k)),
        ],
        out_specs=pl.BlockSpec((None, MOD_ROWS, D), lambda l, k: (l, 0, k)),
        out_shape=jax.ShapeDtypeStruct((depth, MOD_ROWS, N_MOD * D), F32),
        compiler_params=_cparams(("arbitrary", "arbitrary")),
        name="mod",
    )(cond, w_mod, b_mod.reshape(depth, 1, N_MOD * D))


def _mod_row(tile, tiles_per_batch, n_batch):
    return jnp.minimum(tile // tiles_per_batch, n_batch)


def _ffn_kernel(*refs, k0, n_ff, n_real, in_stream):
    h_ref, mod_ref, wg_ref, wu_ref, wo_ref, o_ref, xn_ref, acc_ref = refs[:1] + refs[2 if in_stream else 1:]
    tokens = lambda rows: h_ref[rows, :]
    j = pl.program_id(1)
    n_chunks = o_ref.shape[0] // FFN_ROWS
    assert n_ff >= 2

    def step(first, last):
        wg, wu, wo = wg_ref[...].astype(BF16), wu_ref[...].astype(BF16), wo_ref[...].astype(BF16)
        gu = {}
        for c in range(n_chunks + 1):
            if c < n_chunks:
                rows = slice(c * FFN_ROWS, (c + 1) * FFN_ROWS)
                if first:
                    xn = _norm_mod(tokens(rows), mod_ref[k0:k0 + 1, :], mod_ref[k0 + 1:k0 + 2, :]).astype(BF16)
                    xn_ref[rows, :] = xn
                else:
                    xn = xn_ref[rows, :]
                gu[c] = (_dot(xn, wg), _dot(xn, wu))
            if c >= 1:
                rows = slice((c - 1) * FFN_ROWS, c * FFN_ROWS)
                g, u = gu.pop(c - 1)
                y = _dot((_silu(g) * u).astype(BF16), wo)
                if first:
                    acc_ref[rows, :] = y
                elif last:
                    o_ref[rows, :] = tokens(rows) + (0.5 * mod_ref[k0 + 2:k0 + 3, :]) * (acc_ref[rows, :] + y)
                else:
                    acc_ref[rows, :] += y

    real = pl.program_id(0) < n_real
    pl.when(jnp.logical_and(real, j == 0))(functools.partial(step, True, False))
    pl.when(jnp.logical_and(real, jnp.logical_and(j > 0, j < n_ff - 1)))(functools.partial(step, False, False))
    pl.when(jnp.logical_and(real, j == n_ff - 1))(functools.partial(step, False, True))

    @pl.when(jnp.logical_and(jnp.logical_not(real), j == n_ff - 1))
    def _():
        o_ref[...] = jnp.zeros_like(o_ref)


def _ffn_call(tok, mod_l, w_gu, w_o, layer, which, k0, n_tiles, n_batch, seq, *, tm, tile0=0, in_place=True,
              stream=None, out_rows=None, zero_tiles=0):
    n_ff = D_FF // FF_TILE
    tpb = seq // tm
    assert not (in_place and stream is not None) and (in_place or stream is not None or tile0 == 0)
    assert zero_tiles == 0 or not in_place
    tok_tile0 = tile0 if in_place else 0
    ff = lambda i, j: jnp.where(i < n_tiles, j, n_ff - 1)
    last_tok = tok.shape[0] // tm - 1
    operands = [tok] + ([stream] if stream is not None else [])
    specs = [pl.BlockSpec((tm, D), lambda i, j: (jnp.minimum(i + tok_tile0, last_tok), 0))]
    if stream is not None:
        specs.append(pl.BlockSpec(memory_space=pl.ANY))
        out_rows = stream.shape[0]
    elif in_place:
        out_rows = tok.shape[0]
    elif out_rows is None:
        out_rows = (n_tiles + zero_tiles) * tm
    mod_row = lambda i: _mod_row(jnp.minimum(i, n_tiles - 1) + tile0, tpb, n_batch)
    return pl.pallas_call(
        functools.partial(_ffn_kernel, k0=k0, n_ff=n_ff, n_real=n_tiles, in_stream=stream is not None),
        grid=(n_tiles + zero_tiles, n_ff),
        in_specs=specs + [
            pl.BlockSpec((None, N_MOD, D), lambda i, j: (mod_row(i), 0, 0)),
            pl.BlockSpec((None, None, D, FF_TILE), lambda i, j: (layer, which, 0, ff(i, j))),
            pl.BlockSpec((None, None, D, FF_TILE), lambda i, j: (layer, which, 0, n_ff + ff(i, j))),
            pl.BlockSpec((None, None, FF_TILE, D), lambda i, j: (layer, which, ff(i, j), 0)),
        ],
        out_specs=pl.BlockSpec((tm, D), lambda i, j: (i + tile0, 0)),
        out_shape=jax.ShapeDtypeStruct((out_rows, D), F32),
        scratch_shapes=[pltpu.VMEM((tm, D), BF16), pltpu.VMEM((tm, D), F32)],
        input_output_aliases={0: 0} if in_place else ({1: 0} if stream is not None else {}),
        compiler_params=pltpu.CompilerParams(dimension_semantics=("arbitrary", "arbitrary"),
                                             vmem_limit_bytes=VMEM_LIMIT_WIDE if tm > TM_FFN else VMEM_LIMIT),
        name="ffn",
    )(*operands, mod_l, w_gu, w_gu, w_o)


def _lin_res_kernel(*refs, fourier):
    if fourier:
        xr_ref, xi_ref, cc_ref, sc_ref, w_ref, b_ref, h_ref, mod_ref, o_ref = refs
        parts = []
        for g in range(N_GROUPS):
            sl = slice(g * GROUP_C, (g + 1) * GROUP_C)
            parts.append(_dot(xr_ref[:, sl], cc_ref[...]) + _dot(xi_ref[:, sl], sc_ref[...]))
        x = jnp.concatenate(parts, axis=-1).astype(BF16)
    else:
        x_ref, w_ref, b_ref, h_ref, mod_ref, o_ref = refs
        x = x_ref[...]
    y = _dot(x, w_ref[...]) + b_ref[...]
    o_ref[...] = h_ref[...] + mod_ref[5:6, :] * y


def _lin_res_call(hs, xs, tables, w, b, mod_l, tile0, n_tiles, mod_row_fn):
    fourier = len(xs) == 2
    x_spec = pl.BlockSpec((TM_LIN, D), lambda i: (i, 0))
    t_spec = pl.BlockSpec((GROUP_C, GROUP_C), lambda i: (0, 0))
    in_specs = [x_spec] * len(xs) + [t_spec] * len(tables) + [
        pl.BlockSpec((D, D), lambda i: (0, 0)),
        pl.BlockSpec((1, D), lambda i: (0, 0)),
        pl.BlockSpec((TM_LIN, D), lambda i: (i + tile0, 0)),
        pl.BlockSpec((None, N_MOD, D), lambda i: (mod_row_fn(i + tile0), 0, 0)),
    ]
    h_index = len(xs) + len(tables) + 2
    return pl.pallas_call(
        functools.partial(_lin_res_kernel, fourier=fourier),
        grid=(n_tiles,),
        in_specs=in_specs,
        out_specs=pl.BlockSpec((TM_LIN, D), lambda i: (i + tile0, 0)),
        out_shape=jax.ShapeDtypeStruct(hs.shape, F32),
        input_output_aliases={h_index: 0},
        compiler_params=_cparams(("arbitrary",)),
        name="lin_res_fourier" if fourier else "lin_res",
    )(*xs, *tables, w, b.reshape(1, D), hs, mod_l)


def _fnet_rows_kernel(h_ref, mod_ref, kf_ref, zr_ref, zi_ref):
    rows = h_ref.shape[0]
    half = rows // 2
    xn = _norm_mod(h_ref[...], mod_ref[3:4, :], mod_ref[4:5, :])
    n = rows * SUBLANE
    n_cos = (half + 1) * SUBLANE
    zr, zi = [], []
    for c0 in range(0, COL_TILE, SUBLANE):
        a = xn[:, c0:c0 + SUBLANE, :].reshape(n, D).astype(BF16)
        z = _dot(kf_ref[...], a)
        zc = z[:n_cos].reshape(half + 1, SUBLANE, D)
        zs = z[n_cos:].reshape(half - 1, SUBLANE, D)
        zero = jnp.zeros((1, SUBLANE, D), F32)
        zr.append(jnp.concatenate([zc] + [zc[rows - k:rows - k + 1] for k in range(half + 1, rows)], axis=0))
        zi.append(jnp.concatenate([zero, zs, zero] + [-zs[rows - k - 1:rows - k] for k in range(half + 1, rows)],
                                  axis=0))
    zr_ref[...] = jnp.concatenate(zr, axis=1).astype(BF16)
    zi_ref[...] = jnp.concatenate(zi, axis=1).astype(BF16)


def _fnet_cols_kernel(zr_ref, zi_ref, m2_ref, cc_ref, sc_ref, w_ref, b_ref, h_ref, mod_ref, o_ref,
                      xr_ref, xi_ref, y_ref):
    n = zr_ref.shape[1]
    for j in range(SUBLANE):
        z = jnp.concatenate([zr_ref[j], zi_ref[j]], axis=0)
        x = _dot(m2_ref[j], z)
        xr_ref[j * n:(j + 1) * n, :] = x[:n].astype(BF16)
        xi_ref[j * n:(j + 1) * n, :] = x[n:].astype(BF16)
    parts = []
    for g in range(N_GROUPS):
        sl = slice(g * GROUP_C, (g + 1) * GROUP_C)
        parts.append(_dot(xr_ref[:, sl], cc_ref[...]) + _dot(xi_ref[:, sl], sc_ref[...]))
    y = _dot(jnp.concatenate(parts, axis=-1).astype(BF16), w_ref[...]) + b_ref[...]
    for j in range(SUBLANE):
        rows = pl.ds(j, n, stride=SUBLANE)
        for lt in range(D // LANE):
            y_ref[lt, rows, :] = y[j * n:(j + 1) * n, lt * LANE:(lt + 1) * LANE]
    y_t = jnp.concatenate([y_ref[lt] for lt in range(D // LANE)], axis=-1).reshape(n, SUBLANE, D)
    o_ref[...] = h_ref[...] + mod_ref[5:6, :] * y_t


def _fnet_ctx_kernel(h_ref, mod_ref, f_ref, xr_ref, xi_ref):
    n = h_ref.shape[0]
    xn = _norm_mod(h_ref[...], mod_ref[3:4, :], mod_ref[4:5, :]).astype(BF16)
    x = _dot(f_ref[...], xn)
    xr_ref[...] = x[:n].astype(BF16)
    xi_ref[...] = x[n:].astype(BF16)


def _dft_tables(rows):
    n_seq = rows * GRID_W
    a = np.arange(rows)
    th1 = 2.0 * np.pi * np.outer(a, a) / rows
    half = rows // 2
    f1 = np.concatenate([np.cos(th1)[:half + 1], -np.sin(th1)[1:half]], axis=0)
    f1 = np.kron(f1, np.eye(SUBLANE))
    k2 = np.arange(GRID_W)[:, None, None]
    k1 = np.arange(rows)[None, :, None]
    n1 = np.arange(GRID_W)[None, None, :]
    th2 = 2.0 * np.pi * (n1 * (GRID_W * k1 + k2) % n_seq) / n_seq
    c2, s2 = np.cos(th2), np.sin(th2)
    m2 = np.concatenate([np.concatenate([c2, s2], axis=2), np.concatenate([-s2, c2], axis=2)], axis=1)
    return f1.astype(np.float32), m2.astype(np.float32)


def _channel_tables(n_seq):
    c = np.arange(GROUP_C)
    th = 2.0 * np.pi * np.outer(c, c) / GROUP_C
    s = 1.0 / np.sqrt(float(n_seq) * GROUP_C)
    return (np.cos(th) * s).astype(np.float32), (np.sin(th) * s).astype(np.float32)


def _fourier_mixer(hs, mod_l, w_out, b_out, n_batch, seq, n_ctx, with_ctx):
    rows = seq // GRID_W
    assert rows == GRID_W, "the two-stage DFT is written for a square token grid"
    n_lat = n_batch * seq
    f1, m2 = _dft_tables(rows)
    cc, sc = _channel_tables(seq)
    f1, m2, cc, sc = (jnp.asarray(t).astype(BF16) for t in (f1, m2, cc, sc))
    w = w_out.astype(BF16)
    n_ct = GRID_W // COL_TILE
    grid3 = hs.reshape(hs.shape[0] // GRID_W, GRID_W, D)
    z_shape = jax.ShapeDtypeStruct((n_batch * rows, GRID_W, D), BF16)
    blk_cols = pl.BlockSpec((rows, COL_TILE, D), lambda b, c: (b, c, 0))
    mod_b = pl.BlockSpec((None, N_MOD, D), lambda b, c: (b, 0, 0))
    zr, zi = pl.pallas_call(
        _fnet_rows_kernel,
        grid=(n_batch, n_ct),
        in_specs=[blk_cols, mod_b, pl.BlockSpec(f1.shape, lambda b, c: (0, 0))],
        out_specs=[blk_cols, blk_cols],
        out_shape=[z_shape, z_shape],
        compiler_params=_cparams(("arbitrary", "arbitrary")),
        name="fnet_rows",
    )(grid3, mod_l, f1)
    n_kt = GRID_W // SUBLANE
    blk_k2 = pl.BlockSpec((SUBLANE, GRID_W, D), lambda b, k: (b * n_kt + k, 0, 0))
    blk_tok = pl.BlockSpec((rows, SUBLANE, D), lambda b, k: (b, k, 0))
    table = pl.BlockSpec((GROUP_C, GROUP_C), lambda b, k: (0, 0))
    n_tok = rows * SUBLANE
    grid3 = pl.pallas_call(
        _fnet_cols_kernel,
        grid=(n_batch, n_kt),
        in_specs=[blk_k2, blk_k2,
                  pl.BlockSpec((SUBLANE, 2 * rows, 2 * GRID_W), lambda b, k: (k, 0, 0)),
                  table, table,
                  pl.BlockSpec((D, D), lambda b, k: (0, 0)),
                  pl.BlockSpec((1, D), lambda b, k: (0, 0)),
                  blk_tok, mod_b],
        out_specs=blk_tok,
        out_shape=jax.ShapeDtypeStruct(grid3.shape, F32),
        scratch_shapes=[pltpu.VMEM((n_tok, D), BF16), pltpu.VMEM((n_tok, D), BF16),
                        pltpu.VMEM((D // LANE, n_tok, LANE), F32)],
        input_output_aliases={7: 0},
        compiler_params=_cparams(("arbitrary", "arbitrary")),
        name="fnet_cols",
    )(zr, zi, m2, cc, sc, w, b_out.reshape(1, D), grid3, mod_l)
    hs = grid3.reshape(hs.shape)
    tpb = seq // TM_LIN
    row_fn = lambda t: _mod_row(t, tpb, n_batch)
    if with_ctx:
        a = np.arange(n_ctx)
        th = 2.0 * np.pi * np.outer(a, a) / n_ctx
        s = np.sqrt(float(seq) / n_ctx)
        fc = jnp.asarray(np.concatenate([np.cos(th) * s, -np.sin(th) * s], axis=0).astype(np.float32)).astype(BF16)
        c_shape = jax.ShapeDtypeStruct((n_batch * n_ctx, D), BF16)
        tile0 = n_lat // n_ctx
        xr, xi = pl.pallas_call(
            _fnet_ctx_kernel,
            grid=(n_batch,),
            in_specs=[pl.BlockSpec((n_ctx, D), lambda b: (tile0 + b, 0)),
                      pl.BlockSpec((None, N_MOD, D), lambda b: (n_batch, 0, 0)),
                      pl.BlockSpec((2 * n_ctx, n_ctx), lambda b: (0, 0))],
            out_specs=[pl.BlockSpec((n_ctx, D), lambda b: (b, 0))] * 2,
            out_shape=[c_shape, c_shape],
            compiler_params=_cparams(("arbitrary",)),
            name="fnet_ctx",
        )(hs, mod_l, fc)
        hs = _lin_res_call(hs, (xr, xi), (cc, sc), w, b_out, mod_l,
                           n_lat // TM_LIN, n_batch * n_ctx // TM_LIN, row_fn)
    return hs


def _mixer_input_chunks(h_ref, mod_ref, xn_ref, first):
    for r0 in range(0, h_ref.shape[0], PROJ_ROWS):
        rows = slice(r0, r0 + PROJ_ROWS)
        if first:
            xn = _norm_mod(h_ref[rows, :], mod_ref[3:4, :], mod_ref[4:5, :]).astype(BF16)
            xn_ref[rows, :] = xn
        else:
            xn = xn_ref[rows, :]
        yield rows, xn


def _glu_kernel(h_ref, mod_ref, wa_ref, wg_ref, ba_ref, bg_ref, o_ref, xn_ref):
    def step(first):
        for rows, xn in _mixer_input_chunks(h_ref, mod_ref, xn_ref, first):
            a = _dot(xn, wa_ref[...]) + ba_ref[...]
            g = _dot(xn, wg_ref[...]) + bg_ref[...]
            o_ref[rows, :] = a / (1.0 + jnp.exp(-g))

    j = pl.program_id(1)
    pl.when(j == 0)(functools.partial(step, True))
    pl.when(j > 0)(functools.partial(step, False))


def _conv_kernel(u_ref, up_ref, un_ref, wdw_ref, bdw_ref, lng_ref, lnb_ref, w_ref, b_ref, h_ref, mod_ref,
                 o_ref, buf_ref, conv_ref, v_ref, *, lat_tiles, tiles_per_seq):
    i = pl.program_id(0)
    is_ctx = i >= lat_tiles
    first = jnp.logical_or(is_ctx, i % tiles_per_seq == 0)
    last = jnp.logical_or(is_ctx, i % tiles_per_seq == tiles_per_seq - 1)
    tm = u_ref.shape[0]
    n_buf = tm + 2 * CONV_HALO
    buf_ref[0, 0:CONV_HALO, :] = jnp.where(first, 0.0, up_ref[...])
    buf_ref[0, CONV_HALO:CONV_HALO + tm, :] = u_ref[...]
    buf_ref[0, CONV_HALO + tm:, :] = jnp.where(last, 0.0, un_ref[...])
    for s in range(1, SUBLANE):
        buf_ref[s] = pltpu.roll(buf_ref[0], n_buf - s, axis=0)
    off = CONV_HALO - CONV_K // 2
    for lt in range(D // LANE):
        lanes = slice(lt * LANE, (lt + 1) * LANE)
        for c0 in range(0, tm, CONV_ROWS):
            acc = jnp.zeros((CONV_ROWS, LANE), F32) + bdw_ref[:, lanes]
            for k in range(CONV_K):
                row = c0 + (off + k) // SUBLANE * SUBLANE
                acc = acc + wdw_ref[k:k + 1, lanes] * buf_ref[(off + k) % SUBLANE, row:row + CONV_ROWS, lanes]
            conv_ref[c0:c0 + CONV_ROWS, lanes] = acc
    chunk = 32
    for c0 in range(0, tm, chunk):
        acc = conv_ref[c0:c0 + chunk, :]
        mu = jnp.mean(acc, axis=-1, keepdims=True)
        d = acc - mu
        var = jnp.mean(d * d, axis=-1, keepdims=True)
        v = d * lax.rsqrt(var + EPS) * lng_ref[...] + lnb_ref[...]
        v_ref[c0:c0 + chunk, :] = _silu(v).astype(BF16)
    y = _dot(v_ref[...], w_ref[...]) + b_ref[...]
    o_ref[...] = h_ref[...] + mod_ref[5:6, :] * y


def _conv_mixer(hs, mod_l, w_in, b_in, w_dw, b_dw, ln_g, ln_b, w_out, b_out, n_batch, seq, n_ctx):
    n_lat = n_batch * seq
    n_all = n_lat + n_batch * n_ctx
    assert n_ctx == TM_CONV, "context sequences are one conv tile long"
    n_tiles = n_all // TM_FFN
    n_col = D // GLU_COLS
    tpb = seq // TM_FFN
    col = pl.BlockSpec((D, GLU_COLS), lambda i, j: (0, j))
    gcol = pl.BlockSpec((D, GLU_COLS), lambda i, j: (0, n_col + j))
    u = pl.pallas_call(
        _glu_kernel,
        grid=(n_tiles, n_col),
        in_specs=[pl.BlockSpec((TM_FFN, D), lambda i, j: (i, 0)),
                  pl.BlockSpec((None, N_MOD, D), lambda i, j: (_mod_row(i, tpb, n_batch), 0, 0)),
                  col, gcol,
                  pl.BlockSpec((1, GLU_COLS), lambda i, j: (0, j)),
                  pl.BlockSpec((1, GLU_COLS), lambda i, j: (0, n_col + j))],
        out_specs=pl.BlockSpec((TM_FFN, GLU_COLS), lambda i, j: (i, j)),
        out_shape=jax.ShapeDtypeStruct((n_all, D), F32),
        scratch_shapes=[pltpu.VMEM((TM_FFN, D), BF16)],
        compiler_params=_cparams(("arbitrary", "arbitrary")),
        name="conv_glu",
    )(hs, mod_l, w_in.astype(BF16), w_in.astype(BF16), b_in.reshape(1, 2 * D), b_in.reshape(1, 2 * D))

    n_ct = n_all // TM_CONV
    hpt = TM_CONV // CONV_HALO
    n_halo = n_all // CONV_HALO
    tps = seq // TM_CONV
    vec = pl.BlockSpec((1, D), lambda i: (0, 0))
    tile = pl.BlockSpec((TM_CONV, D), lambda i: (i, 0))
    return pl.pallas_call(
        functools.partial(_conv_kernel, lat_tiles=n_lat // TM_CONV, tiles_per_seq=tps),
        grid=(n_ct,),
        in_specs=[tile,
                  pl.BlockSpec((CONV_HALO, D), lambda i: (jnp.maximum(i * hpt - 1, 0), 0)),
                  pl.BlockSpec((CONV_HALO, D), lambda i: (jnp.minimum((i + 1) * hpt, n_halo - 1), 0)),
                  pl.BlockSpec((CONV_K, D), lambda i: (0, 0)),
                  vec, vec, vec,
                  pl.BlockSpec((D, D), lambda i: (0, 0)),
                  vec, tile,
                  pl.BlockSpec((None, N_MOD, D), lambda i: (_mod_row(i, tps, n_batch), 0, 0))],
        out_specs=tile,
        out_shape=jax.ShapeDtypeStruct(hs.shape, F32),
        scratch_shapes=[pltpu.VMEM((SUBLANE, TM_CONV + 2 * CONV_HALO, D), F32), pltpu.VMEM((TM_CONV, D), F32),
                        pltpu.VMEM((TM_CONV, D), BF16)],
        input_output_aliases={9: 0},
        compiler_params=_cparams(("arbitrary",)),
        name="conv_dw",
    )(u, u, u, w_dw, b_dw.reshape(1, D), ln_g.reshape(1, D), ln_b.reshape(1, D), w_out.astype(BF16),
      b_out.reshape(1, D), hs, mod_l)


def _qkv_kernel(h_ref, mod_ref, w_ref, gain_ref, e_ref, o_ref, xn_ref):
    n_e = e_ref.shape[0]

    def step(first, head_norm):
        for rows, xn in _mixer_input_chunks(h_ref, mod_ref, xn_ref, first):
            y = _dot(xn, w_ref[...])
            if not head_norm:
                o_ref[rows, :] = y.astype(BF16)
                continue
            for c0 in range(0, D, n_e):
                cols = slice(c0, c0 + n_e)
                yc = y[:, cols]
                ss = _dot((yc * yc).astype(BF16), e_ref[...])
                rs = lax.rsqrt(ss * (1.0 / HEAD_DIM) + EPS) * gain_ref[:, cols]
                o_ref[rows, cols] = (yc * rs).astype(BF16)

    j = pl.program_id(1)
    pl.when(j == 0)(functools.partial(step, True, True))
    pl.when(j == 1)(functools.partial(step, False, True))
    pl.when(j == 2)(functools.partial(step, False, False))


def _softmax_pv(s_list, v_list):
    m = s_list[0].max(axis=-1, keepdims=True)
    for s in s_list[1:]:
        m = jnp.maximum(m, s.max(axis=-1, keepdims=True))
    den = 0.0
    acc = 0.0
    for s, v in zip(s_list, v_list):
        p = jnp.exp(s - m)
        den = den + p.sum(axis=-1, keepdims=True)
        acc = acc + _dot(p.astype(BF16), v)
    return acc / den


def _qk(q, k):
    return lax.dot_general(q, k, (((1,), (1,)), ((), ())), preferred_element_type=F32)


def _natten_kernel(q_ref, k_ref, v_ref, kc_ref, vc_ref, bias_ref, o_ref, *, rows, kr):
    r0 = pl.program_id(1) * NATTEN_ROWS
    left = lax.broadcasted_iota(jnp.int32, (GRID_W, LANE), 1) < HEAD_DIM
    n2 = 2 * GRID_W
    wins, dr0s = [], []
    for i in range(NATTEN_ROWS):
        rs = jnp.clip(r0 + i - kr // 2, 0, rows - kr)
        wins.append(pl.ds(pl.multiple_of(rs * GRID_W, GRID_W), kr * GRID_W))
        dr0s.append(rs - (r0 + i) + WIN_R - 1)
    for p in range(N_HEADS // 2):
        cols = slice(p * LANE, (p + 1) * LANE)
        q2 = []
        for i in range(NATTEN_ROWS):
            q = q_ref[i * GRID_W:(i + 1) * GRID_W, cols]
            zero = jnp.zeros_like(q)
            q2 += [jnp.where(left, q, zero), jnp.where(left, zero, q)]
        q2 = jnp.concatenate(q2, axis=0)
        s_ctx = _qk(q2, kc_ref[:, cols])
        m_ctx = s_ctx.max(axis=-1, keepdims=True)
        m, den, acc = [], [], []
        for i in range(NATTEN_ROWS):
            mine = slice(i * n2, (i + 1) * n2)
            bias = jnp.concatenate([bias_ref[p, dr0s[i] + 2 * t] for t in range(kr // 2)], axis=-1)
            s_lat = _qk(q2[mine], k_ref[wins[i], cols]) + bias
            m.append(jnp.maximum(s_lat.max(axis=-1, keepdims=True), m_ctx[mine]))
            p_lat = jnp.exp(s_lat - m[i])
            den.append(p_lat.sum(axis=-1, keepdims=True))
            acc.append(_dot(p_lat.astype(BF16), v_ref[wins[i], cols]))
        p_ctx = jnp.exp(s_ctx - jnp.concatenate(m, axis=0))
        den_ctx = p_ctx.sum(axis=-1, keepdims=True)
        acc_ctx = _dot(p_ctx.astype(BF16), vc_ref[:, cols])
        for i in range(NATTEN_ROWS):
            mine = slice(i * n2, (i + 1) * n2)
            o = (acc[i] + acc_ctx[mine]) / (den[i] + den_ctx[mine])
            o_ref[i * GRID_W:(i + 1) * GRID_W, cols] = jnp.where(left, o[:GRID_W], o[GRID_W:]).astype(BF16)


def _bias_kernel(rpb_ref, o_ref):
    qc = lax.broadcasted_iota(jnp.int32, (GRID_W, LANE), 0)
    lane = lax.broadcasted_iota(jnp.int32, (GRID_W, LANE), 1)
    kc = lane % GRID_W
    w0 = jnp.clip(qc - WIN_C // 2, 0, GRID_W - WIN_C)
    ok = jnp.logical_and(kc >= w0, kc < w0 + WIN_C)
    sub = lax.broadcasted_iota(jnp.int32, (SUBLANE, LANE), 0)
    tiles = []
    for dr in range(2 * WIN_R - 1):
        t = jnp.broadcast_to(rpb_ref[dr:dr + 1, :], (SUBLANE, LANE))
        t = pltpu.roll(t, LANE - (WIN_C - 1), axis=1)
        for bit in range(SUBLANE.bit_length() - 1):
            t = jnp.where((sub >> bit) & 1 == 1, pltpu.roll(t, 1 << bit, axis=1), t)
        tiles.append(jnp.concatenate([t] + [pltpu.roll(t, SUBLANE * a, axis=1)
                                            for a in range(1, GRID_W // SUBLANE)], axis=0))
    for p in range(2 * WIN_R - 2):
        pair = jnp.where(lane < GRID_W, tiles[p], pltpu.roll(tiles[p + 1], GRID_W, axis=1))
        o_ref[p] = jnp.where(ok, pair, NEG_INF)


def _ctx_attn_kernel(q_ref, k_ref, v_ref, o_ref):
    outs = []
    for h in range(N_HEADS):
        hd = slice(h * HEAD_DIM, (h + 1) * HEAD_DIM)
        outs.append(_softmax_pv([_qk(q_ref[:, hd], k_ref[:, hd])], [v_ref[:, hd]]))
    o_ref[...] = jnp.concatenate(outs, axis=-1).astype(BF16)


def _natten_bias(rpb):
    n_dr, n_dc = 2 * WIN_R - 1, 2 * WIN_C - 1
    assert rpb.shape == (N_HEADS, n_dr, n_dc) and GRID_W * 2 == LANE
    padded = jnp.pad(rpb.astype(F32), ((0, 0), (0, 0), (0, LANE - n_dc)))
    return pl.pallas_call(
        _bias_kernel,
        grid=(N_HEADS,),
        in_specs=[pl.BlockSpec((None, n_dr, LANE), lambda h: (h, 0, 0))],
        out_specs=pl.BlockSpec((None, n_dr - 1, GRID_W, LANE), lambda h: (h // 2, 0, h % 2, 0)),
        out_shape=jax.ShapeDtypeStruct((N_HEADS // 2, n_dr - 1, 2 * GRID_W, LANE), F32),
        compiler_params=_cparams(("arbitrary",)),
        name="natten_bias",
    )(padded)


def _natten_mixer(hs, mod_l, w_qkv, q_gain, k_gain, rpb, w_o, n_batch, seq, n_ctx):
    n_lat = n_batch * seq
    n_all = n_lat + n_batch * n_ctx
    rows = seq // GRID_W
    kr = min(WIN_R, rows)
    n_tiles = n_all // TM_FFN
    tpb = seq // TM_FFN
    scale = HEAD_DIM ** -0.5
    gain = jnp.concatenate([jnp.tile(q_gain, N_HEADS) * scale, jnp.tile(k_gain, N_HEADS),
                            jnp.ones((D,), F32)]).reshape(1, 3 * D)
    lane = np.arange(FF_TILE)
    e = jnp.asarray((lane[:, None] // HEAD_DIM == lane[None, :] // HEAD_DIM).astype(np.float32)).astype(BF16)
    qkv = pl.pallas_call(
        _qkv_kernel,
        grid=(n_tiles, 3),
        in_specs=[pl.BlockSpec((TM_FFN, D), lambda i, j: (i, 0)),
                  pl.BlockSpec((None, N_MOD, D), lambda i, j: (_mod_row(i, tpb, n_batch), 0, 0)),
                  pl.BlockSpec((D, D), lambda i, j: (0, j)),
                  pl.BlockSpec((1, D), lambda i, j: (0, j)),
                  pl.BlockSpec((FF_TILE, FF_TILE), lambda i, j: (0, 0))],
        out_specs=pl.BlockSpec((TM_FFN, D), lambda i, j: (i, j)),
        out_shape=jax.ShapeDtypeStruct((n_all, 3 * D), BF16),
        scratch_shapes=[pltpu.VMEM((TM_FFN, D), BF16)],
        compiler_params=_cparams(("arbitrary", "arbitrary")),
        name="qkv",
    )(hs, mod_l, w_qkv.astype(BF16), gain, e)

    assert kr == WIN_R, "bias tiles pair up the key rows of a full window"
    bias = _natten_bias(rpb)
    ctx0 = n_lat // n_ctx
    once = pl.Buffered(1)
    n_rb = rows // NATTEN_ROWS
    o_lat = pl.pallas_call(
        functools.partial(_natten_kernel, rows=rows, kr=kr),
        grid=(n_batch, n_rb),
        in_specs=[pl.BlockSpec((NATTEN_ROWS * GRID_W, D), lambda b, r: (b * n_rb + r, 0)),
                  pl.BlockSpec((seq, D), lambda b, r: (b, 1)),
                  pl.BlockSpec((seq, D), lambda b, r: (b, 2), pipeline_mode=once),
                  pl.BlockSpec((n_ctx, D), lambda b, r: (ctx0 + b, 1)),
                  pl.BlockSpec((n_ctx, D), lambda b, r: (ctx0 + b, 2)),
                  pl.BlockSpec(bias.shape, lambda b, r: (0, 0, 0, 0), pipeline_mode=once)],
        out_specs=pl.BlockSpec((NATTEN_ROWS * GRID_W, D), lambda b, r: (b * n_rb + r, 0)),
        out_shape=jax.ShapeDtypeStruct((n_lat, D), BF16),
        compiler_params=_cparams(("arbitrary", "arbitrary")),
        name="natten",
    )(qkv, qkv, qkv, qkv, qkv, bias)
    o_ctx = pl.pallas_call(
        _ctx_attn_kernel,
        grid=(n_batch,),
        in_specs=[pl.BlockSpec((n_ctx, D), lambda b: (ctx0 + b, 0)),
                  pl.BlockSpec((n_ctx, D), lambda b: (ctx0 + b, 1)),
                  pl.BlockSpec((n_ctx, D), lambda b: (ctx0 + b, 2))],
        out_specs=pl.BlockSpec((n_ctx, D), lambda b: (b, 0)),
        out_shape=jax.ShapeDtypeStruct((n_batch * n_ctx, D), BF16),
        compiler_params=_cparams(("arbitrary",)),
        name="ctx_attn",
    )(qkv, qkv, qkv)
    w = w_o.astype(BF16)
    zero_b = jnp.zeros((D,), F32)
    row_fn = lambda t: _mod_row(t, seq // TM_LIN, n_batch)
    hs = _lin_res_call(hs, (o_lat,), (), w, zero_b, mod_l, 0, n_lat // TM_LIN, row_fn)
    return _lin_res_call(hs, (o_ctx,), (), w, zero_b, mod_l, n_lat // TM_LIN, n_batch * n_ctx // TM_LIN, row_fn)


def kernel(x, c, ctx, c_ctx, w_mod, b_mod, w_ff_in, w_ff_out, fnet_w_out, fnet_b_out, conv_w_in, conv_b_in, conv_w_dw, conv_b_dw, conv_ln_g, conv_ln_b, conv_w_out, conv_b_out, na_w_qkv, na_q_gain, na_k_gain, na_rpb, na_w_o):
    n_batch, seq, d = x.shape
    n_ctx = ctx.shape[1]
    depth = w_mod.shape[0]
    assert d == D and n_batch < MOD_ROWS
    assert seq % TM_FFN == 0 and (n_batch * n_ctx) % TM_FFN == 0
    n_lat = n_batch * seq

    cond = jnp.concatenate([c, c_ctx[None, :], jnp.zeros((MOD_ROWS - n_batch - 1, D), F32)], axis=0)
    mod = _mod_call(cond, w_mod, b_mod).reshape(depth, MOD_ROWS, N_MOD, D)
    n_tok = n_lat + n_batch * n_ctx
    n_stream = n_tok + (-n_tok % seq)
    assert depth > 1, "the first half-step is written for a layer that updates the context tokens"
    lat_tiles = n_lat // TM_FFN
    ctx_tiles = n_batch * n_ctx // TM_FFN
    wide_tiles = n_lat // TM_FFN_WIDE
    assert n_lat % TM_FFN_WIDE == 0 and (n_stream - n_lat) % TM_FFN_WIDE == 0
    w_gu, w_o = w_ff_in, w_ff_out

    def half_step(hs, layer, which, k0, with_ctx, in_place):
        args = (mod[layer], w_gu, w_o, layer, which, k0)
        hs = _ffn_call(hs, *args, wide_tiles, n_batch, seq, tm=TM_FFN_WIDE, in_place=in_place)
        if with_ctx:
            hs = _ffn_call(hs, *args, ctx_tiles, n_batch, seq, tm=TM_FFN, tile0=lat_tiles)
        return hs

    for i in range(depth):
        kind, slot = i % 3, i // 3
        ctx_post = i < depth - 1
        ctx_pre = ctx_post or kind == 2
        mod_l = mod[i]
        if i == 0:
            args = (mod_l, w_gu, w_o, 0, 0, 0)
            hs = _ffn_call(x.reshape(n_lat, D), *args, wide_tiles, n_batch, seq, tm=TM_FFN_WIDE, in_place=False,
                           out_rows=n_stream, zero_tiles=(n_stream - n_lat) // TM_FFN_WIDE)
            hs = _ffn_call(ctx.reshape(n_batch * n_ctx, D), *args, ctx_tiles, n_batch, seq, tm=TM_FFN,
                           tile0=lat_tiles, in_place=False, stream=hs)
        else:
            hs = half_step(hs, i, 0, 0, ctx_pre, True)
        if kind == 0:
            hs = _fourier_mixer(hs, mod_l, fnet_w_out[slot], fnet_b_out[slot], n_batch, seq, n_ctx, ctx_post)
        elif kind == 1:
            assert ctx_post
            hs = _conv_mixer(hs, mod_l, conv_w_in[slot], conv_b_in[slot], conv_w_dw[slot], conv_b_dw[slot],
                             conv_ln_g[slot], conv_ln_b[slot], conv_w_out[slot], conv_b_out[slot],
                             n_batch, seq, n_ctx)
        else:
            assert ctx_post
            hs = _natten_mixer(hs, mod_l, na_w_qkv[slot], na_q_gain[slot], na_k_gain[slot], na_rpb[slot],
                               na_w_o[slot], n_batch, seq, n_ctx)
        final = i == depth - 1
        assert ctx_post != final
        hs = half_step(hs, i, 1, 6, ctx_post, not final)
    return hs[:n_lat].reshape(n_batch, seq, D)
```

```python
import functools

import numpy as np
import jax
import jax.numpy as jnp
from jax import lax
from jax.experimental import pallas as pl
from jax.experimental.pallas import tpu as pltpu

F32 = jnp.float32
BF16 = jnp.bfloat16

D = 1024
D_FF = 2816
N_MOD = 9
EPS = 1e-6
GRID_W = 64
LANE = 128
SUBLANE = 8
N_GROUPS = 4
GROUP_C = D // N_GROUPS
CONV_K = 31
CONV_HALO = 16
N_HEADS = 16
HEAD_DIM = D // N_HEADS
WIN_R = 8
WIN_C = 16
NEG_INF = -1e30
MOD_ROWS = 8

FF_TILE = 256
FF_PER_STEP = 2
TM_FFN = 1024
PROJ_ROWS = 256
GLU_COLS = 1024
FFN_ROWS = 256
TM_LIN = 512
TM_CONV = 256
CONV_ROWS = 32
COL_TILE = 16
NATTEN_ROWS = 4
VMEM_LIMIT = 52 * 1024 * 1024
VMEM_LIMIT_WIDE = 58 * 1024 * 1024
TM_FFN_WIDE = 2048


def _cparams(sem):
    return pltpu.CompilerParams(dimension_semantics=sem, vmem_limit_bytes=VMEM_LIMIT)


def _silu(x):
    return x / (1.0 + jnp.exp(-x))


def _norm_mod(x, shift, scale):
    ms = jnp.mean(x * x, axis=-1, keepdims=True)
    return x * lax.rsqrt(ms + EPS) * (1.0 + scale) + shift


def _dot(a, b):
    return jnp.dot(a, b, preferred_element_type=F32)


def _mod_kernel(c_ref, w_ref, b_ref, o_ref):
    s = _silu(c_ref[...]).astype(BF16)
    o_ref[...] = _dot(s, w_ref[...].astype(BF16)) + b_ref[...]


def _mod_call(cond, w_mod, b_mod):
    depth = w_mod.shape[0]
    return pl.pallas_call(
        _mod_kernel,
        grid=(depth, N_MOD),
        in_specs=[
            pl.BlockSpec((MOD_ROWS, D), lambda l, k: (0, 0)),
            pl.BlockSpec((None, D, D), lambda l, k: (l, 0, k)),
            pl.BlockSpec((None, 1, D), lambda l, k: (l, 0, k)),
        ],
        out_specs=pl.BlockSpec((None, MOD_ROWS, D), lambda l, k: (l, 0, k)),
        out_shape=jax.ShapeDtypeStruct((depth, MOD_ROWS, N_MOD * D), F32),
        compiler_params=_cparams(("arbitrary", "arbitrary")),
        name="mod",
    )(cond, w_mod, b_mod.reshape(depth, 1, N_MOD * D))


def _mod_row(tile, tiles_per_batch, n_batch):
    return jnp.minimum(tile // tiles_per_batch, n_batch)


def _ffn_kernel(*refs, k0, n_ff, n_real, in_stream):
    h_ref, mod_ref = refs[0], refs[2 if in_stream else 1]
    w_refs = refs[3 if in_stream else 2:-2]
    o_ref, xn_ref = refs[-2:]
    wg_refs, wu_refs, wo_refs = (w_refs[t * FF_PER_STEP:(t + 1) * FF_PER_STEP] for t in range(3))
    tokens = lambda rows: h_ref[rows, :]
    j = pl.program_id(1)
    n_chunks = o_ref.shape[0] // FFN_ROWS
    n_steps = pl.cdiv(n_ff, FF_PER_STEP)
    assert n_steps >= 2

    def step(first, last):
        n_sub = n_ff - FF_PER_STEP * (n_steps - 1) if last else FF_PER_STEP
        wg, wu, wo = ([r[...].astype(BF16) for r in w[:n_sub]] for w in (wg_refs, wu_refs, wo_refs))
        gu = {}
        for c in range(n_chunks + 1):
            if c < n_chunks:
                rows = slice(c * FFN_ROWS, (c + 1) * FFN_ROWS)
                if first:
                    xn = _norm_mod(tokens(rows), mod_ref[k0:k0 + 1, :], mod_ref[k0 + 1:k0 + 2, :]).astype(BF16)
                    xn_ref[rows, :] = xn
                else:
                    xn = xn_ref[rows, :]
                gu[c] = [(_dot(xn, wg[t]), _dot(xn, wu[t])) for t in range(n_sub)]
            if c >= 1:
                rows = slice((c - 1) * FFN_ROWS, c * FFN_ROWS)
                y = None
                for t, (g, u) in enumerate(gu.pop(c - 1)):
                    yt = _dot((_silu(g) * u).astype(BF16), wo[t])
                    y = yt if y is None else y + yt
                if first:
                    o_ref[rows, :] = y
                elif last:
                    o_ref[rows, :] = tokens(rows) + (0.5 * mod_ref[k0 + 2:k0 + 3, :]) * (o_ref[rows, :] + y)
                else:
                    o_ref[rows, :] += y

    real = pl.program_id(0) < n_real
    pl.when(jnp.logical_and(real, j == 0))(functools.partial(step, True, False))
    pl.when(jnp.logical_and(real, jnp.logical_and(j > 0, j < n_steps - 1)))(functools.partial(step, False, False))
    pl.when(jnp.logical_and(real, j == n_steps - 1))(functools.partial(step, False, True))

    @pl.when(jnp.logical_and(jnp.logical_not(real), j == n_steps - 1))
    def _():
        o_ref[...] = jnp.zeros_like(o_ref)


def _ffn_call(tok, mod_l, w_gu, w_o, layer, which, k0, n_tiles, n_batch, seq, *, tm, tile0=0, in_place=True,
              stream=None, out_rows=None, zero_tiles=0):
    n_ff = D_FF // FF_TILE
    tpb = seq // tm
    assert not (in_place and stream is not None) and (in_place or stream is not None or tile0 == 0)
    assert zero_tiles == 0 or not in_place
    tok_tile0 = tile0 if in_place else 0
    n_steps = pl.cdiv(n_ff, FF_PER_STEP)

    def ff(t):
        return lambda i, j: jnp.where(i < n_tiles, jnp.minimum(FF_PER_STEP * j + t, n_ff - 1), n_ff - 1)

    subs = range(FF_PER_STEP)
    last_tok = tok.shape[0] // tm - 1
    operands = [tok] + ([stream] if stream is not None else [])
    specs = [pl.BlockSpec((tm, D), lambda i, j: (jnp.minimum(i + tok_tile0, last_tok), 0))]
    if stream is not None:
        specs.append(pl.BlockSpec(memory_space=pl.ANY))
        out_rows = stream.shape[0]
    elif in_place:
        out_rows = tok.shape[0]
    elif out_rows is None:
        out_rows = (n_tiles + zero_tiles) * tm
    mod_row = lambda i: _mod_row(jnp.minimum(i, n_tiles - 1) + tile0, tpb, n_batch)
    return pl.pallas_call(
        functools.partial(_ffn_kernel, k0=k0, n_ff=n_ff, n_real=n_tiles, in_stream=stream is not None),
        grid=(n_tiles + zero_tiles, n_steps),
        in_specs=specs + [pl.BlockSpec((None, N_MOD, D), lambda i, j: (mod_row(i), 0, 0))]
        + [pl.BlockSpec((None, None, D, FF_TILE), lambda i, j, f=ff(t): (layer, which, 0, f(i, j))) for t in subs]
        + [pl.BlockSpec((None, None, D, FF_TILE), lambda i, j, f=ff(t): (layer, which, 0, n_ff + f(i, j)))
           for t in subs]
        + [pl.BlockSpec((None, None, FF_TILE, D), lambda i, j, f=ff(t): (layer, which, f(i, j), 0)) for t in subs],
        out_specs=pl.BlockSpec((tm, D), lambda i, j: (i + tile0, 0)),
        out_shape=jax.ShapeDtypeStruct((out_rows, D), F32),
        scratch_shapes=[pltpu.VMEM((tm, D), BF16)],
        input_output_aliases={0: 0} if in_place else ({1: 0} if stream is not None else {}),
        compiler_params=pltpu.CompilerParams(dimension_semantics=("arbitrary", "arbitrary"),
                                             vmem_limit_bytes=VMEM_LIMIT_WIDE if tm > TM_FFN else VMEM_LIMIT),
        name="ffn",
    )(*operands, mod_l, *([w_gu] * (2 * FF_PER_STEP)), *([w_o] * FF_PER_STEP))


def _lin_res_kernel(*refs, fourier):
    if fourier:
        xr_ref, xi_ref, cc_ref, sc_ref, w_ref, b_ref, h_ref, mod_ref, o_ref = refs
        parts = []
        for g in range(N_GROUPS):
            sl = slice(g * GROUP_C, (g + 1) * GROUP_C)
            parts.append(_dot(xr_ref[:, sl], cc_ref[...]) + _dot(xi_ref[:, sl], sc_ref[...]))
        x = jnp.concatenate(parts, axis=-1).astype(BF16)
    else:
        x_ref, w_ref, b_ref, h_ref, mod_ref, o_ref = refs
        x = x_ref[...]
    y = _dot(x, w_ref[...]) + b_ref[...]
    o_ref[...] = h_ref[...] + mod_ref[5:6, :] * y


def _lin_res_call(hs, xs, tables, w, b, mod_l, tile0, n_tiles, mod_row_fn):
    fourier = len(xs) == 2
    x_spec = pl.BlockSpec((TM_LIN, D), lambda i: (i, 0))
    t_spec = pl.BlockSpec((GROUP_C, GROUP_C), lambda i: (0, 0))
    in_specs = [x_spec] * len(xs) + [t_spec] * len(tables) + [
        pl.BlockSpec((D, D), lambda i: (0, 0)),
        pl.BlockSpec((1, D), lambda i: (0, 0)),
        pl.BlockSpec((TM_LIN, D), lambda i: (i + tile0, 0)),
        pl.BlockSpec((None, N_MOD, D), lambda i: (mod_row_fn(i + tile0), 0, 0)),
    ]
    h_index = len(xs) + len(tables) + 2
    return pl.pallas_call(
        functools.partial(_lin_res_kernel, fourier=fourier),
        grid=(n_tiles,),
        in_specs=in_specs,
        out_specs=pl.BlockSpec((TM_LIN, D), lambda i: (i + tile0, 0)),
        out_shape=jax.ShapeDtypeStruct(hs.shape, F32),
        input_output_aliases={h_index: 0},
        compiler_params=_cparams(("arbitrary",)),
        name="lin_res_fourier" if fourier else "lin_res",
    )(*xs, *tables, w, b.reshape(1, D), hs, mod_l)


def _fnet_rows_kernel(h_ref, mod_ref, kf_ref, zr_ref, zi_ref):
    rows = h_ref.shape[0]
    half = rows // 2
    xn = _norm_mod(h_ref[...], mod_ref[3:4, :], mod_ref[4:5, :])
    n = rows * SUBLANE
    n_cos = (half + 1) * SUBLANE
    zr, zi = [], []
    for c0 in range(0, COL_TILE, SUBLANE):
        a = xn[:, c0:c0 + SUBLANE, :].reshape(n, D).astype(BF16)
        z = _dot(kf_ref[...], a)
        zc = z[:n_cos].reshape(half + 1, SUBLANE, D)
        zs = z[n_cos:].reshape(half - 1, SUBLANE, D)
        zero = jnp.zeros((1, SUBLANE, D), F32)
        zr.append(jnp.concatenate([zc] + [zc[rows - k:rows - k + 1] for k in range(half + 1, rows)], axis=0))
        zi.append(jnp.concatenate([zero, zs, zero] + [-zs[rows - k - 1:rows - k] for k in range(half + 1, rows)],
                                  axis=0))
    zr_ref[...] = jnp.concatenate(zr, axis=1).astype(BF16)
    zi_ref[...] = jnp.concatenate(zi, axis=1).astype(BF16)


def _fnet_cols_kernel(zr_ref, zi_ref, m2_ref, cc_ref, sc_ref, w_ref, b_ref, h_ref, mod_ref, o_ref,
                      xr_ref, xi_ref, y_ref):
    n = zr_ref.shape[1]
    for j in range(SUBLANE):
        z = jnp.concatenate([zr_ref[j], zi_ref[j]], axis=0)
        x = _dot(m2_ref[j], z)
        xr_ref[j * n:(j + 1) * n, :] = x[:n].astype(BF16)
        xi_ref[j * n:(j + 1) * n, :] = x[n:].astype(BF16)
    parts = []
    for g in range(N_GROUPS):
        sl = slice(g * GROUP_C, (g + 1) * GROUP_C)
        parts.append(_dot(xr_ref[:, sl], cc_ref[...]) + _dot(xi_ref[:, sl], sc_ref[...]))
    y = _dot(jnp.concatenate(parts, axis=-1).astype(BF16), w_ref[...]) + b_ref[...]
    for j in range(SUBLANE):
        rows = pl.ds(j, n, stride=SUBLANE)
        for lt in range(D // LANE):
            y_ref[lt, rows, :] = y[j * n:(j + 1) * n, lt * LANE:(lt + 1) * LANE]
    y_t = jnp.concatenate([y_ref[lt] for lt in range(D // LANE)], axis=-1).reshape(n, SUBLANE, D)
    o_ref[...] = h_ref[...] + mod_ref[5:6, :] * y_t


def _fnet_ctx_kernel(h_ref, mod_ref, f_ref, xr_ref, xi_ref):
    n = h_ref.shape[0]
    xn = _norm_mod(h_ref[...], mod_ref[3:4, :], mod_ref[4:5, :]).astype(BF16)
    x = _dot(f_ref[...], xn)
    xr_ref[...] = x[:n].astype(BF16)
    xi_ref[...] = x[n:].astype(BF16)


def _dft_tables(rows):
    n_seq = rows * GRID_W
    a = np.arange(rows)
    th1 = 2.0 * np.pi * np.outer(a, a) / rows
    half = rows // 2
    f1 = np.concatenate([np.cos(th1)[:half + 1], -np.sin(th1)[1:half]], axis=0)
    f1 = np.kron(f1, np.eye(SUBLANE))
    k2 = np.arange(GRID_W)[:, None, None]
    k1 = np.arange(rows)[None, :, None]
    n1 = np.arange(GRID_W)[None, None, :]
    th2 = 2.0 * np.pi * (n1 * (GRID_W * k1 + k2) % n_seq) / n_seq
    c2, s2 = np.cos(th2), np.sin(th2)
    m2 = np.concatenate([np.concatenate([c2, s2], axis=2), np.concatenate([-s2, c2], axis=2)], axis=1)
    return f1.astype(np.float32), m2.astype(np.float32)


def _channel_tables(n_seq):
    c = np.arange(GROUP_C)
    th = 2.0 * np.pi * np.outer(c, c) / GROUP_C
    s = 1.0 / np.sqrt(float(n_seq) * GROUP_C)
    return (np.cos(th) * s).astype(np.float32), (np.sin(th) * s).astype(np.float32)


def _fourier_mixer(hs, mod_l, w_out, b_out, n_batch, seq, n_ctx, with_ctx):
    rows = seq // GRID_W
    assert rows == GRID_W, "the two-stage DFT is written for a square token grid"
    n_lat = n_batch * seq
    f1, m2 = _dft_tables(rows)
    cc, sc = _channel_tables(seq)
    f1, m2, cc, sc = (jnp.asarray(t).astype(BF16) for t in (f1, m2, cc, sc))
    w = w_out.astype(BF16)
    n_ct = GRID_W // COL_TILE
    grid3 = hs.reshape(hs.shape[0] // GRID_W, GRID_W, D)
    z_shape = jax.ShapeDtypeStruct((n_batch * rows, GRID_W, D), BF16)
    blk_cols = pl.BlockSpec((rows, COL_TILE, D), lambda b, c: (b, c, 0))
    mod_b = pl.BlockSpec((None, N_MOD, D), lambda b, c: (b, 0, 0))
    zr, zi = pl.pallas_call(
        _fnet_rows_kernel,
        grid=(n_batch, n_ct),
        in_specs=[blk_cols, mod_b, pl.BlockSpec(f1.shape, lambda b, c: (0, 0))],
        out_specs=[blk_cols, blk_cols],
        out_shape=[z_shape, z_shape],
        compiler_params=_cparams(("arbitrary", "arbitrary")),
        name="fnet_rows",
    )(grid3, mod_l, f1)
    n_kt = GRID_W // SUBLANE
    blk_k2 = pl.BlockSpec((SUBLANE, GRID_W, D), lambda b, k: (b * n_kt + k, 0, 0))
    blk_tok = pl.BlockSpec((rows, SUBLANE, D), lambda b, k: (b, k, 0))
    table = pl.BlockSpec((GROUP_C, GROUP_C), lambda b, k: (0, 0))
    n_tok = rows * SUBLANE
    grid3 = pl.pallas_call(
        _fnet_cols_kernel,
        grid=(n_batch, n_kt),
        in_specs=[blk_k2, blk_k2,
                  pl.BlockSpec((SUBLANE, 2 * rows, 2 * GRID_W), lambda b, k: (k, 0, 0)),
                  table, table,
                  pl.BlockSpec((D, D), lambda b, k: (0, 0)),
                  pl.BlockSpec((1, D), lambda b, k: (0, 0)),
                  blk_tok, mod_b],
        out_specs=blk_tok,
        out_shape=jax.ShapeDtypeStruct(grid3.shape, F32),
        scratch_shapes=[pltpu.VMEM((n_tok, D), BF16), pltpu.VMEM((n_tok, D), BF16),
                        pltpu.VMEM((D // LANE, n_tok, LANE), F32)],
        input_output_aliases={7: 0},
        compiler_params=_cparams(("arbitrary", "arbitrary")),
        name="fnet_cols",
    )(zr, zi, m2, cc, sc, w, b_out.reshape(1, D), grid3, mod_l)
    hs = grid3.reshape(hs.shape)
    tpb = seq // TM_LIN
    row_fn = lambda t: _mod_row(t, tpb, n_batch)
    if with_ctx:
        a = np.arange(n_ctx)
        th = 2.0 * np.pi * np.outer(a, a) / n_ctx
        s = np.sqrt(float(seq) / n_ctx)
        fc = jnp.asarray(np.concatenate([np.cos(th) * s, -np.sin(th) * s], axis=0).astype(np.float32)).astype(BF16)
        c_shape = jax.ShapeDtypeStruct((n_batch * n_ctx, D), BF16)
        tile0 = n_lat // n_ctx
        xr, xi = pl.pallas_call(
            _fnet_ctx_kernel,
            grid=(n_batch,),
            in_specs=[pl.BlockSpec((n_ctx, D), lambda b: (tile0 + b, 0)),
                      pl.BlockSpec((None, N_MOD, D), lambda b: (n_batch, 0, 0)),
                      pl.BlockSpec((2 * n_ctx, n_ctx), lambda b: (0, 0))],
            out_specs=[pl.BlockSpec((n_ctx, D), lambda b: (b, 0))] * 2,
            out_shape=[c_shape, c_shape],
            compiler_params=_cparams(("arbitrary",)),
            name="fnet_ctx",
        )(hs, mod_l, fc)
        hs = _lin_res_call(hs, (xr, xi), (cc, sc), w, b_out, mod_l,
                           n_lat // TM_LIN, n_batch * n_ctx // TM_LIN, row_fn)
    return hs


def _mixer_input_chunks(h_ref, mod_ref, xn_ref, first):
    for r0 in range(0, h_ref.shape[0], PROJ_ROWS):
        rows = slice(r0, r0 + PROJ_ROWS)
        if first:
            xn = _norm_mod(h_ref[rows, :], mod_ref[3:4, :], mod_ref[4:5, :]).astype(BF16)
            xn_ref[rows, :] = xn
        else:
            xn = xn_ref[rows, :]
        yield rows, xn


def _glu_kernel(h_ref, mod_ref, wa_ref, wg_ref, ba_ref, bg_ref, o_ref, xn_ref):
    def step(first):
        for rows, xn in _mixer_input_chunks(h_ref, mod_ref, xn_ref, first):
            a = _dot(xn, wa_ref[...]) + ba_ref[...]
            g = _dot(xn, wg_ref[...]) + bg_ref[...]
            o_ref[rows, :] = a / (1.0 + jnp.exp(-g))

    j = pl.program_id(1)
    pl.when(j == 0)(functools.partial(step, True))
    pl.when(j > 0)(functools.partial(step, False))


def _conv_kernel(u_ref, up_ref, un_ref, wdw_ref, bdw_ref, lng_ref, lnb_ref, w_ref, b_ref, h_ref, mod_ref,
                 o_ref, buf_ref, conv_ref, v_ref, *, lat_tiles, tiles_per_seq):
    i = pl.program_id(0)
    is_ctx = i >= lat_tiles
    first = jnp.logical_or(is_ctx, i % tiles_per_seq == 0)
    last = jnp.logical_or(is_ctx, i % tiles_per_seq == tiles_per_seq - 1)
    tm = u_ref.shape[0]
    n_buf = tm + 2 * CONV_HALO
    buf_ref[0, 0:CONV_HALO, :] = jnp.where(first, 0.0, up_ref[...])
    buf_ref[0, CONV_HALO:CONV_HALO + tm, :] = u_ref[...]
    buf_ref[0, CONV_HALO + tm:, :] = jnp.where(last, 0.0, un_ref[...])
    for s in range(1, SUBLANE):
        buf_ref[s] = pltpu.roll(buf_ref[0], n_buf - s, axis=0)
    off = CONV_HALO - CONV_K // 2
    for lt in range(D // LANE):
        lanes = slice(lt * LANE, (lt + 1) * LANE)
        for c0 in range(0, tm, CONV_ROWS):
            acc = jnp.zeros((CONV_ROWS, LANE), F32) + bdw_ref[:, lanes]
            for k in range(CONV_K):
                row = c0 + (off + k) // SUBLANE * SUBLANE
                acc = acc + wdw_ref[k:k + 1, lanes] * buf_ref[(off + k) % SUBLANE, row:row + CONV_ROWS, lanes]
            conv_ref[c0:c0 + CONV_ROWS, lanes] = acc
    chunk = 32
    for c0 in range(0, tm, chunk):
        acc = conv_ref[c0:c0 + chunk, :]
        mu = jnp.mean(acc, axis=-1, keepdims=True)
        d = acc - mu
        var = jnp.mean(d * d, axis=-1, keepdims=True)
        v = d * lax.rsqrt(var + EPS) * lng_ref[...] + lnb_ref[...]
        v_ref[c0:c0 + chunk, :] = _silu(v).astype(BF16)
    y = _dot(v_ref[...], w_ref[...]) + b_ref[...]
    o_ref[...] = h_ref[...] + mod_ref[5:6, :] * y


def _conv_mixer(hs, mod_l, w_in, b_in, w_dw, b_dw, ln_g, ln_b, w_out, b_out, n_batch, seq, n_ctx):
    n_lat = n_batch * seq
    n_all = n_lat + n_batch * n_ctx
    assert n_ctx == TM_CONV, "context sequences are one conv tile long"
    n_tiles = n_all // TM_FFN
    n_col = D // GLU_COLS
    tpb = seq // TM_FFN
    col = pl.BlockSpec((D, GLU_COLS), lambda i, j: (0, j))
    gcol = pl.BlockSpec((D, GLU_COLS), lambda i, j: (0, n_col + j))
    u = pl.pallas_call(
        _glu_kernel,
        grid=(n_tiles, n_col),
        in_specs=[pl.BlockSpec((TM_FFN, D), lambda i, j: (i, 0)),
                  pl.BlockSpec((None, N_MOD, D), lambda i, j: (_mod_row(i, tpb, n_batch), 0, 0)),
                  col, gcol,
                  pl.BlockSpec((1, GLU_COLS), lambda i, j: (0, j)),
                  pl.BlockSpec((1, GLU_COLS), lambda i, j: (0, n_col + j))],
        out_specs=pl.BlockSpec((TM_FFN, GLU_COLS), lambda i, j: (i, j)),
        out_shape=jax.ShapeDtypeStruct((n_all, D), F32),
        scratch_shapes=[pltpu.VMEM((TM_FFN, D), BF16)],
        compiler_params=_cparams(("arbitrary", "arbitrary")),
        name="conv_glu",
    )(hs, mod_l, w_in.astype(BF16), w_in.astype(BF16), b_in.reshape(1, 2 * D), b_in.reshape(1, 2 * D))

    n_ct = n_all // TM_CONV
    hpt = TM_CONV // CONV_HALO
    n_halo = n_all // CONV_HALO
    tps = seq // TM_CONV
    vec = pl.BlockSpec((1, D), lambda i: (0, 0))
    tile = pl.BlockSpec((TM_CONV, D), lambda i: (i, 0))
    return pl.pallas_call(
        functools.partial(_conv_kernel, lat_tiles=n_lat // TM_CONV, tiles_per_seq=tps),
        grid=(n_ct,),
        in_specs=[tile,
                  pl.BlockSpec((CONV_HALO, D), lambda i: (jnp.maximum(i * hpt - 1, 0), 0)),
                  pl.BlockSpec((CONV_HALO, D), lambda i: (jnp.minimum((i + 1) * hpt, n_halo - 1), 0)),
                  pl.BlockSpec((CONV_K, D), lambda i: (0, 0)),
                  vec, vec, vec,
                  pl.BlockSpec((D, D), lambda i: (0, 0)),
                  vec, tile,
                  pl.BlockSpec((None, N_MOD, D), lambda i: (_mod_row(i, tps, n_batch), 0, 0))],
        out_specs=tile,
        out_shape=jax.ShapeDtypeStruct(hs.shape, F32),
        scratch_shapes=[pltpu.VMEM((SUBLANE, TM_CONV + 2 * CONV_HALO, D), F32), pltpu.VMEM((TM_CONV, D), F32),
                        pltpu.VMEM((TM_CONV, D), BF16)],
        input_output_aliases={9: 0},
        compiler_params=_cparams(("arbitrary",)),
        name="conv_dw",
    )(u, u, u, w_dw, b_dw.reshape(1, D), ln_g.reshape(1, D), ln_b.reshape(1, D), w_out.astype(BF16),
      b_out.reshape(1, D), hs, mod_l)


def _qkv_kernel(h_ref, mod_ref, w_ref, gain_ref, e_ref, o_ref, xn_ref):
    n_e = e_ref.shape[0]

    def step(first, head_norm):
        for rows, xn in _mixer_input_chunks(h_ref, mod_ref, xn_ref, first):
            y = _dot(xn, w_ref[...])
            if not head_norm:
                o_ref[rows, :] = y.astype(BF16)
                continue
            for c0 in range(0, D, n_e):
                cols = slice(c0, c0 + n_e)
                yc = y[:, cols]
                ss = _dot((yc * yc).astype(BF16), e_ref[...])
                rs = lax.rsqrt(ss * (1.0 / HEAD_DIM) + EPS) * gain_ref[:, cols]
                o_ref[rows, cols] = (yc * rs).astype(BF16)

    j = pl.program_id(1)
    pl.when(j == 0)(functools.partial(step, True, True))
    pl.when(j == 1)(functools.partial(step, False, True))
    pl.when(j == 2)(functools.partial(step, False, False))


def _softmax_pv(s_list, v_list):
    m = s_list[0].max(axis=-1, keepdims=True)
    for s in s_list[1:]:
        m = jnp.maximum(m, s.max(axis=-1, keepdims=True))
    den = 0.0
    acc = 0.0
    for s, v in zip(s_list, v_list):
        p = jnp.exp(s - m)
        den = den + p.sum(axis=-1, keepdims=True)
        acc = acc + _dot(p.astype(BF16), v)
    return acc / den


def _qk(q, k):
    return lax.dot_general(q, k, (((1,), (1,)), ((), ())), preferred_element_type=F32)


def _natten_kernel(q_ref, k_ref, v_ref, kc_ref, vc_ref, bias_ref, o_ref, *, rows, kr):
    r0 = pl.program_id(1) * NATTEN_ROWS
    left = lax.broadcasted_iota(jnp.int32, (GRID_W, LANE), 1) < HEAD_DIM
    n2 = 2 * GRID_W
    wins, dr0s = [], []
    for i in range(NATTEN_ROWS):
        rs = jnp.clip(r0 + i - kr // 2, 0, rows - kr)
        wins.append(pl.ds(pl.multiple_of(rs * GRID_W, GRID_W), kr * GRID_W))
        dr0s.append(rs - (r0 + i) + WIN_R - 1)
    for p in range(N_HEADS // 2):
        cols = slice(p * LANE, (p + 1) * LANE)
        q2 = []
        for i in range(NATTEN_ROWS):
            q = q_ref[i * GRID_W:(i + 1) * GRID_W, cols]
            zero = jnp.zeros_like(q)
            q2 += [jnp.where(left, q, zero), jnp.where(left, zero, q)]
        q2 = jnp.concatenate(q2, axis=0)
        s_ctx = _qk(q2, kc_ref[:, cols])
        m_ctx = s_ctx.max(axis=-1, keepdims=True)
        m, den, acc = [], [], []
        for i in range(NATTEN_ROWS):
            mine = slice(i * n2, (i + 1) * n2)
            bias = jnp.concatenate([bias_ref[p, dr0s[i] + 2 * t] for t in range(kr // 2)], axis=-1)
            s_lat = _qk(q2[mine], k_ref[wins[i], cols]) + bias
            m.append(jnp.maximum(s_lat.max(axis=-1, keepdims=True), m_ctx[mine]))
            p_lat = jnp.exp(s_lat - m[i])
            den.append(p_lat.sum(axis=-1, keepdims=True))
            acc.append(_dot(p_lat.astype(BF16), v_ref[wins[i], cols]))
        p_ctx = jnp.exp(s_ctx - jnp.concatenate(m, axis=0))
        den_ctx = p_ctx.sum(axis=-1, keepdims=True)
        acc_ctx = _dot(p_ctx.astype(BF16), vc_ref[:, cols])
        for i in range(NATTEN_ROWS):
            mine = slice(i * n2, (i + 1) * n2)
            o = (acc[i] + acc_ctx[mine]) / (den[i] + den_ctx[mine])
            o_ref[i * GRID_W:(i + 1) * GRID_W, cols] = jnp.where(left, o[:GRID_W], o[GRID_W:]).astype(BF16)


def _bias_kernel(rpb_ref, o_ref):
    qc = lax.broadcasted_iota(jnp.int32, (GRID_W, LANE), 0)
    lane = lax.broadcasted_iota(jnp.int32, (GRID_W, LANE), 1)
    kc = lane % GRID_W
    w0 = jnp.clip(qc - WIN_C // 2, 0, GRID_W - WIN_C)
    ok = jnp.logical_and(kc >= w0, kc < w0 + WIN_C)
    sub = lax.broadcasted_iota(jnp.int32, (SUBLANE, LANE), 0)
    tiles = []
    for dr in range(2 * WIN_R - 1):
        t = jnp.broadcast_to(rpb_ref[dr:dr + 1, :], (SUBLANE, LANE))
        t = pltpu.roll(t, LANE - (WIN_C - 1), axis=1)
        for bit in range(SUBLANE.bit_length() - 1):
            t = jnp.where((sub >> bit) & 1 == 1, pltpu.roll(t, 1 << bit, axis=1), t)
        tiles.append(jnp.concatenate([t] + [pltpu.roll(t, SUBLANE * a, axis=1)
                                            for a in range(1, GRID_W // SUBLANE)], axis=0))
    for p in range(2 * WIN_R - 2):
        pair = jnp.where(lane < GRID_W, tiles[p], pltpu.roll(tiles[p + 1], GRID_W, axis=1))
        o_ref[p] = jnp.where(ok, pair, NEG_INF)


def _ctx_attn_kernel(q_ref, k_ref, v_ref, o_ref):
    outs = []
    for h in range(N_HEADS):
        hd = slice(h * HEAD_DIM, (h + 1) * HEAD_DIM)
        outs.append(_softmax_pv([_qk(q_ref[:, hd], k_ref[:, hd])], [v_ref[:, hd]]))
    o_ref[...] = jnp.concatenate(outs, axis=-1).astype(BF16)


def _natten_bias(rpb):
    n_dr, n_dc = 2 * WIN_R - 1, 2 * WIN_C - 1
    assert rpb.shape == (N_HEADS, n_dr, n_dc) and GRID_W * 2 == LANE
    padded = jnp.pad(rpb.astype(F32), ((0, 0), (0, 0), (0, LANE - n_dc)))
    return pl.pallas_call(
        _bias_kernel,
        grid=(N_HEADS,),
        in_specs=[pl.BlockSpec((None, n_dr, LANE), lambda h: (h, 0, 0))],
        out_specs=pl.BlockSpec((None, n_dr - 1, GRID_W, LANE), lambda h: (h // 2, 0, h % 2, 0)),
        out_shape=jax.ShapeDtypeStruct((N_HEADS // 2, n_dr - 1, 2 * GRID_W, LANE), F32),
        compiler_params=_cparams(("arbitrary",)),
        name="natten_bias",
    )(padded)


def _natten_mixer(hs, mod_l, w_qkv, q_gain, k_gain, rpb, w_o, n_batch, seq, n_ctx):
    n_lat = n_batch * seq
    n_all = n_lat + n_batch * n_ctx
    rows = seq // GRID_W
    kr = min(WIN_R, rows)
    n_tiles = n_all // TM_FFN
    tpb = seq // TM_FFN
    scale = HEAD_DIM ** -0.5
    gain = jnp.concatenate([jnp.tile(q_gain, N_HEADS) * scale, jnp.tile(k_gain, N_HEADS),
                            jnp.ones((D,), F32)]).reshape(1, 3 * D)
    lane = np.arange(FF_TILE)
    e = jnp.asarray((lane[:, None] // HEAD_DIM == lane[None, :] // HEAD_DIM).astype(np.float32)).astype(BF16)
    qkv = pl.pallas_call(
        _qkv_kernel,
        grid=(n_tiles, 3),
        in_specs=[pl.BlockSpec((TM_FFN, D), lambda i, j: (i, 0)),
                  pl.BlockSpec((None, N_MOD, D), lambda i, j: (_mod_row(i, tpb, n_batch), 0, 0)),
                  pl.BlockSpec((D, D), lambda i, j: (0, j)),
                  pl.BlockSpec((1, D), lambda i, j: (0, j)),
                  pl.BlockSpec((FF_TILE, FF_TILE), lambda i, j: (0, 0))],
        out_specs=pl.BlockSpec((TM_FFN, D), lambda i, j: (i, j)),
        out_shape=jax.ShapeDtypeStruct((n_all, 3 * D), BF16),
        scratch_shapes=[pltpu.VMEM((TM_FFN, D), BF16)],
        compiler_params=_cparams(("arbitrary", "arbitrary")),
        name="qkv",
    )(hs, mod_l, w_qkv.astype(BF16), gain, e)

    assert kr == WIN_R, "bias tiles pair up the key rows of a full window"
    bias = _natten_bias(rpb)
    ctx0 = n_lat // n_ctx
    once = pl.Buffered(1)
    n_rb = rows // NATTEN_ROWS
    o_lat = pl.pallas_call(
        functools.partial(_natten_kernel, rows=rows, kr=kr),
        grid=(n_batch, n_rb),
        in_specs=[pl.BlockSpec((NATTEN_ROWS * GRID_W, D), lambda b, r: (b * n_rb + r, 0)),
                  pl.BlockSpec((seq, D), lambda b, r: (b, 1)),
                  pl.BlockSpec((seq, D), lambda b, r: (b, 2), pipeline_mode=once),
                  pl.BlockSpec((n_ctx, D), lambda b, r: (ctx0 + b, 1)),
                  pl.BlockSpec((n_ctx, D), lambda b, r: (ctx0 + b, 2)),
                  pl.BlockSpec(bias.shape, lambda b, r: (0, 0, 0, 0), pipeline_mode=once)],
        out_specs=pl.BlockSpec((NATTEN_ROWS * GRID_W, D), lambda b, r: (b * n_rb + r, 0)),
        out_shape=jax.ShapeDtypeStruct((n_lat, D), BF16),
        compiler_params=_cparams(("arbitrary", "arbitrary")),
        name="natten",
    )(qkv, qkv, qkv, qkv, qkv, bias)
    o_ctx = pl.pallas_call(
        _ctx_attn_kernel,
        grid=(n_batch,),
        in_specs=[pl.BlockSpec((n_ctx, D), lambda b: (ctx0 + b, 0)),
                  pl.BlockSpec((n_ctx, D), lambda b: (ctx0 + b, 1)),
                  pl.BlockSpec((n_ctx, D), lambda b: (ctx0 + b, 2))],
        out_specs=pl.BlockSpec((n_ctx, D), lambda b: (b, 0)),
        out_shape=jax.ShapeDtypeStruct((n_batch * n_ctx, D), BF16),
        compiler_params=_cparams(("arbitrary",)),
        name="ctx_attn",
    )(qkv, qkv, qkv)
    w = w_o.astype(BF16)
    zero_b = jnp.zeros((D,), F32)
    row_fn = lambda t: _mod_row(t, seq // TM_LIN, n_batch)
    hs = _lin_res_call(hs, (o_lat,), (), w, zero_b, mod_l, 0, n_lat // TM_LIN, row_fn)
    return _lin_res_call(hs, (o_ctx,), (), w, zero_b, mod_l, n_lat // TM_LIN, n_batch * n_ctx // TM_LIN, row_fn)


def kernel(x, c, ctx, c_ctx, w_mod, b_mod, w_ff_in, w_ff_out, fnet_w_out, fnet_b_out, conv_w_in, conv_b_in, conv_w_dw, conv_b_dw, conv_ln_g, conv_ln_b, conv_w_out, conv_b_out, na_w_qkv, na_q_gain, na_k_gain, na_rpb, na_w_o):
    n_batch, seq, d = x.shape
    n_ctx = ctx.shape[1]
    depth = w_mod.shape[0]
    assert d == D and n_batch < MOD_ROWS
    assert seq % TM_FFN == 0 and (n_batch * n_ctx) % TM_FFN == 0
    n_lat = n_batch * seq

    cond = jnp.concatenate([c, c_ctx[None, :], jnp.zeros((MOD_ROWS - n_batch - 1, D), F32)], axis=0)
    mod = _mod_call(cond, w_mod, b_mod).reshape(depth, MOD_ROWS, N_MOD, D)
    n_tok = n_lat + n_batch * n_ctx
    n_stream = n_tok + (-n_tok % seq)
    assert depth > 1, "the first half-step is written for a layer that updates the context tokens"
    lat_tiles = n_lat // TM_FFN
    ctx_tiles = n_batch * n_ctx // TM_FFN
    wide_tiles = n_lat // TM_FFN_WIDE
    assert n_lat % TM_FFN_WIDE == 0 and (n_stream - n_lat) % TM_FFN_WIDE == 0
    w_gu, w_o = w_ff_in, w_ff_out

    def half_step(hs, layer, which, k0, with_ctx, in_place):
        args = (mod[layer], w_gu, w_o, layer, which, k0)
        hs = _ffn_call(hs, *args, wide_tiles, n_batch, seq, tm=TM_FFN_WIDE, in_place=in_place)
        if with_ctx:
            hs = _ffn_call(hs, *args, ctx_tiles, n_batch, seq, tm=TM_FFN, tile0=lat_tiles)
        return hs

    for i in range(depth):
        kind, slot = i % 3, i // 3
        ctx_post = i < depth - 1
        ctx_pre = ctx_post or kind == 2
        mod_l = mod[i]
        if i == 0:
            args = (mod_l, w_gu, w_o, 0, 0, 0)
            hs = _ffn_call(x.reshape(n_lat, D), *args, wide_tiles, n_batch, seq, tm=TM_FFN_WIDE, in_place=False,
                           out_rows=n_stream, zero_tiles=(n_stream - n_lat) // TM_FFN_WIDE)
            hs = _ffn_call(ctx.reshape(n_batch * n_ctx, D), *args, ctx_tiles, n_batch, seq, tm=TM_FFN,
                           tile0=lat_tiles, in_place=False, stream=hs)
        else:
            hs = half_step(hs, i, 0, 0, ctx_pre, True)
        if kind == 0:
            hs = _fourier_mixer(hs, mod_l, fnet_w_out[slot], fnet_b_out[slot], n_batch, seq, n_ctx, ctx_post)
        elif kind == 1:
            assert ctx_post
            hs = _conv_mixer(hs, mod_l, conv_w_in[slot], conv_b_in[slot], conv_w_dw[slot], conv_b_dw[slot],
                             conv_ln_g[slot], conv_ln_b[slot], conv_w_out[slot], conv_b_out[slot],
                             n_batch, seq, n_ctx)
        else:
            assert ctx_post
            hs = _natten_mixer(hs, mod_l, na_w_qkv[slot], na_q_gain[slot], na_k_gain[slot], na_rpb[slot],
                               na_w_o[slot], n_batch, seq, n_ctx)
        final = i == depth - 1
        assert ctx_post != final
        hs = half_step(hs, i, 1, 6, ctx_post, not final)
    return hs[:n_lat].reshape(n_batch, seq, D)
```

```python
import functools

import numpy as np
import jax
import jax.numpy as jnp
from jax import lax
from jax.experimental import pallas as pl
from jax.experimental.pallas import tpu as pltpu

F32 = jnp.float32
BF16 = jnp.bfloat16

D = 1024
D_FF = 2816
N_MOD = 9
EPS = 1e-6
GRID_W = 64
LANE = 128
SUBLANE = 8
N_GROUPS = 4
GROUP_C = D // N_GROUPS
CONV_K = 31
CONV_HALO = 16
N_HEADS = 16
HEAD_DIM = D // N_HEADS
WIN_R = 8
WIN_C = 16
NEG_INF = -1e30
MOD_ROWS = 8
MOD_COLS = 3 * D

FF_TILE = 256
FF_PER_STEP = 2
TM_FFN = 1024
PROJ_ROWS = 256
GLU_COLS = 1024
FFN_ROWS = 256
TM_LIN = 1024
TM_CONV = 256
CONV_ROWS = 32
COL_TILE = 16
NATTEN_ROWS = 4
VMEM_LIMIT = 52 * 1024 * 1024
VMEM_LIMIT_WIDE = 58 * 1024 * 1024
TM_FFN_WIDE = 2048


def _cparams(sem):
    return pltpu.CompilerParams(dimension_semantics=sem, vmem_limit_bytes=VMEM_LIMIT)


def _silu(x):
    return x / (1.0 + jnp.exp(-x))


def _norm_mod(x, shift, scale):
    ms = jnp.mean(x * x, axis=-1, keepdims=True)
    return x * lax.rsqrt(ms + EPS) * (1.0 + scale) + shift


def _dot(a, b):
    return jnp.dot(a, b, preferred_element_type=F32)


def _mod_kernel(c_ref, w_ref, b_ref, o_ref):
    s = _silu(c_ref[...]).astype(BF16)
    o_ref[...] = _dot(s, w_ref[...].astype(BF16)) + b_ref[...]


def _mod_call(cond, w_mod, b_mod):
    depth = w_mod.shape[0]
    return pl.pallas_call(
        _mod_kernel,
        grid=(depth, N_MOD * D // MOD_COLS),
        in_specs=[
            pl.BlockSpec((MOD_ROWS, D), lambda l, k: (0, 0)),
            pl.BlockSpec((None, D, MOD_COLS), lambda l, k: (l, 0, k)),
            pl.BlockSpec((None, 1, MOD_COLS), lambda l, k: (l, 0, k)),
        ],
        out_specs=pl.BlockSpec((None, MOD_ROWS, MOD_COLS), lambda l, k: (l, 0, k)),
        out_shape=jax.ShapeDtypeStruct((depth, MOD_ROWS, N_MOD * D), F32),
        compiler_params=_cparams(("arbitrary", "arbitrary")),
        name="mod",
    )(cond, w_mod, b_mod.reshape(depth, 1, N_MOD * D))


def _mod_row(tile, tiles_per_batch, n_batch):
    return jnp.minimum(tile // tiles_per_batch, n_batch)


def _ffn_kernel(*refs, k0, n_ff, n_real, in_stream):
    h_ref, mod_ref = refs[0], refs[2 if in_stream else 1]
    w_refs = refs[3 if in_stream else 2:-2]
    o_ref, xn_ref = refs[-2:]
    wg_refs, wu_refs, wo_refs = (w_refs[t * FF_PER_STEP:(t + 1) * FF_PER_STEP] for t in range(3))
    tokens = lambda rows: h_ref[rows, :]
    j = pl.program_id(1)
    n_chunks = o_ref.shape[0] // FFN_ROWS
    n_steps = pl.cdiv(n_ff, FF_PER_STEP)
    assert n_steps >= 2

    def step(first, last):
        n_sub = n_ff - FF_PER_STEP * (n_steps - 1) if last else FF_PER_STEP
        wg, wu, wo = ([r[...].astype(BF16) for r in w[:n_sub]] for w in (wg_refs, wu_refs, wo_refs))
        gu = {}
        for c in range(n_chunks + 1):
            if c < n_chunks:
                rows = slice(c * FFN_ROWS, (c + 1) * FFN_ROWS)
                if first:
                    xn = _norm_mod(tokens(rows), mod_ref[k0:k0 + 1, :], mod_ref[k0 + 1:k0 + 2, :]).astype(BF16)
                    xn_ref[rows, :] = xn
                else:
                    xn = xn_ref[rows, :]
                gu[c] = [(_dot(xn, wg[t]), _dot(xn, wu[t])) for t in range(n_sub)]
            if c >= 1:
                rows = slice((c - 1) * FFN_ROWS, c * FFN_ROWS)
                y = None
                for t, (g, u) in enumerate(gu.pop(c - 1)):
                    yt = _dot((_silu(g) * u).astype(BF16), wo[t])
                    y = yt if y is None else y + yt
                if first:
                    o_ref[rows, :] = y
                elif last:
                    o_ref[rows, :] = tokens(rows) + (0.5 * mod_ref[k0 + 2:k0 + 3, :]) * (o_ref[rows, :] + y)
                else:
                    o_ref[rows, :] += y

    real = pl.program_id(0) < n_real
    pl.when(jnp.logical_and(real, j == 0))(functools.partial(step, True, False))
    pl.when(jnp.logical_and(real, jnp.logical_and(j > 0, j < n_steps - 1)))(functools.partial(step, False, False))
    pl.when(jnp.logical_and(real, j == n_steps - 1))(functools.partial(step, False, True))

    @pl.when(jnp.logical_and(jnp.logical_not(real), j == n_steps - 1))
    def _():
        o_ref[...] = jnp.zeros_like(o_ref)


def _ffn_call(tok, mod_l, w_gu, w_o, layer, which, k0, n_tiles, n_batch, seq, *, tm, tile0=0, in_place=True,
              stream=None, out_rows=None, zero_tiles=0):
    n_ff = D_FF // FF_TILE
    tpb = seq // tm
    assert not (in_place and stream is not None) and (in_place or stream is not None or tile0 == 0)
    assert zero_tiles == 0 or not in_place
    tok_tile0 = tile0 if in_place else 0
    n_steps = pl.cdiv(n_ff, FF_PER_STEP)

    def ff(t):
        return lambda i, j: jnp.where(i < n_tiles, jnp.minimum(FF_PER_STEP * j + t, n_ff - 1), n_ff - 1)

    subs = range(FF_PER_STEP)
    last_tok = tok.shape[0] // tm - 1
    operands = [tok] + ([stream] if stream is not None else [])
    specs = [pl.BlockSpec((tm, D), lambda i, j: (jnp.minimum(i + tok_tile0, last_tok), 0))]
    if stream is not None:
        specs.append(pl.BlockSpec(memory_space=pl.ANY))
        out_rows = stream.shape[0]
    elif in_place:
        out_rows = tok.shape[0]
    elif out_rows is None:
        out_rows = (n_tiles + zero_tiles) * tm
    mod_row = lambda i: _mod_row(jnp.minimum(i, n_tiles - 1) + tile0, tpb, n_batch)
    return pl.pallas_call(
        functools.partial(_ffn_kernel, k0=k0, n_ff=n_ff, n_real=n_tiles, in_stream=stream is not None),
        grid=(n_tiles + zero_tiles, n_steps),
        in_specs=specs + [pl.BlockSpec((None, N_MOD, D), lambda i, j: (mod_row(i), 0, 0))]
        + [pl.BlockSpec((None, None, D, FF_TILE), lambda i, j, f=ff(t): (layer, which, 0, f(i, j))) for t in subs]
        + [pl.BlockSpec((None, None, D, FF_TILE), lambda i, j, f=ff(t): (layer, which, 0, n_ff + f(i, j)))
           for t in subs]
        + [pl.BlockSpec((None, None, FF_TILE, D), lambda i, j, f=ff(t): (layer, which, f(i, j), 0)) for t in subs],
        out_specs=pl.BlockSpec((tm, D), lambda i, j: (i + tile0, 0)),
        out_shape=jax.ShapeDtypeStruct((out_rows, D), F32),
        scratch_shapes=[pltpu.VMEM((tm, D), BF16)],
        input_output_aliases={0: 0} if in_place else ({1: 0} if stream is not None else {}),
        compiler_params=pltpu.CompilerParams(dimension_semantics=("arbitrary", "arbitrary"),
                                             vmem_limit_bytes=VMEM_LIMIT_WIDE if tm > TM_FFN else VMEM_LIMIT),
        name="ffn",
    )(*operands, mod_l, *([w_gu] * (2 * FF_PER_STEP)), *([w_o] * FF_PER_STEP))


def _lin_res_kernel(*refs, fourier):
    if fourier:
        xr_ref, xi_ref, cc_ref, sc_ref, w_ref, b_ref, h_ref, mod_ref, o_ref = refs
        parts = []
        for g in range(N_GROUPS):
            sl = slice(g * GROUP_C, (g + 1) * GROUP_C)
            parts.append(_dot(xr_ref[:, sl], cc_ref[...]) + _dot(xi_ref[:, sl], sc_ref[...]))
        x = jnp.concatenate(parts, axis=-1).astype(BF16)
    else:
        x_ref, w_ref, b_ref, h_ref, mod_ref, o_ref = refs
        x = x_ref[...]
    y = _dot(x, w_ref[...]) + b_ref[...]
    o_ref[...] = h_ref[...] + mod_ref[5:6, :] * y


def _lin_res_call(hs, xs, tables, w, b, mod_l, tile0, n_tiles, mod_row_fn):
    fourier = len(xs) == 2
    x_spec = pl.BlockSpec((TM_LIN, D), lambda i: (i, 0))
    t_spec = pl.BlockSpec((GROUP_C, GROUP_C), lambda i: (0, 0))
    in_specs = [x_spec] * len(xs) + [t_spec] * len(tables) + [
        pl.BlockSpec((D, D), lambda i: (0, 0)),
        pl.BlockSpec((1, D), lambda i: (0, 0)),
        pl.BlockSpec((TM_LIN, D), lambda i: (i + tile0, 0)),
        pl.BlockSpec((None, N_MOD, D), lambda i: (mod_row_fn(i + tile0), 0, 0)),
    ]
    h_index = len(xs) + len(tables) + 2
    return pl.pallas_call(
        functools.partial(_lin_res_kernel, fourier=fourier),
        grid=(n_tiles,),
        in_specs=in_specs,
        out_specs=pl.BlockSpec((TM_LIN, D), lambda i: (i + tile0, 0)),
        out_shape=jax.ShapeDtypeStruct(hs.shape, F32),
        input_output_aliases={h_index: 0},
        compiler_params=_cparams(("arbitrary",)),
        name="lin_res_fourier" if fourier else "lin_res",
    )(*xs, *tables, w, b.reshape(1, D), hs, mod_l)


def _fnet_rows_kernel(h_ref, mod_ref, kf_ref, zr_ref, zi_ref):
    rows = h_ref.shape[0]
    half = rows // 2
    xn = _norm_mod(h_ref[...], mod_ref[3:4, :], mod_ref[4:5, :])
    n = rows * SUBLANE
    n_cos = (half + 1) * SUBLANE
    zr, zi = [], []
    for c0 in range(0, COL_TILE, SUBLANE):
        a = xn[:, c0:c0 + SUBLANE, :].reshape(n, D).astype(BF16)
        z = _dot(kf_ref[...], a)
        zc = z[:n_cos].reshape(half + 1, SUBLANE, D)
        zs = z[n_cos:].reshape(half - 1, SUBLANE, D)
        zero = jnp.zeros((1, SUBLANE, D), F32)
        zr.append(jnp.concatenate([zc] + [zc[rows - k:rows - k + 1] for k in range(half + 1, rows)], axis=0))
        zi.append(jnp.concatenate([zero, zs, zero] + [-zs[rows - k - 1:rows - k] for k in range(half + 1, rows)],
                                  axis=0))
    zr_ref[...] = jnp.concatenate(zr, axis=1).astype(BF16)
    zi_ref[...] = jnp.concatenate(zi, axis=1).astype(BF16)


def _fnet_cols_kernel(zr_ref, zi_ref, m2_ref, cc_ref, sc_ref, w_ref, b_ref, h_ref, mod_ref, o_ref,
                      xr_ref, xi_ref, y_ref):
    n = zr_ref.shape[1]
    for j in range(SUBLANE):
        z = jnp.concatenate([zr_ref[j], zi_ref[j]], axis=0)
        x = _dot(m2_ref[j], z)
        xr_ref[j * n:(j + 1) * n, :] = x[:n].astype(BF16)
        xi_ref[j * n:(j + 1) * n, :] = x[n:].astype(BF16)
    parts = []
    for g in range(N_GROUPS):
        sl = slice(g * GROUP_C, (g + 1) * GROUP_C)
        parts.append(_dot(xr_ref[:, sl], cc_ref[...]) + _dot(xi_ref[:, sl], sc_ref[...]))
    y = _dot(jnp.concatenate(parts, axis=-1).astype(BF16), w_ref[...]) + b_ref[...]
    for j in range(SUBLANE):
        rows = pl.ds(j, n, stride=SUBLANE)
        for lt in range(D // LANE):
            y_ref[lt, rows, :] = y[j * n:(j + 1) * n, lt * LANE:(lt + 1) * LANE]
    y_t = jnp.concatenate([y_ref[lt] for lt in range(D // LANE)], axis=-1).reshape(n, SUBLANE, D)
    o_ref[...] = h_ref[...] + mod_ref[5:6, :] * y_t


def _fnet_ctx_kernel(h_ref, mod_ref, f_ref, xr_ref, xi_ref):
    n = h_ref.shape[0]
    xn = _norm_mod(h_ref[...], mod_ref[3:4, :], mod_ref[4:5, :]).astype(BF16)
    x = _dot(f_ref[...], xn)
    xr_ref[...] = x[:n].astype(BF16)
    xi_ref[...] = x[n:].astype(BF16)


def _dft_tables(rows):
    n_seq = rows * GRID_W
    a = np.arange(rows)
    th1 = 2.0 * np.pi * np.outer(a, a) / rows
    half = rows // 2
    f1 = np.concatenate([np.cos(th1)[:half + 1], -np.sin(th1)[1:half]], axis=0)
    f1 = np.kron(f1, np.eye(SUBLANE))
    k2 = np.arange(GRID_W)[:, None, None]
    k1 = np.arange(rows)[None, :, None]
    n1 = np.arange(GRID_W)[None, None, :]
    th2 = 2.0 * np.pi * (n1 * (GRID_W * k1 + k2) % n_seq) / n_seq
    c2, s2 = np.cos(th2), np.sin(th2)
    m2 = np.concatenate([np.concatenate([c2, s2], axis=2), np.concatenate([-s2, c2], axis=2)], axis=1)
    return f1.astype(np.float32), m2.astype(np.float32)


def _channel_tables(n_seq):
    c = np.arange(GROUP_C)
    th = 2.0 * np.pi * np.outer(c, c) / GROUP_C
    s = 1.0 / np.sqrt(float(n_seq) * GROUP_C)
    return (np.cos(th) * s).astype(np.float32), (np.sin(th) * s).astype(np.float32)


def _fourier_mixer(hs, mod_l, w_out, b_out, n_batch, seq, n_ctx, with_ctx):
    rows = seq // GRID_W
    assert rows == GRID_W, "the two-stage DFT is written for a square token grid"
    n_lat = n_batch * seq
    f1, m2 = _dft_tables(rows)
    cc, sc = _channel_tables(seq)
    f1, m2, cc, sc = (jnp.asarray(t).astype(BF16) for t in (f1, m2, cc, sc))
    w = w_out.astype(BF16)
    n_ct = GRID_W // COL_TILE
    grid3 = hs.reshape(hs.shape[0] // GRID_W, GRID_W, D)
    z_shape = jax.ShapeDtypeStruct((n_batch * rows, GRID_W, D), BF16)
    blk_cols = pl.BlockSpec((rows, COL_TILE, D), lambda b, c: (b, c, 0))
    mod_b = pl.BlockSpec((None, N_MOD, D), lambda b, c: (b, 0, 0))
    zr, zi = pl.pallas_call(
        _fnet_rows_kernel,
        grid=(n_batch, n_ct),
        in_specs=[blk_cols, mod_b, pl.BlockSpec(f1.shape, lambda b, c: (0, 0))],
        out_specs=[blk_cols, blk_cols],
        out_shape=[z_shape, z_shape],
        compiler_params=_cparams(("arbitrary", "arbitrary")),
        name="fnet_rows",
    )(grid3, mod_l, f1)
    n_kt = GRID_W // SUBLANE
    blk_k2 = pl.BlockSpec((SUBLANE, GRID_W, D), lambda b, k: (b * n_kt + k, 0, 0))
    blk_tok = pl.BlockSpec((rows, SUBLANE, D), lambda b, k: (b, k, 0))
    table = pl.BlockSpec((GROUP_C, GROUP_C), lambda b, k: (0, 0))
    n_tok = rows * SUBLANE
    grid3 = pl.pallas_call(
        _fnet_cols_kernel,
        grid=(n_batch, n_kt),
        in_specs=[blk_k2, blk_k2,
                  pl.BlockSpec((SUBLANE, 2 * rows, 2 * GRID_W), lambda b, k: (k, 0, 0)),
                  table, table,
                  pl.BlockSpec((D, D), lambda b, k: (0, 0)),
                  pl.BlockSpec((1, D), lambda b, k: (0, 0)),
                  blk_tok, mod_b],
        out_specs=blk_tok,
        out_shape=jax.ShapeDtypeStruct(grid3.shape, F32),
        scratch_shapes=[pltpu.VMEM((n_tok, D), BF16), pltpu.VMEM((n_tok, D), BF16),
                        pltpu.VMEM((D // LANE, n_tok, LANE), F32)],
        input_output_aliases={7: 0},
        compiler_params=_cparams(("arbitrary", "arbitrary")),
        name="fnet_cols",
    )(zr, zi, m2, cc, sc, w, b_out.reshape(1, D), grid3, mod_l)
    hs = grid3.reshape(hs.shape)
    tpb = seq // TM_LIN
    row_fn = lambda t: _mod_row(t, tpb, n_batch)
    if with_ctx:
        a = np.arange(n_ctx)
        th = 2.0 * np.pi * np.outer(a, a) / n_ctx
        s = np.sqrt(float(seq) / n_ctx)
        fc = jnp.asarray(np.concatenate([np.cos(th) * s, -np.sin(th) * s], axis=0).astype(np.float32)).astype(BF16)
        c_shape = jax.ShapeDtypeStruct((n_batch * n_ctx, D), BF16)
        tile0 = n_lat // n_ctx
        xr, xi = pl.pallas_call(
            _fnet_ctx_kernel,
            grid=(n_batch,),
            in_specs=[pl.BlockSpec((n_ctx, D), lambda b: (tile0 + b, 0)),
                      pl.BlockSpec((None, N_MOD, D), lambda b: (n_batch, 0, 0)),
                      pl.BlockSpec((2 * n_ctx, n_ctx), lambda b: (0, 0))],
            out_specs=[pl.BlockSpec((n_ctx, D), lambda b: (b, 0))] * 2,
            out_shape=[c_shape, c_shape],
            compiler_params=_cparams(("arbitrary",)),
            name="fnet_ctx",
        )(hs, mod_l, fc)
        hs = _lin_res_call(hs, (xr, xi), (cc, sc), w, b_out, mod_l,
                           n_lat // TM_LIN, n_batch * n_ctx // TM_LIN, row_fn)
    return hs


def _mixer_input_chunks(h_ref, mod_ref, xn_ref, first):
    for r0 in range(0, h_ref.shape[0], PROJ_ROWS):
        rows = slice(r0, r0 + PROJ_ROWS)
        if first:
            xn = _norm_mod(h_ref[rows, :], mod_ref[3:4, :], mod_ref[4:5, :]).astype(BF16)
            xn_ref[rows, :] = xn
        else:
            xn = xn_ref[rows, :]
        yield rows, xn


def _glu_kernel(h_ref, mod_ref, wa_ref, wg_ref, ba_ref, bg_ref, o_ref, xn_ref):
    def step(first):
        for rows, xn in _mixer_input_chunks(h_ref, mod_ref, xn_ref, first):
            a = _dot(xn, wa_ref[...]) + ba_ref[...]
            g = _dot(xn, wg_ref[...]) + bg_ref[...]
            o_ref[rows, :] = a / (1.0 + jnp.exp(-g))

    j = pl.program_id(1)
    pl.when(j == 0)(functools.partial(step, True))
    pl.when(j > 0)(functools.partial(step, False))


def _conv_kernel(u_ref, up_ref, un_ref, wdw_ref, bdw_ref, lng_ref, lnb_ref, w_ref, b_ref, h_ref, mod_ref,
                 o_ref, buf_ref, conv_ref, v_ref, *, lat_tiles, tiles_per_seq):
    i = pl.program_id(0)
    is_ctx = i >= lat_tiles
    first = jnp.logical_or(is_ctx, i % tiles_per_seq == 0)
    last = jnp.logical_or(is_ctx, i % tiles_per_seq == tiles_per_seq - 1)
    tm = u_ref.shape[0]
    n_buf = tm + 2 * CONV_HALO
    buf_ref[0, 0:CONV_HALO, :] = jnp.where(first, 0.0, up_ref[...])
    buf_ref[0, CONV_HALO:CONV_HALO + tm, :] = u_ref[...]
    buf_ref[0, CONV_HALO + tm:, :] = jnp.where(last, 0.0, un_ref[...])
    for s in range(1, SUBLANE):
        buf_ref[s] = pltpu.roll(buf_ref[0], n_buf - s, axis=0)
    off = CONV_HALO - CONV_K // 2
    for lt in range(D // LANE):
        lanes = slice(lt * LANE, (lt + 1) * LANE)
        for c0 in range(0, tm, CONV_ROWS):
            acc = jnp.zeros((CONV_ROWS, LANE), F32) + bdw_ref[:, lanes]
            for k in range(CONV_K):
                row = c0 + (off + k) // SUBLANE * SUBLANE
                acc = acc + wdw_ref[k:k + 1, lanes] * buf_ref[(off + k) % SUBLANE, row:row + CONV_ROWS, lanes]
            conv_ref[c0:c0 + CONV_ROWS, lanes] = acc
    chunk = 32
    for c0 in range(0, tm, chunk):
        acc = conv_ref[c0:c0 + chunk, :]
        mu = jnp.mean(acc, axis=-1, keepdims=True)
        d = acc - mu
        var = jnp.mean(d * d, axis=-1, keepdims=True)
        v = d * lax.rsqrt(var + EPS) * lng_ref[...] + lnb_ref[...]
        v_ref[c0:c0 + chunk, :] = _silu(v).astype(BF16)
    y = _dot(v_ref[...], w_ref[...]) + b_ref[...]
    o_ref[...] = h_ref[...] + mod_ref[5:6, :] * y


def _conv_mixer(hs, mod_l, w_in, b_in, w_dw, b_dw, ln_g, ln_b, w_out, b_out, n_batch, seq, n_ctx):
    n_lat = n_batch * seq
    n_all = n_lat + n_batch * n_ctx
    assert n_ctx == TM_CONV, "context sequences are one conv tile long"
    n_tiles = n_all // TM_FFN
    n_col = D // GLU_COLS
    tpb = seq // TM_FFN
    col = pl.BlockSpec((D, GLU_COLS), lambda i, j: (0, j))
    gcol = pl.BlockSpec((D, GLU_COLS), lambda i, j: (0, n_col + j))
    u = pl.pallas_call(
        _glu_kernel,
        grid=(n_tiles, n_col),
        in_specs=[pl.BlockSpec((TM_FFN, D), lambda i, j: (i, 0)),
                  pl.BlockSpec((None, N_MOD, D), lambda i, j: (_mod_row(i, tpb, n_batch), 0, 0)),
                  col, gcol,
                  pl.BlockSpec((1, GLU_COLS), lambda i, j: (0, j)),
                  pl.BlockSpec((1, GLU_COLS), lambda i, j: (0, n_col + j))],
        out_specs=pl.BlockSpec((TM_FFN, GLU_COLS), lambda i, j: (i, j)),
        out_shape=jax.ShapeDtypeStruct((n_all, D), F32),
        scratch_shapes=[pltpu.VMEM((TM_FFN, D), BF16)],
        compiler_params=_cparams(("arbitrary", "arbitrary")),
        name="conv_glu",
    )(hs, mod_l, w_in.astype(BF16), w_in.astype(BF16), b_in.reshape(1, 2 * D), b_in.reshape(1, 2 * D))

    n_ct = n_all // TM_CONV
    hpt = TM_CONV // CONV_HALO
    n_halo = n_all // CONV_HALO
    tps = seq // TM_CONV
    vec = pl.BlockSpec((1, D), lambda i: (0, 0))
    tile = pl.BlockSpec((TM_CONV, D), lambda i: (i, 0))
    return pl.pallas_call(
        functools.partial(_conv_kernel, lat_tiles=n_lat // TM_CONV, tiles_per_seq=tps),
        grid=(n_ct,),
        in_specs=[tile,
                  pl.BlockSpec((CONV_HALO, D), lambda i: (jnp.maximum(i * hpt - 1, 0), 0)),
                  pl.BlockSpec((CONV_HALO, D), lambda i: (jnp.minimum((i + 1) * hpt, n_halo - 1), 0)),
                  pl.BlockSpec((CONV_K, D), lambda i: (0, 0)),
                  vec, vec, vec,
                  pl.BlockSpec((D, D), lambda i: (0, 0)),
                  vec, tile,
                  pl.BlockSpec((None, N_MOD, D), lambda i: (_mod_row(i, tps, n_batch), 0, 0))],
        out_specs=tile,
        out_shape=jax.ShapeDtypeStruct(hs.shape, F32),
        scratch_shapes=[pltpu.VMEM((SUBLANE, TM_CONV + 2 * CONV_HALO, D), F32), pltpu.VMEM((TM_CONV, D), F32),
                        pltpu.VMEM((TM_CONV, D), BF16)],
        input_output_aliases={9: 0},
        compiler_params=_cparams(("arbitrary",)),
        name="conv_dw",
    )(u, u, u, w_dw, b_dw.reshape(1, D), ln_g.reshape(1, D), ln_b.reshape(1, D), w_out.astype(BF16),
      b_out.reshape(1, D), hs, mod_l)


def _qkv_kernel(h_ref, mod_ref, w_ref, gain_ref, e_ref, o_ref, xn_ref):
    n_e = e_ref.shape[0]

    def step(first, head_norm):
        for rows, xn in _mixer_input_chunks(h_ref, mod_ref, xn_ref, first):
            y = _dot(xn, w_ref[...])
            if not head_norm:
                o_ref[rows, :] = y.astype(BF16)
                continue
            for c0 in range(0, D, n_e):
                cols = slice(c0, c0 + n_e)
                yc = y[:, cols]
                ss = _dot((yc * yc).astype(BF16), e_ref[...])
                rs = lax.rsqrt(ss * (1.0 / HEAD_DIM) + EPS) * gain_ref[:, cols]
                o_ref[rows, cols] = (yc * rs).astype(BF16)

    j = pl.program_id(1)
    pl.when(j == 0)(functools.partial(step, True, True))
    pl.when(j == 1)(functools.partial(step, False, True))
    pl.when(j == 2)(functools.partial(step, False, False))


def _softmax_pv(s_list, v_list):
    m = s_list[0].max(axis=-1, keepdims=True)
    for s in s_list[1:]:
        m = jnp.maximum(m, s.max(axis=-1, keepdims=True))
    den = 0.0
    acc = 0.0
    for s, v in zip(s_list, v_list):
        p = jnp.exp(s - m)
        den = den + p.sum(axis=-1, keepdims=True)
        acc = acc + _dot(p.astype(BF16), v)
    return acc / den


def _qk(q, k):
    return lax.dot_general(q, k, (((1,), (1,)), ((), ())), preferred_element_type=F32)


def _natten_kernel(q_ref, k_ref, v_ref, kc_ref, vc_ref, bias_ref, o_ref, *, rows, kr):
    r0 = pl.program_id(1) * NATTEN_ROWS
    left = lax.broadcasted_iota(jnp.int32, (GRID_W, LANE), 1) < HEAD_DIM
    n2 = 2 * GRID_W
    wins, dr0s = [], []
    for i in range(NATTEN_ROWS):
        rs = jnp.clip(r0 + i - kr // 2, 0, rows - kr)
        wins.append(pl.ds(pl.multiple_of(rs * GRID_W, GRID_W), kr * GRID_W))
        dr0s.append(rs - (r0 + i) + WIN_R - 1)
    for p in range(N_HEADS // 2):
        cols = slice(p * LANE, (p + 1) * LANE)
        q2 = []
        for i in range(NATTEN_ROWS):
            q = q_ref[i * GRID_W:(i + 1) * GRID_W, cols]
            zero = jnp.zeros_like(q)
            q2 += [jnp.where(left, q, zero), jnp.where(left, zero, q)]
        q2 = jnp.concatenate(q2, axis=0)
        s_ctx = _qk(q2, kc_ref[:, cols])
        m_ctx = s_ctx.max(axis=-1, keepdims=True)
        m, den, acc = [], [], []
        for i in range(NATTEN_ROWS):
            mine = slice(i * n2, (i + 1) * n2)
            bias = jnp.concatenate([bias_ref[p, dr0s[i] + 2 * t] for t in range(kr // 2)], axis=-1)
            s_lat = _qk(q2[mine], k_ref[wins[i], cols]) + bias
            m.append(jnp.maximum(s_lat.max(axis=-1, keepdims=True), m_ctx[mine]))
            p_lat = jnp.exp(s_lat - m[i])
            den.append(p_lat.sum(axis=-1, keepdims=True))
            acc.append(_dot(p_lat.astype(BF16), v_ref[wins[i], cols]))
        p_ctx = jnp.exp(s_ctx - jnp.concatenate(m, axis=0))
        den_ctx = p_ctx.sum(axis=-1, keepdims=True)
        acc_ctx = _dot(p_ctx.astype(BF16), vc_ref[:, cols])
        for i in range(NATTEN_ROWS):
            mine = slice(i * n2, (i + 1) * n2)
            o = (acc[i] + acc_ctx[mine]) / (den[i] + den_ctx[mine])
            o_ref[i * GRID_W:(i + 1) * GRID_W, cols] = jnp.where(left, o[:GRID_W], o[GRID_W:]).astype(BF16)


def _bias_kernel(rpb_ref, o_ref):
    qc = lax.broadcasted_iota(jnp.int32, (GRID_W, LANE), 0)
    lane = lax.broadcasted_iota(jnp.int32, (GRID_W, LANE), 1)
    kc = lane % GRID_W
    w0 = jnp.clip(qc - WIN_C // 2, 0, GRID_W - WIN_C)
    ok = jnp.logical_and(kc >= w0, kc < w0 + WIN_C)
    sub = lax.broadcasted_iota(jnp.int32, (SUBLANE, LANE), 0)
    tiles = []
    for dr in range(2 * WIN_R - 1):
        t = jnp.broadcast_to(rpb_ref[dr:dr + 1, :], (SUBLANE, LANE))
        t = pltpu.roll(t, LANE - (WIN_C - 1), axis=1)
        for bit in range(SUBLANE.bit_length() - 1):
            t = jnp.where((sub >> bit) & 1 == 1, pltpu.roll(t, 1 << bit, axis=1), t)
        tiles.append(jnp.concatenate([t] + [pltpu.roll(t, SUBLANE * a, axis=1)
                                            for a in range(1, GRID_W // SUBLANE)], axis=0))
    for p in range(2 * WIN_R - 2):
        pair = jnp.where(lane < GRID_W, tiles[p], pltpu.roll(tiles[p + 1], GRID_W, axis=1))
        o_ref[p] = jnp.where(ok, pair, NEG_INF)


def _ctx_attn_kernel(q_ref, k_ref, v_ref, o_ref):
    outs = []
    for h in range(N_HEADS):
        hd = slice(h * HEAD_DIM, (h + 1) * HEAD_DIM)
        outs.append(_softmax_pv([_qk(q_ref[:, hd], k_ref[:, hd])], [v_ref[:, hd]]))
    o_ref[...] = jnp.concatenate(outs, axis=-1).astype(BF16)


def _natten_bias(rpb):
    n_dr, n_dc = 2 * WIN_R - 1, 2 * WIN_C - 1
    assert rpb.shape == (N_HEADS, n_dr, n_dc) and GRID_W * 2 == LANE
    padded = jnp.pad(rpb.astype(F32), ((0, 0), (0, 0), (0, LANE - n_dc)))
    return pl.pallas_call(
        _bias_kernel,
        grid=(N_HEADS,),
        in_specs=[pl.BlockSpec((None, n_dr, LANE), lambda h: (h, 0, 0))],
        out_specs=pl.BlockSpec((None, n_dr - 1, GRID_W, LANE), lambda h: (h // 2, 0, h % 2, 0)),
        out_shape=jax.ShapeDtypeStruct((N_HEADS // 2, n_dr - 1, 2 * GRID_W, LANE), F32),
        compiler_params=_cparams(("arbitrary",)),
        name="natten_bias",
    )(padded)


def _natten_mixer(hs, mod_l, w_qkv, q_gain, k_gain, rpb, w_o, n_batch, seq, n_ctx):
    n_lat = n_batch * seq
    n_all = n_lat + n_batch * n_ctx
    rows = seq // GRID_W
    kr = min(WIN_R, rows)
    n_tiles = n_all // TM_FFN
    tpb = seq // TM_FFN
    scale = HEAD_DIM ** -0.5
    gain = jnp.concatenate([jnp.tile(q_gain, N_HEADS) * scale, jnp.tile(k_gain, N_HEADS),
                            jnp.ones((D,), F32)]).reshape(1, 3 * D)
    lane = np.arange(FF_TILE)
    e = jnp.asarray((lane[:, None] // HEAD_DIM == lane[None, :] // HEAD_DIM).astype(np.float32)).astype(BF16)
    qkv = pl.pallas_call(
        _qkv_kernel,
        grid=(n_tiles, 3),
        in_specs=[pl.BlockSpec((TM_FFN, D), lambda i, j: (i, 0)),
                  pl.BlockSpec((None, N_MOD, D), lambda i, j: (_mod_row(i, tpb, n_batch), 0, 0)),
                  pl.BlockSpec((D, D), lambda i, j: (0, j)),
                  pl.BlockSpec((1, D), lambda i, j: (0, j)),
                  pl.BlockSpec((FF_TILE, FF_TILE), lambda i, j: (0, 0))],
        out_specs=pl.BlockSpec((TM_FFN, D), lambda i, j: (i, j)),
        out_shape=jax.ShapeDtypeStruct((n_all, 3 * D), BF16),
        scratch_shapes=[pltpu.VMEM((TM_FFN, D), BF16)],
        compiler_params=_cparams(("arbitrary", "arbitrary")),
        name="qkv",
    )(hs, mod_l, w_qkv.astype(BF16), gain, e)

    assert kr == WIN_R, "bias tiles pair up the key rows of a full window"
    bias = _natten_bias(rpb)
    ctx0 = n_lat // n_ctx
    once = pl.Buffered(1)
    n_rb = rows // NATTEN_ROWS
    o_lat = pl.pallas_call(
        functools.partial(_natten_kernel, rows=rows, kr=kr),
        grid=(n_batch, n_rb),
        in_specs=[pl.BlockSpec((NATTEN_ROWS * GRID_W, D), lambda b, r: (b * n_rb + r, 0)),
                  pl.BlockSpec((seq, D), lambda b, r: (b, 1)),
                  pl.BlockSpec((seq, D), lambda b, r: (b, 2), pipeline_mode=once),
                  pl.BlockSpec((n_ctx, D), lambda b, r: (ctx0 + b, 1)),
                  pl.BlockSpec((n_ctx, D), lambda b, r: (ctx0 + b, 2)),
                  pl.BlockSpec(bias.shape, lambda b, r: (0, 0, 0, 0), pipeline_mode=once)],
        out_specs=pl.BlockSpec((NATTEN_ROWS * GRID_W, D), lambda b, r: (b * n_rb + r, 0)),
        out_shape=jax.ShapeDtypeStruct((n_lat, D), BF16),
        compiler_params=_cparams(("arbitrary", "arbitrary")),
        name="natten",
    )(qkv, qkv, qkv, qkv, qkv, bias)
    o_ctx = pl.pallas_call(
        _ctx_attn_kernel,
        grid=(n_batch,),
        in_specs=[pl.BlockSpec((n_ctx, D), lambda b: (ctx0 + b, 0)),
                  pl.BlockSpec((n_ctx, D), lambda b: (ctx0 + b, 1)),
                  pl.BlockSpec((n_ctx, D), lambda b: (ctx0 + b, 2))],
        out_specs=pl.BlockSpec((n_ctx, D), lambda b: (b, 0)),
        out_shape=jax.ShapeDtypeStruct((n_batch * n_ctx, D), BF16),
        compiler_params=_cparams(("arbitrary",)),
        name="ctx_attn",
    )(qkv, qkv, qkv)
    w = w_o.astype(BF16)
    zero_b = jnp.zeros((D,), F32)
    row_fn = lambda t: _mod_row(t, seq // TM_LIN, n_batch)
    hs = _lin_res_call(hs, (o_lat,), (), w, zero_b, mod_l, 0, n_lat // TM_LIN, row_fn)
    return _lin_res_call(hs, (o_ctx,), (), w, zero_b, mod_l, n_lat // TM_LIN, n_batch * n_ctx // TM_LIN, row_fn)


def kernel(x, c, ctx, c_ctx, w_mod, b_mod, w_ff_in, w_ff_out, fnet_w_out, fnet_b_out, conv_w_in, conv_b_in, conv_w_dw, conv_b_dw, conv_ln_g, conv_ln_b, conv_w_out, conv_b_out, na_w_qkv, na_q_gain, na_k_gain, na_rpb, na_w_o):
    n_batch, seq, d = x.shape
    n_ctx = ctx.shape[1]
    depth = w_mod.shape[0]
    assert d == D and n_batch < MOD_ROWS
    assert seq % TM_FFN == 0 and (n_batch * n_ctx) % TM_FFN == 0
    n_lat = n_batch * seq

    cond = jnp.concatenate([c, c_ctx[None, :], jnp.zeros((MOD_ROWS - n_batch - 1, D), F32)], axis=0)
    mod = _mod_call(cond, w_mod, b_mod).reshape(depth, MOD_ROWS, N_MOD, D)
    n_tok = n_lat + n_batch * n_ctx
    n_stream = n_tok + (-n_tok % seq)
    assert depth > 1, "the first half-step is written for a layer that updates the context tokens"
    lat_tiles = n_lat // TM_FFN
    ctx_tiles = n_batch * n_ctx // TM_FFN
    wide_tiles = n_lat // TM_FFN_WIDE
    assert n_lat % TM_FFN_WIDE == 0 and (n_stream - n_lat) % TM_FFN_WIDE == 0
    w_gu, w_o = w_ff_in, w_ff_out

    def half_step(hs, layer, which, k0, with_ctx, in_place):
        args = (mod[layer], w_gu, w_o, layer, which, k0)
        hs = _ffn_call(hs, *args, wide_tiles, n_batch, seq, tm=TM_FFN_WIDE, in_place=in_place)
        if with_ctx:
            hs = _ffn_call(hs, *args, ctx_tiles, n_batch, seq, tm=TM_FFN, tile0=lat_tiles)
        return hs

    for i in range(depth):
        kind, slot = i % 3, i // 3
        ctx_post = i < depth - 1
        ctx_pre = ctx_post or kind == 2
        mod_l = mod[i]
        if i == 0:
            args = (mod_l, w_gu, w_o, 0, 0, 0)
            hs = _ffn_call(x.reshape(n_lat, D), *args, wide_tiles, n_batch, seq, tm=TM_FFN_WIDE, in_place=False,
                           out_rows=n_stream, zero_tiles=(n_stream - n_lat) // TM_FFN_WIDE)
            hs = _ffn_call(ctx.reshape(n_batch * n_ctx, D), *args, ctx_tiles, n_batch, seq, tm=TM_FFN,
                           tile0=lat_tiles, in_place=False, stream=hs)
        else:
            hs = half_step(hs, i, 0, 0, ctx_pre, True)
        if kind == 0:
            hs = _fourier_mixer(hs, mod_l, fnet_w_out[slot], fnet_b_out[slot], n_batch, seq, n_ctx, ctx_post)
        elif kind == 1:
            assert ctx_post
            hs = _conv_mixer(hs, mod_l, conv_w_in[slot], conv_b_in[slot], conv_w_dw[slot], conv_b_dw[slot],
                             conv_ln_g[slot], conv_ln_b[slot], conv_w_out[slot], conv_b_out[slot],
                             n_batch, seq, n_ctx)
        else:
            assert ctx_post
            hs = _natten_mixer(hs, mod_l, na_w_qkv[slot], na_q_gain[slot], na_k_gain[slot], na_rpb[slot],
                               na_w_o[slot], n_batch, seq, n_ctx)
        final = i == depth - 1
        assert ctx_post != final
        hs = half_step(hs, i, 1, 6, ctx_post, not final)
    return hs[:n_lat].reshape(n_batch, seq, D)
```

```python
import functools

import numpy as np
import jax
import jax.numpy as jnp
from jax import lax
from jax.experimental import pallas as pl
from jax.experimental.pallas import tpu as pltpu

F32 = jnp.float32
BF16 = jnp.bfloat16

D = 1024
D_FF = 2816
N_MOD = 9
EPS = 1e-6
GRID_W = 64
LANE = 128
SUBLANE = 8
N_GROUPS = 4
GROUP_C = D // N_GROUPS
CONV_K = 31
CONV_HALO = 16
N_HEADS = 16
HEAD_DIM = D // N_HEADS
WIN_R = 8
WIN_C = 16
NEG_INF = -1e30
MOD_ROWS = 8
MOD_COLS = 3 * D

FF_TILE = 256
FF_PER_STEP = 2
TM_FFN = 1024
TM_PROJ = 2048
PROJ_ROWS = 256
GLU_COLS = 1024
FFN_ROWS = 256
TM_LIN = 1024
TM_CONV = 256
CONV_ROWS = 32
COL_TILE = 16
NATTEN_ROWS = 4
VMEM_LIMIT = 52 * 1024 * 1024
VMEM_LIMIT_WIDE = 58 * 1024 * 1024
TM_FFN_WIDE = 2048


def _cparams(sem):
    return pltpu.CompilerParams(dimension_semantics=sem, vmem_limit_bytes=VMEM_LIMIT)


def _silu(x):
    return x / (1.0 + jnp.exp(-x))


def _norm_mod(x, shift, scale):
    ms = jnp.mean(x * x, axis=-1, keepdims=True)
    return x * lax.rsqrt(ms + EPS) * (1.0 + scale) + shift


def _dot(a, b):
    return jnp.dot(a, b, preferred_element_type=F32)


def _mod_kernel(c_ref, w_ref, b_ref, o_ref):
    s = _silu(c_ref[...]).astype(BF16)
    o_ref[...] = _dot(s, w_ref[...].astype(BF16)) + b_ref[...]


def _mod_call(cond, w_mod, b_mod):
    depth = w_mod.shape[0]
    return pl.pallas_call(
        _mod_kernel,
        grid=(depth, N_MOD * D // MOD_COLS),
        in_specs=[
            pl.BlockSpec((MOD_ROWS, D), lambda l, k: (0, 0)),
            pl.BlockSpec((None, D, MOD_COLS), lambda l, k: (l, 0, k)),
            pl.BlockSpec((None, 1, MOD_COLS), lambda l, k: (l, 0, k)),
        ],
        out_specs=pl.BlockSpec((None, MOD_ROWS, MOD_COLS), lambda l, k: (l, 0, k)),
        out_shape=jax.ShapeDtypeStruct((depth, MOD_ROWS, N_MOD * D), F32),
        compiler_params=_cparams(("arbitrary", "arbitrary")),
        name="mod",
    )(cond, w_mod, b_mod.reshape(depth, 1, N_MOD * D))


def _mod_row(tile, tiles_per_batch, n_batch):
    return jnp.minimum(tile // tiles_per_batch, n_batch)


def _ffn_kernel(*refs, k0, n_ff, n_real, in_stream):
    h_ref, mod_ref = refs[0], refs[2 if in_stream else 1]
    w_refs = refs[3 if in_stream else 2:-2]
    o_ref, xn_ref = refs[-2:]
    wg_refs, wu_refs, wo_refs = (w_refs[t * FF_PER_STEP:(t + 1) * FF_PER_STEP] for t in range(3))
    tokens = lambda rows: h_ref[rows, :]
    j = pl.program_id(1)
    n_chunks = o_ref.shape[0] // FFN_ROWS
    n_steps = pl.cdiv(n_ff, FF_PER_STEP)
    assert n_steps >= 2

    def step(first, last):
        n_sub = n_ff - FF_PER_STEP * (n_steps - 1) if last else FF_PER_STEP
        wg, wu, wo = ([r[...].astype(BF16) for r in w[:n_sub]] for w in (wg_refs, wu_refs, wo_refs))
        gu = {}
        for c in range(n_chunks + 1):
            if c < n_chunks:
                rows = slice(c * FFN_ROWS, (c + 1) * FFN_ROWS)
                if first:
                    xn = _norm_mod(tokens(rows), mod_ref[k0:k0 + 1, :], mod_ref[k0 + 1:k0 + 2, :]).astype(BF16)
                    xn_ref[rows, :] = xn
                else:
                    xn = xn_ref[rows, :]
                gu[c] = [(_dot(xn, wg[t]), _dot(xn, wu[t])) for t in range(n_sub)]
            if c >= 1:
                rows = slice((c - 1) * FFN_ROWS, c * FFN_ROWS)
                y = None
                for t, (g, u) in enumerate(gu.pop(c - 1)):
                    yt = _dot((_silu(g) * u).astype(BF16), wo[t])
                    y = yt if y is None else y + yt
                if first:
                    o_ref[rows, :] = y
                elif last:
                    o_ref[rows, :] = tokens(rows) + (0.5 * mod_ref[k0 + 2:k0 + 3, :]) * (o_ref[rows, :] + y)
                else:
                    o_ref[rows, :] += y

    real = pl.program_id(0) < n_real
    pl.when(jnp.logical_and(real, j == 0))(functools.partial(step, True, False))
    pl.when(jnp.logical_and(real, jnp.logical_and(j > 0, j < n_steps - 1)))(functools.partial(step, False, False))
    pl.when(jnp.logical_and(real, j == n_steps - 1))(functools.partial(step, False, True))

    @pl.when(jnp.logical_and(jnp.logical_not(real), j == n_steps - 1))
    def _():
        o_ref[...] = jnp.zeros_like(o_ref)


def _ffn_call(tok, mod_l, w_gu, w_o, layer, which, k0, n_tiles, n_batch, seq, *, tm, tile0=0, in_place=True,
              stream=None, out_rows=None, zero_tiles=0):
    n_ff = D_FF // FF_TILE
    tpb = seq // tm
    assert not (in_place and stream is not None) and (in_place or stream is not None or tile0 == 0)
    assert zero_tiles == 0 or not in_place
    tok_tile0 = tile0 if in_place else 0
    n_steps = pl.cdiv(n_ff, FF_PER_STEP)

    def ff(t):
        return lambda i, j: jnp.where(i < n_tiles, jnp.minimum(FF_PER_STEP * j + t, n_ff - 1), n_ff - 1)

    subs = range(FF_PER_STEP)
    last_tok = tok.shape[0] // tm - 1
    operands = [tok] + ([stream] if stream is not None else [])
    specs = [pl.BlockSpec((tm, D), lambda i, j: (jnp.minimum(i + tok_tile0, last_tok), 0))]
    if stream is not None:
        specs.append(pl.BlockSpec(memory_space=pl.ANY))
        out_rows = stream.shape[0]
    elif in_place:
        out_rows = tok.shape[0]
    elif out_rows is None:
        out_rows = (n_tiles + zero_tiles) * tm
    mod_row = lambda i: _mod_row(jnp.minimum(i, n_tiles - 1) + tile0, tpb, n_batch)
    return pl.pallas_call(
        functools.partial(_ffn_kernel, k0=k0, n_ff=n_ff, n_real=n_tiles, in_stream=stream is not None),
        grid=(n_tiles + zero_tiles, n_steps),
        in_specs=specs + [pl.BlockSpec((None, N_MOD, D), lambda i, j: (mod_row(i), 0, 0))]
        + [pl.BlockSpec((None, None, D, FF_TILE), lambda i, j, f=ff(t): (layer, which, 0, f(i, j))) for t in subs]
        + [pl.BlockSpec((None, None, D, FF_TILE), lambda i, j, f=ff(t): (layer, which, 0, n_ff + f(i, j)))
           for t in subs]
        + [pl.BlockSpec((None, None, FF_TILE, D), lambda i, j, f=ff(t): (layer, which, f(i, j), 0)) for t in subs],
        out_specs=pl.BlockSpec((tm, D), lambda i, j: (i + tile0, 0)),
        out_shape=jax.ShapeDtypeStruct((out_rows, D), F32),
        scratch_shapes=[pltpu.VMEM((tm, D), BF16)],
        input_output_aliases={0: 0} if in_place else ({1: 0} if stream is not None else {}),
        compiler_params=pltpu.CompilerParams(dimension_semantics=("arbitrary", "arbitrary"),
                                             vmem_limit_bytes=VMEM_LIMIT_WIDE if tm > TM_FFN else VMEM_LIMIT),
        name="ffn",
    )(*operands, mod_l, *([w_gu] * (2 * FF_PER_STEP)), *([w_o] * FF_PER_STEP))


def _lin_res_kernel(*refs, fourier):
    if fourier:
        xr_ref, xi_ref, cc_ref, sc_ref, w_ref, b_ref, h_ref, mod_ref, o_ref = refs
        parts = []
        for g in range(N_GROUPS):
            sl = slice(g * GROUP_C, (g + 1) * GROUP_C)
            parts.append(_dot(xr_ref[:, sl], cc_ref[...]) + _dot(xi_ref[:, sl], sc_ref[...]))
        x = jnp.concatenate(parts, axis=-1).astype(BF16)
    else:
        x_ref, w_ref, b_ref, h_ref, mod_ref, o_ref = refs
        x = x_ref[...]
    y = _dot(x, w_ref[...]) + b_ref[...]
    o_ref[...] = h_ref[...] + mod_ref[5:6, :] * y


def _lin_res_call(hs, xs, tables, w, b, mod_l, tile0, n_tiles, mod_row_fn):
    fourier = len(xs) == 2
    x_spec = pl.BlockSpec((TM_LIN, D), lambda i: (i, 0))
    t_spec = pl.BlockSpec((GROUP_C, GROUP_C), lambda i: (0, 0))
    in_specs = [x_spec] * len(xs) + [t_spec] * len(tables) + [
        pl.BlockSpec((D, D), lambda i: (0, 0)),
        pl.BlockSpec((1, D), lambda i: (0, 0)),
        pl.BlockSpec((TM_LIN, D), lambda i: (i + tile0, 0)),
        pl.BlockSpec((None, N_MOD, D), lambda i: (mod_row_fn(i + tile0), 0, 0)),
    ]
    h_index = len(xs) + len(tables) + 2
    return pl.pallas_call(
        functools.partial(_lin_res_kernel, fourier=fourier),
        grid=(n_tiles,),
        in_specs=in_specs,
        out_specs=pl.BlockSpec((TM_LIN, D), lambda i: (i + tile0, 0)),
        out_shape=jax.ShapeDtypeStruct(hs.shape, F32),
        input_output_aliases={h_index: 0},
        compiler_params=_cparams(("arbitrary",)),
        name="lin_res_fourier" if fourier else "lin_res",
    )(*xs, *tables, w, b.reshape(1, D), hs, mod_l)


def _fnet_rows_kernel(h_ref, mod_ref, kf_ref, zr_ref, zi_ref):
    rows = h_ref.shape[0]
    half = rows // 2
    xn = _norm_mod(h_ref[...], mod_ref[3:4, :], mod_ref[4:5, :])
    n = rows * SUBLANE
    n_cos = (half + 1) * SUBLANE
    zr, zi = [], []
    for c0 in range(0, COL_TILE, SUBLANE):
        a = xn[:, c0:c0 + SUBLANE, :].reshape(n, D).astype(BF16)
        z = _dot(kf_ref[...], a)
        zc = z[:n_cos].reshape(half + 1, SUBLANE, D)
        zs = z[n_cos:].reshape(half - 1, SUBLANE, D)
        zero = jnp.zeros((1, SUBLANE, D), F32)
        zr.append(jnp.concatenate([zc] + [zc[rows - k:rows - k + 1] for k in range(half + 1, rows)], axis=0))
        zi.append(jnp.concatenate([zero, zs, zero] + [-zs[rows - k - 1:rows - k] for k in range(half + 1, rows)],
                                  axis=0))
    zr_ref[...] = jnp.concatenate(zr, axis=1).astype(BF16)
    zi_ref[...] = jnp.concatenate(zi, axis=1).astype(BF16)


def _fnet_cols_kernel(zr_ref, zi_ref, m2_ref, cc_ref, sc_ref, w_ref, b_ref, h_ref, mod_ref, o_ref,
                      xr_ref, xi_ref, y_ref):
    n = zr_ref.shape[1]
    for j in range(SUBLANE):
        z = jnp.concatenate([zr_ref[j], zi_ref[j]], axis=0)
        x = _dot(m2_ref[j], z)
        xr_ref[j * n:(j + 1) * n, :] = x[:n].astype(BF16)
        xi_ref[j * n:(j + 1) * n, :] = x[n:].astype(BF16)
    parts = []
    for g in range(N_GROUPS):
        sl = slice(g * GROUP_C, (g + 1) * GROUP_C)
        parts.append(_dot(xr_ref[:, sl], cc_ref[...]) + _dot(xi_ref[:, sl], sc_ref[...]))
    y = _dot(jnp.concatenate(parts, axis=-1).astype(BF16), w_ref[...]) + b_ref[...]
    for j in range(SUBLANE):
        rows = pl.ds(j, n, stride=SUBLANE)
        for lt in range(D // LANE):
            y_ref[lt, rows, :] = y[j * n:(j + 1) * n, lt * LANE:(lt + 1) * LANE]
    y_t = jnp.concatenate([y_ref[lt] for lt in range(D // LANE)], axis=-1).reshape(n, SUBLANE, D)
    o_ref[...] = h_ref[...] + mod_ref[5:6, :] * y_t


def _fnet_ctx_kernel(h_ref, mod_ref, f_ref, xr_ref, xi_ref):
    n = h_ref.shape[0]
    xn = _norm_mod(h_ref[...], mod_ref[3:4, :], mod_ref[4:5, :]).astype(BF16)
    x = _dot(f_ref[...], xn)
    xr_ref[...] = x[:n].astype(BF16)
    xi_ref[...] = x[n:].astype(BF16)


def _dft_tables(rows):
    n_seq = rows * GRID_W
    a = np.arange(rows)
    th1 = 2.0 * np.pi * np.outer(a, a) / rows
    half = rows // 2
    f1 = np.concatenate([np.cos(th1)[:half + 1], -np.sin(th1)[1:half]], axis=0)
    f1 = np.kron(f1, np.eye(SUBLANE))
    k2 = np.arange(GRID_W)[:, None, None]
    k1 = np.arange(rows)[None, :, None]
    n1 = np.arange(GRID_W)[None, None, :]
    th2 = 2.0 * np.pi * (n1 * (GRID_W * k1 + k2) % n_seq) / n_seq
    c2, s2 = np.cos(th2), np.sin(th2)
    m2 = np.concatenate([np.concatenate([c2, s2], axis=2), np.concatenate([-s2, c2], axis=2)], axis=1)
    return f1.astype(np.float32), m2.astype(np.float32)


def _channel_tables(n_seq):
    c = np.arange(GROUP_C)
    th = 2.0 * np.pi * np.outer(c, c) / GROUP_C
    s = 1.0 / np.sqrt(float(n_seq) * GROUP_C)
    return (np.cos(th) * s).astype(np.float32), (np.sin(th) * s).astype(np.float32)


def _fourier_mixer(hs, mod_l, w_out, b_out, n_batch, seq, n_ctx, with_ctx):
    rows = seq // GRID_W
    assert rows == GRID_W, "the two-stage DFT is written for a square token grid"
    n_lat = n_batch * seq
    f1, m2 = _dft_tables(rows)
    cc, sc = _channel_tables(seq)
    f1, m2, cc, sc = (jnp.asarray(t).astype(BF16) for t in (f1, m2, cc, sc))
    w = w_out.astype(BF16)
    n_ct = GRID_W // COL_TILE
    grid3 = hs.reshape(hs.shape[0] // GRID_W, GRID_W, D)
    z_shape = jax.ShapeDtypeStruct((n_batch * rows, GRID_W, D), BF16)
    blk_cols = pl.BlockSpec((rows, COL_TILE, D), lambda b, c: (b, c, 0))
    mod_b = pl.BlockSpec((None, N_MOD, D), lambda b, c: (b, 0, 0))
    zr, zi = pl.pallas_call(
        _fnet_rows_kernel,
        grid=(n_batch, n_ct),
        in_specs=[blk_cols, mod_b, pl.BlockSpec(f1.shape, lambda b, c: (0, 0))],
        out_specs=[blk_cols, blk_cols],
        out_shape=[z_shape, z_shape],
        compiler_params=_cparams(("arbitrary", "arbitrary")),
        name="fnet_rows",
    )(grid3, mod_l, f1)
    n_kt = GRID_W // SUBLANE
    blk_k2 = pl.BlockSpec((SUBLANE, GRID_W, D), lambda b, k: (b * n_kt + k, 0, 0))
    blk_tok = pl.BlockSpec((rows, SUBLANE, D), lambda b, k: (b, k, 0))
    table = pl.BlockSpec((GROUP_C, GROUP_C), lambda b, k: (0, 0))
    n_tok = rows * SUBLANE
    grid3 = pl.pallas_call(
        _fnet_cols_kernel,
        grid=(n_batch, n_kt),
        in_specs=[blk_k2, blk_k2,
                  pl.BlockSpec((SUBLANE, 2 * rows, 2 * GRID_W), lambda b, k: (k, 0, 0)),
                  table, table,
                  pl.BlockSpec((D, D), lambda b, k: (0, 0)),
                  pl.BlockSpec((1, D), lambda b, k: (0, 0)),
                  blk_tok, mod_b],
        out_specs=blk_tok,
        out_shape=jax.ShapeDtypeStruct(grid3.shape, F32),
        scratch_shapes=[pltpu.VMEM((n_tok, D), BF16), pltpu.VMEM((n_tok, D), BF16),
                        pltpu.VMEM((D // LANE, n_tok, LANE), F32)],
        input_output_aliases={7: 0},
        compiler_params=_cparams(("arbitrary", "arbitrary")),
        name="fnet_cols",
    )(zr, zi, m2, cc, sc, w, b_out.reshape(1, D), grid3, mod_l)
    hs = grid3.reshape(hs.shape)
    tpb = seq // TM_LIN
    row_fn = lambda t: _mod_row(t, tpb, n_batch)
    if with_ctx:
        a = np.arange(n_ctx)
        th = 2.0 * np.pi * np.outer(a, a) / n_ctx
        s = np.sqrt(float(seq) / n_ctx)
        fc = jnp.asarray(np.concatenate([np.cos(th) * s, -np.sin(th) * s], axis=0).astype(np.float32)).astype(BF16)
        c_shape = jax.ShapeDtypeStruct((n_batch * n_ctx, D), BF16)
        tile0 = n_lat // n_ctx
        xr, xi = pl.pallas_call(
            _fnet_ctx_kernel,
            grid=(n_batch,),
            in_specs=[pl.BlockSpec((n_ctx, D), lambda b: (tile0 + b, 0)),
                      pl.BlockSpec((None, N_MOD, D), lambda b: (n_batch, 0, 0)),
                      pl.BlockSpec((2 * n_ctx, n_ctx), lambda b: (0, 0))],
            out_specs=[pl.BlockSpec((n_ctx, D), lambda b: (b, 0))] * 2,
            out_shape=[c_shape, c_shape],
            compiler_params=_cparams(("arbitrary",)),
            name="fnet_ctx",
        )(hs, mod_l, fc)
        hs = _lin_res_call(hs, (xr, xi), (cc, sc), w, b_out, mod_l,
                           n_lat // TM_LIN, n_batch * n_ctx // TM_LIN, row_fn)
    return hs


def _mixer_input_chunks(h_ref, mod_ref, xn_ref, first):
    for r0 in range(0, h_ref.shape[0], PROJ_ROWS):
        rows = slice(r0, r0 + PROJ_ROWS)
        if first:
            xn = _norm_mod(h_ref[rows, :], mod_ref[3:4, :], mod_ref[4:5, :]).astype(BF16)
            xn_ref[rows, :] = xn
        else:
            xn = xn_ref[rows, :]
        yield rows, xn


def _glu_kernel(h_ref, mod_ref, wa_ref, wg_ref, ba_ref, bg_ref, o_ref, xn_ref):
    def step(first):
        for rows, xn in _mixer_input_chunks(h_ref, mod_ref, xn_ref, first):
            a = _dot(xn, wa_ref[...]) + ba_ref[...]
            g = _dot(xn, wg_ref[...]) + bg_ref[...]
            o_ref[rows, :] = a / (1.0 + jnp.exp(-g))

    j = pl.program_id(1)
    pl.when(j == 0)(functools.partial(step, True))
    pl.when(j > 0)(functools.partial(step, False))


def _conv_kernel(u_ref, up_ref, un_ref, wdw_ref, bdw_ref, lng_ref, lnb_ref, w_ref, b_ref, h_ref, mod_ref,
                 o_ref, buf_ref, conv_ref, v_ref, *, lat_tiles, tiles_per_seq):
    i = pl.program_id(0)
    is_ctx = i >= lat_tiles
    first = jnp.logical_or(is_ctx, i % tiles_per_seq == 0)
    last = jnp.logical_or(is_ctx, i % tiles_per_seq == tiles_per_seq - 1)
    tm = u_ref.shape[0]
    n_buf = tm + 2 * CONV_HALO
    buf_ref[0, 0:CONV_HALO, :] = jnp.where(first, 0.0, up_ref[...])
    buf_ref[0, CONV_HALO:CONV_HALO + tm, :] = u_ref[...]
    buf_ref[0, CONV_HALO + tm:, :] = jnp.where(last, 0.0, un_ref[...])
    for s in range(1, SUBLANE):
        buf_ref[s] = pltpu.roll(buf_ref[0], n_buf - s, axis=0)
    off = CONV_HALO - CONV_K // 2
    for lt in range(D // LANE):
        lanes = slice(lt * LANE, (lt + 1) * LANE)
        for c0 in range(0, tm, CONV_ROWS):
            acc = jnp.zeros((CONV_ROWS, LANE), F32) + bdw_ref[:, lanes]
            for k in range(CONV_K):
                row = c0 + (off + k) // SUBLANE * SUBLANE
                acc = acc + wdw_ref[k:k + 1, lanes] * buf_ref[(off + k) % SUBLANE, row:row + CONV_ROWS, lanes]
            conv_ref[c0:c0 + CONV_ROWS, lanes] = acc
    chunk = 32
    for c0 in range(0, tm, chunk):
        acc = conv_ref[c0:c0 + chunk, :]
        mu = jnp.mean(acc, axis=-1, keepdims=True)
        d = acc - mu
        var = jnp.mean(d * d, axis=-1, keepdims=True)
        v = d * lax.rsqrt(var + EPS) * lng_ref[...] + lnb_ref[...]
        v_ref[c0:c0 + chunk, :] = _silu(v).astype(BF16)
    y = _dot(v_ref[...], w_ref[...]) + b_ref[...]
    o_ref[...] = h_ref[...] + mod_ref[5:6, :] * y


def _conv_mixer(hs, mod_l, w_in, b_in, w_dw, b_dw, ln_g, ln_b, w_out, b_out, n_batch, seq, n_ctx):
    n_lat = n_batch * seq
    n_all = n_lat + n_batch * n_ctx
    assert n_ctx == TM_CONV, "context sequences are one conv tile long"
    n_tiles = pl.cdiv(n_all, TM_PROJ)
    n_col = D // GLU_COLS
    tpb = seq // TM_PROJ
    col = pl.BlockSpec((D, GLU_COLS), lambda i, j: (0, j))
    gcol = pl.BlockSpec((D, GLU_COLS), lambda i, j: (0, n_col + j))
    u = pl.pallas_call(
        _glu_kernel,
        grid=(n_tiles, n_col),
        in_specs=[pl.BlockSpec((TM_PROJ, D), lambda i, j: (i, 0)),
                  pl.BlockSpec((None, N_MOD, D), lambda i, j: (_mod_row(i, tpb, n_batch), 0, 0)),
                  col, gcol,
                  pl.BlockSpec((1, GLU_COLS), lambda i, j: (0, j)),
                  pl.BlockSpec((1, GLU_COLS), lambda i, j: (0, n_col + j))],
        out_specs=pl.BlockSpec((TM_PROJ, GLU_COLS), lambda i, j: (i, j)),
        out_shape=jax.ShapeDtypeStruct((n_tiles * TM_PROJ, D), F32),
        scratch_shapes=[pltpu.VMEM((TM_PROJ, D), BF16)],
        compiler_params=_cparams(("arbitrary", "arbitrary")),
        name="conv_glu",
    )(hs, mod_l, w_in.astype(BF16), w_in.astype(BF16), b_in.reshape(1, 2 * D), b_in.reshape(1, 2 * D))

    n_ct = n_all // TM_CONV
    hpt = TM_CONV // CONV_HALO
    n_halo = n_all // CONV_HALO
    tps = seq // TM_CONV
    vec = pl.BlockSpec((1, D), lambda i: (0, 0))
    tile = pl.BlockSpec((TM_CONV, D), lambda i: (i, 0))
    return pl.pallas_call(
        functools.partial(_conv_kernel, lat_tiles=n_lat // TM_CONV, tiles_per_seq=tps),
        grid=(n_ct,),
        in_specs=[tile,
                  pl.BlockSpec((CONV_HALO, D), lambda i: (jnp.maximum(i * hpt - 1, 0), 0)),
                  pl.BlockSpec((CONV_HALO, D), lambda i: (jnp.minimum((i + 1) * hpt, n_halo - 1), 0)),
                  pl.BlockSpec((CONV_K, D), lambda i: (0, 0)),
                  vec, vec, vec,
                  pl.BlockSpec((D, D), lambda i: (0, 0)),
                  vec, tile,
                  pl.BlockSpec((None, N_MOD, D), lambda i: (_mod_row(i, tps, n_batch), 0, 0))],
        out_specs=tile,
        out_shape=jax.ShapeDtypeStruct(hs.shape, F32),
        scratch_shapes=[pltpu.VMEM((SUBLANE, TM_CONV + 2 * CONV_HALO, D), F32), pltpu.VMEM((TM_CONV, D), F32),
                        pltpu.VMEM((TM_CONV, D), BF16)],
        input_output_aliases={9: 0},
        compiler_params=_cparams(("arbitrary",)),
        name="conv_dw",
    )(u, u, u, w_dw, b_dw.reshape(1, D), ln_g.reshape(1, D), ln_b.reshape(1, D), w_out.astype(BF16),
      b_out.reshape(1, D), hs, mod_l)


def _qkv_kernel(h_ref, mod_ref, w_ref, gain_ref, e_ref, o_ref, xn_ref):
    n_e = e_ref.shape[0]

    def step(first, head_norm):
        for rows, xn in _mixer_input_chunks(h_ref, mod_ref, xn_ref, first):
            y = _dot(xn, w_ref[...])
            if not head_norm:
                o_ref[rows, :] = y.astype(BF16)
                continue
            for c0 in range(0, D, n_e):
                cols = slice(c0, c0 + n_e)
                yc = y[:, cols]
                ss = _dot((yc * yc).astype(BF16), e_ref[...])
                rs = lax.rsqrt(ss * (1.0 / HEAD_DIM) + EPS) * gain_ref[:, cols]
                o_ref[rows, cols] = (yc * rs).astype(BF16)

    j = pl.program_id(1)
    pl.when(j == 0)(functools.partial(step, True, True))
    pl.when(j == 1)(functools.partial(step, False, True))
    pl.when(j == 2)(functools.partial(step, False, False))


def _softmax_pv(s_list, v_list):
    m = s_list[0].max(axis=-1, keepdims=True)
    for s in s_list[1:]:
        m = jnp.maximum(m, s.max(axis=-1, keepdims=True))
    den = 0.0
    acc = 0.0
    for s, v in zip(s_list, v_list):
        p = jnp.exp(s - m)
        den = den + p.sum(axis=-1, keepdims=True)
        acc = acc + _dot(p.astype(BF16), v)
    return acc / den


def _qk(q, k):
    return lax.dot_general(q, k, (((1,), (1,)), ((), ())), preferred_element_type=F32)


def _natten_kernel(q_ref, k_ref, v_ref, kc_ref, vc_ref, bias_ref, o_ref, *, rows, kr):
    r0 = pl.program_id(1) * NATTEN_ROWS
    left = lax.broadcasted_iota(jnp.int32, (GRID_W, LANE), 1) < HEAD_DIM
    n2 = 2 * GRID_W
    wins, dr0s = [], []
    for i in range(NATTEN_ROWS):
        rs = jnp.clip(r0 + i - kr // 2, 0, rows - kr)
        wins.append(pl.ds(pl.multiple_of(rs * GRID_W, GRID_W), kr * GRID_W))
        dr0s.append(rs - (r0 + i) + WIN_R - 1)
    for p in range(N_HEADS // 2):
        cols = slice(p * LANE, (p + 1) * LANE)
        q2 = []
        for i in range(NATTEN_ROWS):
            q = q_ref[i * GRID_W:(i + 1) * GRID_W, cols]
            zero = jnp.zeros_like(q)
            q2 += [jnp.where(left, q, zero), jnp.where(left, zero, q)]
        q2 = jnp.concatenate(q2, axis=0)
        s_ctx = _qk(q2, kc_ref[:, cols])
        m_ctx = s_ctx.max(axis=-1, keepdims=True)
        m, den, acc = [], [], []
        for i in range(NATTEN_ROWS):
            mine = slice(i * n2, (i + 1) * n2)
            bias = jnp.concatenate([bias_ref[p, dr0s[i] + 2 * t] for t in range(kr // 2)], axis=-1)
            s_lat = _qk(q2[mine], k_ref[wins[i], cols]) + bias
            m.append(jnp.maximum(s_lat.max(axis=-1, keepdims=True), m_ctx[mine]))
            p_lat = jnp.exp(s_lat - m[i])
            den.append(p_lat.sum(axis=-1, keepdims=True))
            acc.append(_dot(p_lat.astype(BF16), v_ref[wins[i], cols]))
        p_ctx = jnp.exp(s_ctx - jnp.concatenate(m, axis=0))
        den_ctx = p_ctx.sum(axis=-1, keepdims=True)
        acc_ctx = _dot(p_ctx.astype(BF16), vc_ref[:, cols])
        for i in range(NATTEN_ROWS):
            mine = slice(i * n2, (i + 1) * n2)
            o = (acc[i] + acc_ctx[mine]) / (den[i] + den_ctx[mine])
            o_ref[i * GRID_W:(i + 1) * GRID_W, cols] = jnp.where(left, o[:GRID_W], o[GRID_W:]).astype(BF16)


def _bias_kernel(rpb_ref, o_ref):
    qc = lax.broadcasted_iota(jnp.int32, (GRID_W, LANE), 0)
    lane = lax.broadcasted_iota(jnp.int32, (GRID_W, LANE), 1)
    kc = lane % GRID_W
    w0 = jnp.clip(qc - WIN_C // 2, 0, GRID_W - WIN_C)
    ok = jnp.logical_and(kc >= w0, kc < w0 + WIN_C)
    sub = lax.broadcasted_iota(jnp.int32, (SUBLANE, LANE), 0)
    tiles = []
    for dr in range(2 * WIN_R - 1):
        t = jnp.broadcast_to(rpb_ref[dr:dr + 1, :], (SUBLANE, LANE))
        t = pltpu.roll(t, LANE - (WIN_C - 1), axis=1)
        for bit in range(SUBLANE.bit_length() - 1):
            t = jnp.where((sub >> bit) & 1 == 1, pltpu.roll(t, 1 << bit, axis=1), t)
        tiles.append(jnp.concatenate([t] + [pltpu.roll(t, SUBLANE * a, axis=1)
                                            for a in range(1, GRID_W // SUBLANE)], axis=0))
    for p in range(2 * WIN_R - 2):
        pair = jnp.where(lane < GRID_W, tiles[p], pltpu.roll(tiles[p + 1], GRID_W, axis=1))
        o_ref[p] = jnp.where(ok, pair, NEG_INF)


def _ctx_attn_kernel(q_ref, k_ref, v_ref, o_ref):
    outs = []
    for h in range(N_HEADS):
        hd = slice(h * HEAD_DIM, (h + 1) * HEAD_DIM)
        outs.append(_softmax_pv([_qk(q_ref[:, hd], k_ref[:, hd])], [v_ref[:, hd]]))
    o_ref[...] = jnp.concatenate(outs, axis=-1).astype(BF16)


def _natten_bias(rpb):
    n_dr, n_dc = 2 * WIN_R - 1, 2 * WIN_C - 1
    assert rpb.shape == (N_HEADS, n_dr, n_dc) and GRID_W * 2 == LANE
    padded = jnp.pad(rpb.astype(F32), ((0, 0), (0, 0), (0, LANE - n_dc)))
    return pl.pallas_call(
        _bias_kernel,
        grid=(N_HEADS,),
        in_specs=[pl.BlockSpec((None, n_dr, LANE), lambda h: (h, 0, 0))],
        out_specs=pl.BlockSpec((None, n_dr - 1, GRID_W, LANE), lambda h: (h // 2, 0, h % 2, 0)),
        out_shape=jax.ShapeDtypeStruct((N_HEADS // 2, n_dr - 1, 2 * GRID_W, LANE), F32),
        compiler_params=_cparams(("arbitrary",)),
        name="natten_bias",
    )(padded)


def _natten_mixer(hs, mod_l, w_qkv, q_gain, k_gain, rpb, w_o, n_batch, seq, n_ctx):
    n_lat = n_batch * seq
    n_all = n_lat + n_batch * n_ctx
    rows = seq // GRID_W
    kr = min(WIN_R, rows)
    n_tiles = pl.cdiv(n_all, TM_PROJ)
    tpb = seq // TM_PROJ
    scale = HEAD_DIM ** -0.5
    gain = jnp.concatenate([jnp.tile(q_gain, N_HEADS) * scale, jnp.tile(k_gain, N_HEADS),
                            jnp.ones((D,), F32)]).reshape(1, 3 * D)
    lane = np.arange(FF_TILE)
    e = jnp.asarray((lane[:, None] // HEAD_DIM == lane[None, :] // HEAD_DIM).astype(np.float32)).astype(BF16)
    qkv = pl.pallas_call(
        _qkv_kernel,
        grid=(n_tiles, 3),
        in_specs=[pl.BlockSpec((TM_PROJ, D), lambda i, j: (i, 0)),
                  pl.BlockSpec((None, N_MOD, D), lambda i, j: (_mod_row(i, tpb, n_batch), 0, 0)),
                  pl.BlockSpec((D, D), lambda i, j: (0, j)),
                  pl.BlockSpec((1, D), lambda i, j: (0, j)),
                  pl.BlockSpec((FF_TILE, FF_TILE), lambda i, j: (0, 0))],
        out_specs=pl.BlockSpec((TM_PROJ, D), lambda i, j: (i, j)),
        out_shape=jax.ShapeDtypeStruct((n_tiles * TM_PROJ, 3 * D), BF16),
        scratch_shapes=[pltpu.VMEM((TM_PROJ, D), BF16)],
        compiler_params=_cparams(("arbitrary", "arbitrary")),
        name="qkv",
    )(hs, mod_l, w_qkv.astype(BF16), gain, e)

    assert kr == WIN_R, "bias tiles pair up the key rows of a full window"
    bias = _natten_bias(rpb)
    ctx0 = n_lat // n_ctx
    once = pl.Buffered(1)
    n_rb = rows // NATTEN_ROWS
    o_lat = pl.pallas_call(
        functools.partial(_natten_kernel, rows=rows, kr=kr),
        grid=(n_batch, n_rb),
        in_specs=[pl.BlockSpec((NATTEN_ROWS * GRID_W, D), lambda b, r: (b * n_rb + r, 0)),
                  pl.BlockSpec((seq, D), lambda b, r: (b, 1)),
                  pl.BlockSpec((seq, D), lambda b, r: (b, 2), pipeline_mode=once),
                  pl.BlockSpec((n_ctx, D), lambda b, r: (ctx0 + b, 1)),
                  pl.BlockSpec((n_ctx, D), lambda b, r: (ctx0 + b, 2)),
                  pl.BlockSpec(bias.shape, lambda b, r: (0, 0, 0, 0), pipeline_mode=once)],
        out_specs=pl.BlockSpec((NATTEN_ROWS * GRID_W, D), lambda b, r: (b * n_rb + r, 0)),
        out_shape=jax.ShapeDtypeStruct((n_lat, D), BF16),
        compiler_params=_cparams(("arbitrary", "arbitrary")),
        name="natten",
    )(qkv, qkv, qkv, qkv, qkv, bias)
    o_ctx = pl.pallas_call(
        _ctx_attn_kernel,
        grid=(n_batch,),
        in_specs=[pl.BlockSpec((n_ctx, D), lambda b: (ctx0 + b, 0)),
                  pl.BlockSpec((n_ctx, D), lambda b: (ctx0 + b, 1)),
                  pl.BlockSpec((n_ctx, D), lambda b: (ctx0 + b, 2))],
        out_specs=pl.BlockSpec((n_ctx, D), lambda b: (b, 0)),
        out_shape=jax.ShapeDtypeStruct((n_batch * n_ctx, D), BF16),
        compiler_params=_cparams(("arbitrary",)),
        name="ctx_attn",
    )(qkv, qkv, qkv)
    w = w_o.astype(BF16)
    zero_b = jnp.zeros((D,), F32)
    row_fn = lambda t: _mod_row(t, seq // TM_LIN, n_batch)
    hs = _lin_res_call(hs, (o_lat,), (), w, zero_b, mod_l, 0, n_lat // TM_LIN, row_fn)
    return _lin_res_call(hs, (o_ctx,), (), w, zero_b, mod_l, n_lat // TM_LIN, n_batch * n_ctx // TM_LIN, row_fn)


def kernel(x, c, ctx, c_ctx, w_mod, b_mod, w_ff_in, w_ff_out, fnet_w_out, fnet_b_out, conv_w_in, conv_b_in, conv_w_dw, conv_b_dw, conv_ln_g, conv_ln_b, conv_w_out, conv_b_out, na_w_qkv, na_q_gain, na_k_gain, na_rpb, na_w_o):
    n_batch, seq, d = x.shape
    n_ctx = ctx.shape[1]
    depth = w_mod.shape[0]
    assert d == D and n_batch < MOD_ROWS
    assert seq % TM_FFN == 0 and (n_batch * n_ctx) % TM_FFN == 0
    n_lat = n_batch * seq

    cond = jnp.concatenate([c, c_ctx[None, :], jnp.zeros((MOD_ROWS - n_batch - 1, D), F32)], axis=0)
    mod = _mod_call(cond, w_mod, b_mod).reshape(depth, MOD_ROWS, N_MOD, D)
    n_tok = n_lat + n_batch * n_ctx
    n_stream = n_tok + (-n_tok % seq)
    assert depth > 1, "the first half-step is written for a layer that updates the context tokens"
    lat_tiles = n_lat // TM_FFN
    ctx_tiles = n_batch * n_ctx // TM_FFN
    wide_tiles = n_lat // TM_FFN_WIDE
    assert n_lat % TM_FFN_WIDE == 0 and (n_stream - n_lat) % TM_FFN_WIDE == 0
    w_gu, w_o = w_ff_in, w_ff_out

    def half_step(hs, layer, which, k0, with_ctx, in_place):
        assert in_place or not with_ctx
        assert n_batch * n_ctx <= TM_FFN_WIDE <= n_stream - n_lat
        return _ffn_call(hs, mod[layer], w_gu, w_o, layer, which, k0, wide_tiles + (1 if with_ctx else 0),
                         n_batch, seq, tm=TM_FFN_WIDE, in_place=in_place)

    for i in range(depth):
        kind, slot = i % 3, i // 3
        ctx_post = i < depth - 1
        ctx_pre = ctx_post or kind == 2
        mod_l = mod[i]
        if i == 0:
            args = (mod_l, w_gu, w_o, 0, 0, 0)
            hs = _ffn_call(x.reshape(n_lat, D), *args, wide_tiles, n_batch, seq, tm=TM_FFN_WIDE, in_place=False,
                           out_rows=n_stream, zero_tiles=(n_stream - n_lat) // TM_FFN_WIDE)
            hs = _ffn_call(ctx.reshape(n_batch * n_ctx, D), *args, ctx_tiles, n_batch, seq, tm=TM_FFN,
                           tile0=lat_tiles, in_place=False, stream=hs)
        else:
            hs = half_step(hs, i, 0, 0, ctx_pre, True)
        if kind == 0:
            hs = _fourier_mixer(hs, mod_l, fnet_w_out[slot], fnet_b_out[slot], n_batch, seq, n_ctx, ctx_post)
        elif kind == 1:
            assert ctx_post
            hs = _conv_mixer(hs, mod_l, conv_w_in[slot], conv_b_in[slot], conv_w_dw[slot], conv_b_dw[slot],
                             conv_ln_g[slot], conv_ln_b[slot], conv_w_out[slot], conv_b_out[slot],
                             n_batch, seq, n_ctx)
        else:
            assert ctx_post
            hs = _natten_mixer(hs, mod_l, na_w_qkv[slot], na_q_gain[slot], na_k_gain[slot], na_rpb[slot],
                               na_w_o[slot], n_batch, seq, n_ctx)
        final = i == depth - 1
        assert ctx_post != final
        hs = half_step(hs, i, 1, 6, ctx_post, not final)
    return hs[:n_lat].reshape(n_batch, seq, D)
```

```python
import functools

import numpy as np
import jax
import jax.numpy as jnp
from jax import lax
from jax.experimental import pallas as pl
from jax.experimental.pallas import tpu as pltpu

F32 = jnp.float32
BF16 = jnp.bfloat16

D = 1024
D_FF = 2816
N_MOD = 9
EPS = 1e-6
GRID_W = 64
LANE = 128
SUBLANE = 8
N_GROUPS = 4
GROUP_C = D // N_GROUPS
CONV_K = 31
CONV_HALO = 16
N_HEADS = 16
HEAD_DIM = D // N_HEADS
WIN_R = 8
WIN_C = 16
NEG_INF = -1e30
MOD_ROWS = 8
MOD_COLS = 3 * D

FF_TILE = 256
FF_PER_STEP = 2
TM_FFN = 1024
TM_QKV = 2048
PROJ_ROWS = 256
GLU_COLS = 1024
FFN_ROWS = 256
TM_LIN = 1024
TM_CONV = 256
CONV_ROWS = 32
COL_TILE = 16
NATTEN_ROWS = 4
VMEM_LIMIT = 52 * 1024 * 1024
VMEM_LIMIT_WIDE = 58 * 1024 * 1024
TM_FFN_WIDE = 2048


def _cparams(sem):
    return pltpu.CompilerParams(dimension_semantics=sem, vmem_limit_bytes=VMEM_LIMIT)


def _silu(x):
    return x / (1.0 + jnp.exp(-x))


def _norm_mod(x, shift, scale):
    ms = jnp.mean(x * x, axis=-1, keepdims=True)
    return x * lax.rsqrt(ms + EPS) * (1.0 + scale) + shift


def _dot(a, b):
    return jnp.dot(a, b, preferred_element_type=F32)


def _mod_kernel(c_ref, w_ref, b_ref, o_ref):
    s = _silu(c_ref[...]).astype(BF16)
    o_ref[...] = _dot(s, w_ref[...].astype(BF16)) + b_ref[...]


def _mod_call(cond, w_mod, b_mod):
    depth = w_mod.shape[0]
    return pl.pallas_call(
        _mod_kernel,
        grid=(depth, N_MOD * D // MOD_COLS),
        in_specs=[
            pl.BlockSpec((MOD_ROWS, D), lambda l, k: (0, 0)),
            pl.BlockSpec((None, D, MOD_COLS), lambda l, k: (l, 0, k)),
            pl.BlockSpec((None, 1, MOD_COLS), lambda l, k: (l, 0, k)),
        ],
        out_specs=pl.BlockSpec((None, MOD_ROWS, MOD_COLS), lambda l, k: (l, 0, k)),
        out_shape=jax.ShapeDtypeStruct((depth, MOD_ROWS, N_MOD * D), F32),
        compiler_params=_cparams(("arbitrary", "arbitrary")),
        name="mod",
    )(cond, w_mod, b_mod.reshape(depth, 1, N_MOD * D))


def _mod_row(tile, tiles_per_batch, n_batch):
    return jnp.minimum(tile // tiles_per_batch, n_batch)


def _ffn_kernel(*refs, k0, n_ff, n_real, in_stream):
    h_ref, mod_ref = refs[0], refs[2 if in_stream else 1]
    w_refs = refs[3 if in_stream else 2:-2]
    o_ref, xn_ref = refs[-2:]
    wg_refs, wu_refs, wo_refs = (w_refs[t * FF_PER_STEP:(t + 1) * FF_PER_STEP] for t in range(3))
    tokens = lambda rows: h_ref[rows, :]
    j = pl.program_id(1)
    n_chunks = o_ref.shape[0] // FFN_ROWS
    n_steps = pl.cdiv(n_ff, FF_PER_STEP)
    assert n_steps >= 2

    def step(first, last):
        n_sub = n_ff - FF_PER_STEP * (n_steps - 1) if last else FF_PER_STEP
        wg, wu, wo = ([r[...].astype(BF16) for r in w[:n_sub]] for w in (wg_refs, wu_refs, wo_refs))
        gu = {}
        for c in range(n_chunks + 1):
            if c < n_chunks:
                rows = slice(c * FFN_ROWS, (c + 1) * FFN_ROWS)
                if first:
                    xn = _norm_mod(tokens(rows), mod_ref[k0:k0 + 1, :], mod_ref[k0 + 1:k0 + 2, :]).astype(BF16)
                    xn_ref[rows, :] = xn
                else:
                    xn = xn_ref[rows, :]
                gu[c] = [(_dot(xn, wg[t]), _dot(xn, wu[t])) for t in range(n_sub)]
            if c >= 1:
                rows = slice((c - 1) * FFN_ROWS, c * FFN_ROWS)
                y = None
                for t, (g, u) in enumerate(gu.pop(c - 1)):
                    yt = _dot((_silu(g) * u).astype(BF16), wo[t])
                    y = yt if y is None else y + yt
                if first:
                    o_ref[rows, :] = y
                elif last:
                    o_ref[rows, :] = tokens(rows) + (0.5 * mod_ref[k0 + 2:k0 + 3, :]) * (o_ref[rows, :] + y)
                else:
                    o_ref[rows, :] += y

    real = pl.program_id(0) < n_real
    pl.when(jnp.logical_and(real, j == 0))(functools.partial(step, True, False))
    pl.when(jnp.logical_and(real, jnp.logical_and(j > 0, j < n_steps - 1)))(functools.partial(step, False, False))
    pl.when(jnp.logical_and(real, j == n_steps - 1))(functools.partial(step, False, True))

    @pl.when(jnp.logical_and(jnp.logical_not(real), j == n_steps - 1))
    def _():
        o_ref[...] = jnp.zeros_like(o_ref)


def _ffn_call(tok, mod_l, w_gu, w_o, layer, which, k0, n_tiles, n_batch, seq, *, tm, tile0=0, in_place=True,
              stream=None, out_rows=None, zero_tiles=0):
    n_ff = D_FF // FF_TILE
    tpb = seq // tm
    assert not (in_place and stream is not None) and (in_place or stream is not None or tile0 == 0)
    assert zero_tiles == 0 or not in_place
    tok_tile0 = tile0 if in_place else 0
    n_steps = pl.cdiv(n_ff, FF_PER_STEP)

    def ff(t):
        return lambda i, j: jnp.where(i < n_tiles, jnp.minimum(FF_PER_STEP * j + t, n_ff - 1), n_ff - 1)

    subs = range(FF_PER_STEP)
    last_tok = tok.shape[0] // tm - 1
    operands = [tok] + ([stream] if stream is not None else [])
    specs = [pl.BlockSpec((tm, D), lambda i, j: (jnp.minimum(i + tok_tile0, last_tok), 0))]
    if stream is not None:
        specs.append(pl.BlockSpec(memory_space=pl.ANY))
        out_rows = stream.shape[0]
    elif in_place:
        out_rows = tok.shape[0]
    elif out_rows is None:
        out_rows = (n_tiles + zero_tiles) * tm
    mod_row = lambda i: _mod_row(jnp.minimum(i, n_tiles - 1) + tile0, tpb, n_batch)
    return pl.pallas_call(
        functools.partial(_ffn_kernel, k0=k0, n_ff=n_ff, n_real=n_tiles, in_stream=stream is not None),
        grid=(n_tiles + zero_tiles, n_steps),
        in_specs=specs + [pl.BlockSpec((None, N_MOD, D), lambda i, j: (mod_row(i), 0, 0))]
        + [pl.BlockSpec((None, None, D, FF_TILE), lambda i, j, f=ff(t): (layer, which, 0, f(i, j))) for t in subs]
        + [pl.BlockSpec((None, None, D, FF_TILE), lambda i, j, f=ff(t): (layer, which, 0, n_ff + f(i, j)))
           for t in subs]
        + [pl.BlockSpec((None, None, FF_TILE, D), lambda i, j, f=ff(t): (layer, which, f(i, j), 0)) for t in subs],
        out_specs=pl.BlockSpec((tm, D), lambda i, j: (i + tile0, 0)),
        out_shape=jax.ShapeDtypeStruct((out_rows, D), F32),
        scratch_shapes=[pltpu.VMEM((tm, D), BF16)],
        input_output_aliases={0: 0} if in_place else ({1: 0} if stream is not None else {}),
        compiler_params=pltpu.CompilerParams(dimension_semantics=("arbitrary", "arbitrary"),
                                             vmem_limit_bytes=VMEM_LIMIT_WIDE if tm > TM_FFN else VMEM_LIMIT),
        name="ffn",
    )(*operands, mod_l, *([w_gu] * (2 * FF_PER_STEP)), *([w_o] * FF_PER_STEP))


def _lin_res_kernel(*refs, fourier):
    if fourier:
        xr_ref, xi_ref, cc_ref, sc_ref, w_ref, b_ref, h_ref, mod_ref, o_ref = refs
        parts = []
        for g in range(N_GROUPS):
            sl = slice(g * GROUP_C, (g + 1) * GROUP_C)
            parts.append(_dot(xr_ref[:, sl], cc_ref[...]) + _dot(xi_ref[:, sl], sc_ref[...]))
        x = jnp.concatenate(parts, axis=-1).astype(BF16)
    else:
        x_ref, w_ref, b_ref, h_ref, mod_ref, o_ref = refs
        x = x_ref[...]
    y = _dot(x, w_ref[...]) + b_ref[...]
    o_ref[...] = h_ref[...] + mod_ref[5:6, :] * y


def _lin_res_call(hs, xs, tables, w, b, mod_l, tile0, n_tiles, mod_row_fn):
    fourier = len(xs) == 2
    x_spec = pl.BlockSpec((TM_LIN, D), lambda i: (i, 0))
    t_spec = pl.BlockSpec((GROUP_C, GROUP_C), lambda i: (0, 0))
    in_specs = [x_spec] * len(xs) + [t_spec] * len(tables) + [
        pl.BlockSpec((D, D), lambda i: (0, 0)),
        pl.BlockSpec((1, D), lambda i: (0, 0)),
        pl.BlockSpec((TM_LIN, D), lambda i: (i + tile0, 0)),
        pl.BlockSpec((None, N_MOD, D), lambda i: (mod_row_fn(i + tile0), 0, 0)),
    ]
    h_index = len(xs) + len(tables) + 2
    return pl.pallas_call(
        functools.partial(_lin_res_kernel, fourier=fourier),
        grid=(n_tiles,),
        in_specs=in_specs,
        out_specs=pl.BlockSpec((TM_LIN, D), lambda i: (i + tile0, 0)),
        out_shape=jax.ShapeDtypeStruct(hs.shape, F32),
        input_output_aliases={h_index: 0},
        compiler_params=_cparams(("arbitrary",)),
        name="lin_res_fourier" if fourier else "lin_res",
    )(*xs, *tables, w, b.reshape(1, D), hs, mod_l)


def _fnet_rows_kernel(h_ref, mod_ref, kf_ref, zr_ref, zi_ref):
    rows = h_ref.shape[0]
    half = rows // 2
    xn = _norm_mod(h_ref[...], mod_ref[3:4, :], mod_ref[4:5, :])
    n = rows * SUBLANE
    n_cos = (half + 1) * SUBLANE
    zr, zi = [], []
    for c0 in range(0, COL_TILE, SUBLANE):
        a = xn[:, c0:c0 + SUBLANE, :].reshape(n, D).astype(BF16)
        z = _dot(kf_ref[...], a)
        zc = z[:n_cos].reshape(half + 1, SUBLANE, D)
        zs = z[n_cos:].reshape(half - 1, SUBLANE, D)
        zero = jnp.zeros((1, SUBLANE, D), F32)
        zr.append(jnp.concatenate([zc] + [zc[rows - k:rows - k + 1] for k in range(half + 1, rows)], axis=0))
        zi.append(jnp.concatenate([zero, zs, zero] + [-zs[rows - k - 1:rows - k] for k in range(half + 1, rows)],
                                  axis=0))
    zr_ref[...] = jnp.concatenate(zr, axis=1).astype(BF16)
    zi_ref[...] = jnp.concatenate(zi, axis=1).astype(BF16)


def _fnet_cols_kernel(zr_ref, zi_ref, m2_ref, cc_ref, sc_ref, w_ref, b_ref, h_ref, mod_ref, o_ref,
                      xr_ref, xi_ref, y_ref):
    n = zr_ref.shape[1]
    for j in range(SUBLANE):
        z = jnp.concatenate([zr_ref[j], zi_ref[j]], axis=0)
        x = _dot(m2_ref[j], z)
        xr_ref[j * n:(j + 1) * n, :] = x[:n].astype(BF16)
        xi_ref[j * n:(j + 1) * n, :] = x[n:].astype(BF16)
    parts = []
    for g in range(N_GROUPS):
        sl = slice(g * GROUP_C, (g + 1) * GROUP_C)
        parts.append(_dot(xr_ref[:, sl], cc_ref[...]) + _dot(xi_ref[:, sl], sc_ref[...]))
    y = _dot(jnp.concatenate(parts, axis=-1).astype(BF16), w_ref[...]) + b_ref[...]
    for j in range(SUBLANE):
        rows = pl.ds(j, n, stride=SUBLANE)
        for lt in range(D // LANE):
            y_ref[lt, rows, :] = y[j * n:(j + 1) * n, lt * LANE:(lt + 1) * LANE]
    y_t = jnp.concatenate([y_ref[lt] for lt in range(D // LANE)], axis=-1).reshape(n, SUBLANE, D)
    o_ref[...] = h_ref[...] + mod_ref[5:6, :] * y_t


def _fnet_ctx_kernel(h_ref, mod_ref, f_ref, xr_ref, xi_ref):
    n = h_ref.shape[0]
    xn = _norm_mod(h_ref[...], mod_ref[3:4, :], mod_ref[4:5, :]).astype(BF16)
    x = _dot(f_ref[...], xn)
    xr_ref[...] = x[:n].astype(BF16)
    xi_ref[...] = x[n:].astype(BF16)


def _dft_tables(rows):
    n_seq = rows * GRID_W
    a = np.arange(rows)
    th1 = 2.0 * np.pi * np.outer(a, a) / rows
    half = rows // 2
    f1 = np.concatenate([np.cos(th1)[:half + 1], -np.sin(th1)[1:half]], axis=0)
    f1 = np.kron(f1, np.eye(SUBLANE))
    k2 = np.arange(GRID_W)[:, None, None]
    k1 = np.arange(rows)[None, :, None]
    n1 = np.arange(GRID_W)[None, None, :]
    th2 = 2.0 * np.pi * (n1 * (GRID_W * k1 + k2) % n_seq) / n_seq
    c2, s2 = np.cos(th2), np.sin(th2)
    m2 = np.concatenate([np.concatenate([c2, s2], axis=2), np.concatenate([-s2, c2], axis=2)], axis=1)
    return f1.astype(np.float32), m2.astype(np.float32)


def _channel_tables(n_seq):
    c = np.arange(GROUP_C)
    th = 2.0 * np.pi * np.outer(c, c) / GROUP_C
    s = 1.0 / np.sqrt(float(n_seq) * GROUP_C)
    return (np.cos(th) * s).astype(np.float32), (np.sin(th) * s).astype(np.float32)


def _fourier_mixer(hs, mod_l, w_out, b_out, n_batch, seq, n_ctx, with_ctx):
    rows = seq // GRID_W
    assert rows == GRID_W, "the two-stage DFT is written for a square token grid"
    n_lat = n_batch * seq
    f1, m2 = _dft_tables(rows)
    cc, sc = _channel_tables(seq)
    f1, m2, cc, sc = (jnp.asarray(t).astype(BF16) for t in (f1, m2, cc, sc))
    w = w_out.astype(BF16)
    n_ct = GRID_W // COL_TILE
    grid3 = hs.reshape(hs.shape[0] // GRID_W, GRID_W, D)
    z_shape = jax.ShapeDtypeStruct((n_batch * rows, GRID_W, D), BF16)
    blk_cols = pl.BlockSpec((rows, COL_TILE, D), lambda b, c: (b, c, 0))
    mod_b = pl.BlockSpec((None, N_MOD, D), lambda b, c: (b, 0, 0))
    zr, zi = pl.pallas_call(
        _fnet_rows_kernel,
        grid=(n_batch, n_ct),
        in_specs=[blk_cols, mod_b, pl.BlockSpec(f1.shape, lambda b, c: (0, 0))],
        out_specs=[blk_cols, blk_cols],
        out_shape=[z_shape, z_shape],
        compiler_params=_cparams(("arbitrary", "arbitrary")),
        name="fnet_rows",
    )(grid3, mod_l, f1)
    n_kt = GRID_W // SUBLANE
    blk_k2 = pl.BlockSpec((SUBLANE, GRID_W, D), lambda b, k: (b * n_kt + k, 0, 0))
    blk_tok = pl.BlockSpec((rows, SUBLANE, D), lambda b, k: (b, k, 0))
    table = pl.BlockSpec((GROUP_C, GROUP_C), lambda b, k: (0, 0))
    n_tok = rows * SUBLANE
    grid3 = pl.pallas_call(
        _fnet_cols_kernel,
        grid=(n_batch, n_kt),
        in_specs=[blk_k2, blk_k2,
                  pl.BlockSpec((SUBLANE, 2 * rows, 2 * GRID_W), lambda b, k: (k, 0, 0)),
                  table, table,
                  pl.BlockSpec((D, D), lambda b, k: (0, 0)),
                  pl.BlockSpec((1, D), lambda b, k: (0, 0)),
                  blk_tok, mod_b],
        out_specs=blk_tok,
        out_shape=jax.ShapeDtypeStruct(grid3.shape, F32),
        scratch_shapes=[pltpu.VMEM((n_tok, D), BF16), pltpu.VMEM((n_tok, D), BF16),
                        pltpu.VMEM((D // LANE, n_tok, LANE), F32)],
        input_output_aliases={7: 0},
        compiler_params=_cparams(("arbitrary", "arbitrary")),
        name="fnet_cols",
    )(zr, zi, m2, cc, sc, w, b_out.reshape(1, D), grid3, mod_l)
    hs = grid3.reshape(hs.shape)
    tpb = seq // TM_LIN
    row_fn = lambda t: _mod_row(t, tpb, n_batch)
    if with_ctx:
        a = np.arange(n_ctx)
        th = 2.0 * np.pi * np.outer(a, a) / n_ctx
        s = np.sqrt(float(seq) / n_ctx)
        fc = jnp.asarray(np.concatenate([np.cos(th) * s, -np.sin(th) * s], axis=0).astype(np.float32)).astype(BF16)
        c_shape = jax.ShapeDtypeStruct((n_batch * n_ctx, D), BF16)
        tile0 = n_lat // n_ctx
        xr, xi = pl.pallas_call(
            _fnet_ctx_kernel,
            grid=(n_batch,),
            in_specs=[pl.BlockSpec((n_ctx, D), lambda b: (tile0 + b, 0)),
                      pl.BlockSpec((None, N_MOD, D), lambda b: (n_batch, 0, 0)),
                      pl.BlockSpec((2 * n_ctx, n_ctx), lambda b: (0, 0))],
            out_specs=[pl.BlockSpec((n_ctx, D), lambda b: (b, 0))] * 2,
            out_shape=[c_shape, c_shape],
            compiler_params=_cparams(("arbitrary",)),
            name="fnet_ctx",
        )(hs, mod_l, fc)
        hs = _lin_res_call(hs, (xr, xi), (cc, sc), w, b_out, mod_l,
                           n_lat // TM_LIN, n_batch * n_ctx // TM_LIN, row_fn)
    return hs


def _mixer_input_chunks(h_ref, mod_ref, xn_ref, first):
    for r0 in range(0, h_ref.shape[0], PROJ_ROWS):
        rows = slice(r0, r0 + PROJ_ROWS)
        if first:
            xn = _norm_mod(h_ref[rows, :], mod_ref[3:4, :], mod_ref[4:5, :]).astype(BF16)
            xn_ref[rows, :] = xn
        else:
            xn = xn_ref[rows, :]
        yield rows, xn


def _glu_kernel(h_ref, mod_ref, wa_ref, wg_ref, ba_ref, bg_ref, o_ref, xn_ref):
    def step(first):
        for rows, xn in _mixer_input_chunks(h_ref, mod_ref, xn_ref, first):
            a = _dot(xn, wa_ref[...]) + ba_ref[...]
            g = _dot(xn, wg_ref[...]) + bg_ref[...]
            o_ref[rows, :] = a / (1.0 + jnp.exp(-g))

    j = pl.program_id(1)
    pl.when(j == 0)(functools.partial(step, True))
    pl.when(j > 0)(functools.partial(step, False))


def _conv_kernel(u_ref, up_ref, un_ref, wdw_ref, bdw_ref, lng_ref, lnb_ref, w_ref, b_ref, h_ref, mod_ref,
                 o_ref, buf_ref, conv_ref, v_ref, *, lat_tiles, tiles_per_seq):
    i = pl.program_id(0)
    is_ctx = i >= lat_tiles
    first = jnp.logical_or(is_ctx, i % tiles_per_seq == 0)
    last = jnp.logical_or(is_ctx, i % tiles_per_seq == tiles_per_seq - 1)
    tm = u_ref.shape[0]
    n_buf = tm + 2 * CONV_HALO
    buf_ref[0, 0:CONV_HALO, :] = jnp.where(first, 0.0, up_ref[...])
    buf_ref[0, CONV_HALO:CONV_HALO + tm, :] = u_ref[...]
    buf_ref[0, CONV_HALO + tm:, :] = jnp.where(last, 0.0, un_ref[...])
    for s in range(1, SUBLANE):
        buf_ref[s] = pltpu.roll(buf_ref[0], n_buf - s, axis=0)
    off = CONV_HALO - CONV_K // 2
    for lt in range(D // LANE):
        lanes = slice(lt * LANE, (lt + 1) * LANE)
        for c0 in range(0, tm, CONV_ROWS):
            acc = jnp.zeros((CONV_ROWS, LANE), F32) + bdw_ref[:, lanes]
            for k in range(CONV_K):
                row = c0 + (off + k) // SUBLANE * SUBLANE
                acc = acc + wdw_ref[k:k + 1, lanes] * buf_ref[(off + k) % SUBLANE, row:row + CONV_ROWS, lanes]
            conv_ref[c0:c0 + CONV_ROWS, lanes] = acc
    chunk = 32
    for c0 in range(0, tm, chunk):
        acc = conv_ref[c0:c0 + chunk, :]
        mu = jnp.mean(acc, axis=-1, keepdims=True)
        d = acc - mu
        var = jnp.mean(d * d, axis=-1, keepdims=True)
        v = d * lax.rsqrt(var + EPS) * lng_ref[...] + lnb_ref[...]
        v_ref[c0:c0 + chunk, :] = _silu(v).astype(BF16)
    y = _dot(v_ref[...], w_ref[...]) + b_ref[...]
    o_ref[...] = h_ref[...] + mod_ref[5:6, :] * y


def _conv_mixer(hs, mod_l, w_in, b_in, w_dw, b_dw, ln_g, ln_b, w_out, b_out, n_batch, seq, n_ctx):
    n_lat = n_batch * seq
    n_all = n_lat + n_batch * n_ctx
    assert n_ctx == TM_CONV, "context sequences are one conv tile long"
    n_tiles = n_all // TM_FFN
    n_col = D // GLU_COLS
    tpb = seq // TM_FFN
    col = pl.BlockSpec((D, GLU_COLS), lambda i, j: (0, j))
    gcol = pl.BlockSpec((D, GLU_COLS), lambda i, j: (0, n_col + j))
    u = pl.pallas_call(
        _glu_kernel,
        grid=(n_tiles, n_col),
        in_specs=[pl.BlockSpec((TM_FFN, D), lambda i, j: (i, 0)),
                  pl.BlockSpec((None, N_MOD, D), lambda i, j: (_mod_row(i, tpb, n_batch), 0, 0)),
                  col, gcol,
                  pl.BlockSpec((1, GLU_COLS), lambda i, j: (0, j)),
                  pl.BlockSpec((1, GLU_COLS), lambda i, j: (0, n_col + j))],
        out_specs=pl.BlockSpec((TM_FFN, GLU_COLS), lambda i, j: (i, j)),
        out_shape=jax.ShapeDtypeStruct((n_all, D), F32),
        scratch_shapes=[pltpu.VMEM((TM_FFN, D), BF16)],
        compiler_params=_cparams(("arbitrary", "arbitrary")),
        name="conv_glu",
    )(hs, mod_l, w_in.astype(BF16), w_in.astype(BF16), b_in.reshape(1, 2 * D), b_in.reshape(1, 2 * D))

    n_ct = n_all // TM_CONV
    hpt = TM_CONV // CONV_HALO
    n_halo = n_all // CONV_HALO
    tps = seq // TM_CONV
    vec = pl.BlockSpec((1, D), lambda i: (0, 0))
    tile = pl.BlockSpec((TM_CONV, D), lambda i: (i, 0))
    return pl.pallas_call(
        functools.partial(_conv_kernel, lat_tiles=n_lat // TM_CONV, tiles_per_seq=tps),
        grid=(n_ct,),
        in_specs=[tile,
                  pl.BlockSpec((CONV_HALO, D), lambda i: (jnp.maximum(i * hpt - 1, 0), 0)),
                  pl.BlockSpec((CONV_HALO, D), lambda i: (jnp.minimum((i + 1) * hpt, n_halo - 1), 0)),
                  pl.BlockSpec((CONV_K, D), lambda i: (0, 0)),
                  vec, vec, vec,
                  pl.BlockSpec((D, D), lambda i: (0, 0)),
                  vec, tile,
                  pl.BlockSpec((None, N_MOD, D), lambda i: (_mod_row(i, tps, n_batch), 0, 0))],
        out_specs=tile,
        out_shape=jax.ShapeDtypeStruct(hs.shape, F32),
        scratch_shapes=[pltpu.VMEM((SUBLANE, TM_CONV + 2 * CONV_HALO, D), F32), pltpu.VMEM((TM_CONV, D), F32),
                        pltpu.VMEM((TM_CONV, D), BF16)],
        input_output_aliases={9: 0},
        compiler_params=_cparams(("arbitrary",)),
        name="conv_dw",
    )(u, u, u, w_dw, b_dw.reshape(1, D), ln_g.reshape(1, D), ln_b.reshape(1, D), w_out.astype(BF16),
      b_out.reshape(1, D), hs, mod_l)


def _qkv_kernel(h_ref, mod_ref, w_ref, gain_ref, e_ref, o_ref, xn_ref):
    n_e = e_ref.shape[0]

    def step(first, head_norm):
        for rows, xn in _mixer_input_chunks(h_ref, mod_ref, xn_ref, first):
            y = _dot(xn, w_ref[...])
            if not head_norm:
                o_ref[rows, :] = y.astype(BF16)
                continue
            for c0 in range(0, D, n_e):
                cols = slice(c0, c0 + n_e)
                yc = y[:, cols]
                ss = _dot((yc * yc).astype(BF16), e_ref[...])
                rs = lax.rsqrt(ss * (1.0 / HEAD_DIM) + EPS) * gain_ref[:, cols]
                o_ref[rows, cols] = (yc * rs).astype(BF16)

    j = pl.program_id(1)
    pl.when(j == 0)(functools.partial(step, True, True))
    pl.when(j == 1)(functools.partial(step, False, True))
    pl.when(j == 2)(functools.partial(step, False, False))


def _softmax_pv(s_list, v_list):
    m = s_list[0].max(axis=-1, keepdims=True)
    for s in s_list[1:]:
        m = jnp.maximum(m, s.max(axis=-1, keepdims=True))
    den = 0.0
    acc = 0.0
    for s, v in zip(s_list, v_list):
        p = jnp.exp(s - m)
        den = den + p.sum(axis=-1, keepdims=True)
        acc = acc + _dot(p.astype(BF16), v)
    return acc / den


def _qk(q, k):
    return lax.dot_general(q, k, (((1,), (1,)), ((), ())), preferred_element_type=F32)


def _natten_kernel(q_ref, k_ref, v_ref, kc_ref, vc_ref, bias_ref, o_ref, *, rows, kr):
    r0 = pl.program_id(1) * NATTEN_ROWS
    left = lax.broadcasted_iota(jnp.int32, (GRID_W, LANE), 1) < HEAD_DIM
    n2 = 2 * GRID_W
    wins, dr0s = [], []
    for i in range(NATTEN_ROWS):
        rs = jnp.clip(r0 + i - kr // 2, 0, rows - kr)
        wins.append(pl.ds(pl.multiple_of(rs * GRID_W, GRID_W), kr * GRID_W))
        dr0s.append(rs - (r0 + i) + WIN_R - 1)
    for p in range(N_HEADS // 2):
        cols = slice(p * LANE, (p + 1) * LANE)
        q2 = []
        for i in range(NATTEN_ROWS):
            q = q_ref[i * GRID_W:(i + 1) * GRID_W, cols]
            zero = jnp.zeros_like(q)
            q2 += [jnp.where(left, q, zero), jnp.where(left, zero, q)]
        q2 = jnp.concatenate(q2, axis=0)
        s_ctx = _qk(q2, kc_ref[:, cols])
        m_ctx = s_ctx.max(axis=-1, keepdims=True)
        m, den, acc = [], [], []
        for i in range(NATTEN_ROWS):
            mine = slice(i * n2, (i + 1) * n2)
            bias = jnp.concatenate([bias_ref[p, dr0s[i] + 2 * t] for t in range(kr // 2)], axis=-1)
            s_lat = _qk(q2[mine], k_ref[wins[i], cols]) + bias
            m.append(jnp.maximum(s_lat.max(axis=-1, keepdims=True), m_ctx[mine]))
            p_lat = jnp.exp(s_lat - m[i])
            den.append(p_lat.sum(axis=-1, keepdims=True))
            acc.append(_dot(p_lat.astype(BF16), v_ref[wins[i], cols]))
        p_ctx = jnp.exp(s_ctx - jnp.concatenate(m, axis=0))
        den_ctx = p_ctx.sum(axis=-1, keepdims=True)
        acc_ctx = _dot(p_ctx.astype(BF16), vc_ref[:, cols])
        for i in range(NATTEN_ROWS):
            mine = slice(i * n2, (i + 1) * n2)
            o = (acc[i] + acc_ctx[mine]) / (den[i] + den_ctx[mine])
            o_ref[i * GRID_W:(i + 1) * GRID_W, cols] = jnp.where(left, o[:GRID_W], o[GRID_W:]).astype(BF16)


def _bias_kernel(rpb_ref, o_ref):
    qc = lax.broadcasted_iota(jnp.int32, (GRID_W, LANE), 0)
    lane = lax.broadcasted_iota(jnp.int32, (GRID_W, LANE), 1)
    kc = lane % GRID_W
    w0 = jnp.clip(qc - WIN_C // 2, 0, GRID_W - WIN_C)
    ok = jnp.logical_and(kc >= w0, kc < w0 + WIN_C)
    sub = lax.broadcasted_iota(jnp.int32, (SUBLANE, LANE), 0)
    tiles = []
    for dr in range(2 * WIN_R - 1):
        t = jnp.broadcast_to(rpb_ref[dr:dr + 1, :], (SUBLANE, LANE))
        t = pltpu.roll(t, LANE - (WIN_C - 1), axis=1)
        for bit in range(SUBLANE.bit_length() - 1):
            t = jnp.where((sub >> bit) & 1 == 1, pltpu.roll(t, 1 << bit, axis=1), t)
        tiles.append(jnp.concatenate([t] + [pltpu.roll(t, SUBLANE * a, axis=1)
                                            for a in range(1, GRID_W // SUBLANE)], axis=0))
    for p in range(2 * WIN_R - 2):
        pair = jnp.where(lane < GRID_W, tiles[p], pltpu.roll(tiles[p + 1], GRID_W, axis=1))
        o_ref[p] = jnp.where(ok, pair, NEG_INF)


def _ctx_attn_kernel(q_ref, k_ref, v_ref, o_ref):
    outs = []
    for h in range(N_HEADS):
        hd = slice(h * HEAD_DIM, (h + 1) * HEAD_DIM)
        outs.append(_softmax_pv([_qk(q_ref[:, hd], k_ref[:, hd])], [v_ref[:, hd]]))
    o_ref[...] = jnp.concatenate(outs, axis=-1).astype(BF16)


def _natten_bias(rpb):
    n_dr, n_dc = 2 * WIN_R - 1, 2 * WIN_C - 1
    assert rpb.shape == (N_HEADS, n_dr, n_dc) and GRID_W * 2 == LANE
    padded = jnp.pad(rpb.astype(F32), ((0, 0), (0, 0), (0, LANE - n_dc)))
    return pl.pallas_call(
        _bias_kernel,
        grid=(N_HEADS,),
        in_specs=[pl.BlockSpec((None, n_dr, LANE), lambda h: (h, 0, 0))],
        out_specs=pl.BlockSpec((None, n_dr - 1, GRID_W, LANE), lambda h: (h // 2, 0, h % 2, 0)),
        out_shape=jax.ShapeDtypeStruct((N_HEADS // 2, n_dr - 1, 2 * GRID_W, LANE), F32),
        compiler_params=_cparams(("arbitrary",)),
        name="natten_bias",
    )(padded)


def _natten_mixer(hs, mod_l, w_qkv, q_gain, k_gain, rpb, w_o, n_batch, seq, n_ctx):
    n_lat = n_batch * seq
    n_all = n_lat + n_batch * n_ctx
    rows = seq // GRID_W
    kr = min(WIN_R, rows)
    n_tiles = pl.cdiv(n_all, TM_QKV)
    tpb = seq // TM_QKV
    scale = HEAD_DIM ** -0.5
    gain = jnp.concatenate([jnp.tile(q_gain, N_HEADS) * scale, jnp.tile(k_gain, N_HEADS),
                            jnp.ones((D,), F32)]).reshape(1, 3 * D)
    lane = np.arange(FF_TILE)
    e = jnp.asarray((lane[:, None] // HEAD_DIM == lane[None, :] // HEAD_DIM).astype(np.float32)).astype(BF16)
    qkv = pl.pallas_call(
        _qkv_kernel,
        grid=(n_tiles, 3),
        in_specs=[pl.BlockSpec((TM_QKV, D), lambda i, j: (i, 0)),
                  pl.BlockSpec((None, N_MOD, D), lambda i, j: (_mod_row(i, tpb, n_batch), 0, 0)),
                  pl.BlockSpec((D, D), lambda i, j: (0, j)),
                  pl.BlockSpec((1, D), lambda i, j: (0, j)),
                  pl.BlockSpec((FF_TILE, FF_TILE), lambda i, j: (0, 0))],
        out_specs=pl.BlockSpec((TM_QKV, D), lambda i, j: (i, j)),
        out_shape=jax.ShapeDtypeStruct((n_tiles * TM_QKV, 3 * D), BF16),
        scratch_shapes=[pltpu.VMEM((TM_QKV, D), BF16)],
        compiler_params=_cparams(("arbitrary", "arbitrary")),
        name="qkv",
    )(hs, mod_l, w_qkv.astype(BF16), gain, e)

    assert kr == WIN_R, "bias tiles pair up the key rows of a full window"
    bias = _natten_bias(rpb)
    ctx0 = n_lat // n_ctx
    once = pl.Buffered(1)
    n_rb = rows // NATTEN_ROWS
    o_lat = pl.pallas_call(
        functools.partial(_natten_kernel, rows=rows, kr=kr),
        grid=(n_batch, n_rb),
        in_specs=[pl.BlockSpec((NATTEN_ROWS * GRID_W, D), lambda b, r: (b * n_rb + r, 0)),
                  pl.BlockSpec((seq, D), lambda b, r: (b, 1)),
                  pl.BlockSpec((seq, D), lambda b, r: (b, 2), pipeline_mode=once),
                  pl.BlockSpec((n_ctx, D), lambda b, r: (ctx0 + b, 1)),
                  pl.BlockSpec((n_ctx, D), lambda b, r: (ctx0 + b, 2)),
                  pl.BlockSpec(bias.shape, lambda b, r: (0, 0, 0, 0), pipeline_mode=once)],
        out_specs=pl.BlockSpec((NATTEN_ROWS * GRID_W, D), lambda b, r: (b * n_rb + r, 0)),
        out_shape=jax.ShapeDtypeStruct((n_lat, D), BF16),
        compiler_params=_cparams(("arbitrary", "arbitrary")),
        name="natten",
    )(qkv, qkv, qkv, qkv, qkv, bias)
    o_ctx = pl.pallas_call(
        _ctx_attn_kernel,
        grid=(n_batch,),
        in_specs=[pl.BlockSpec((n_ctx, D), lambda b: (ctx0 + b, 0)),
                  pl.BlockSpec((n_ctx, D), lambda b: (ctx0 + b, 1)),
                  pl.BlockSpec((n_ctx, D), lambda b: (ctx0 + b, 2))],
        out_specs=pl.BlockSpec((n_ctx, D), lambda b: (b, 0)),
        out_shape=jax.ShapeDtypeStruct((n_batch * n_ctx, D), BF16),
        compiler_params=_cparams(("arbitrary",)),
        name="ctx_attn",
    )(qkv, qkv, qkv)
    w = w_o.astype(BF16)
    zero_b = jnp.zeros((D,), F32)
    row_fn = lambda t: _mod_row(t, seq // TM_LIN, n_batch)
    hs = _lin_res_call(hs, (o_lat,), (), w, zero_b, mod_l, 0, n_lat // TM_LIN, row_fn)
    return _lin_res_call(hs, (o_ctx,), (), w, zero_b, mod_l, n_lat // TM_LIN, n_batch * n_ctx // TM_LIN, row_fn)


def kernel(x, c, ctx, c_ctx, w_mod, b_mod, w_ff_in, w_ff_out, fnet_w_out, fnet_b_out, conv_w_in, conv_b_in, conv_w_dw, conv_b_dw, conv_ln_g, conv_ln_b, conv_w_out, conv_b_out, na_w_qkv, na_q_gain, na_k_gain, na_rpb, na_w_o):
    n_batch, seq, d = x.shape
    n_ctx = ctx.shape[1]
    depth = w_mod.shape[0]
    assert d == D and n_batch < MOD_ROWS
    assert seq % TM_FFN == 0 and (n_batch * n_ctx) % TM_FFN == 0
    n_lat = n_batch * seq

    cond = jnp.concatenate([c, c_ctx[None, :], jnp.zeros((MOD_ROWS - n_batch - 1, D), F32)], axis=0)
    mod = _mod_call(cond, w_mod, b_mod).reshape(depth, MOD_ROWS, N_MOD, D)
    n_tok = n_lat + n_batch * n_ctx
    n_stream = n_tok + (-n_tok % seq)
    assert depth > 1, "the first half-step is written for a layer that updates the context tokens"
    lat_tiles = n_lat // TM_FFN
    ctx_tiles = n_batch * n_ctx // TM_FFN
    wide_tiles = n_lat // TM_FFN_WIDE
    assert n_lat % TM_FFN_WIDE == 0 and (n_stream - n_lat) % TM_FFN_WIDE == 0
    w_gu, w_o = w_ff_in, w_ff_out

    def half_step(hs, layer, which, k0, with_ctx, in_place):
        args = (mod[layer], w_gu, w_o, layer, which, k0)
        hs = _ffn_call(hs, *args, wide_tiles, n_batch, seq, tm=TM_FFN_WIDE, in_place=in_place)
        if with_ctx:
            hs = _ffn_call(hs, *args, ctx_tiles, n_batch, seq, tm=TM_FFN, tile0=lat_tiles)
        return hs

    for i in range(depth):
        kind, slot = i % 3, i // 3
        ctx_post = i < depth - 1
        ctx_pre = ctx_post or kind == 2
        mod_l = mod[i]
        if i == 0:
            args = (mod_l, w_gu, w_o, 0, 0, 0)
            hs = _ffn_call(x.reshape(n_lat, D), *args, wide_tiles, n_batch, seq, tm=TM_FFN_WIDE, in_place=False,
                           out_rows=n_stream, zero_tiles=(n_stream - n_lat) // TM_FFN_WIDE)
            hs = _ffn_call(ctx.reshape(n_batch * n_ctx, D), *args, ctx_tiles, n_batch, seq, tm=TM_FFN,
                           tile0=lat_tiles, in_place=False, stream=hs)
        else:
            hs = half_step(hs, i, 0, 0, ctx_pre, True)
        if kind == 0:
            hs = _fourier_mixer(hs, mod_l, fnet_w_out[slot], fnet_b_out[slot], n_batch, seq, n_ctx, ctx_post)
        elif kind == 1:
            assert ctx_post
            hs = _conv_mixer(hs, mod_l, conv_w_in[slot], conv_b_in[slot], conv_w_dw[slot], conv_b_dw[slot],
                             conv_ln_g[slot], conv_ln_b[slot], conv_w_out[slot], conv_b_out[slot],
                             n_batch, seq, n_ctx)
        else:
            assert ctx_post
            hs = _natten_mixer(hs, mod_l, na_w_qkv[slot], na_q_gain[slot], na_k_gain[slot], na_rpb[slot],
                               na_w_o[slot], n_batch, seq, n_ctx)
        final = i == depth - 1
        assert ctx_post != final
        hs = half_step(hs, i, 1, 6, ctx_post, not final)
    return hs[:n_lat].reshape(n_batch, seq, D)
```
